```python
import jax, jax.numpy as jnp
from jax import lax
import numpy as np

D_MODEL = 1024
BATCH = 1
SEQ = 16384
DEPTH = 1
DEC_BATCH = 128
DEC_SEQ = 8
PAST_LEN = 16384
PAGE_SIZE = 128

N_Q_HEADS = 8
N_KV_HEADS = 2
GROUP = N_Q_HEADS // N_KV_HEADS
HEAD_DIM = 64
ATTN_Q = N_Q_HEADS * HEAD_DIM
ATTN_KV = N_KV_HEADS * HEAD_DIM
WINDOW = 128
ROPE_THETA = 10000.0
CHUNK = 128
SG_GROUPS = 4
SG_GROUP_DIM = 128
SG_WIDTH = SG_GROUPS * SG_GROUP_DIM
N_MEM = 256
MEM_HEADS = 4
MEM_HEAD_DIM = 128
MEM_Q = MEM_HEADS * MEM_HEAD_DIM
N_BRANCHES = 3
BRANCH_WIDTH = 512
D_FF = 2816
CONV_WIDTH = 3
EPS = 1e-6
NEG_INF = -1e30
IN_SPLITS = (ATTN_Q, ATTN_KV, ATTN_KV, SG_WIDTH, SG_WIDTH, MEM_Q, N_BRANCHES * D_MODEL)
IN_WIDTH = sum(IN_SPLITS)
IN_OFFSETS = tuple(int(o) for o in np.cumsum(IN_SPLITS)[:-1])

kernel_name = 'gated_parallel_swa_sgmlp_memory_decoder_step'


def rms_norm(x, g):
    xf = x.astype(jnp.float32)
    y = xf * lax.rsqrt(jnp.mean(jnp.square(xf), axis=-1, keepdims=True) + EPS)
    return (y * g.astype(jnp.float32)).astype(x.dtype)


def layer_norm(x, g, b):
    xf = x.astype(jnp.float32)
    xc = xf - jnp.mean(xf, axis=-1, keepdims=True)
    var = jnp.mean(jnp.square(xc), axis=-1, keepdims=True)
    return (xc * lax.rsqrt(var + EPS) * g.astype(jnp.float32) + b.astype(jnp.float32)).astype(x.dtype)


def rope(x, pos):
    half = x.shape[-1] // 2
    inv_freq = ROPE_THETA ** (-jnp.arange(half, dtype=jnp.float32) / half)
    ang = pos.astype(jnp.float32)[:, None] * inv_freq[None, :]
    cos = jnp.cos(ang)[:, None, :]
    sin = jnp.sin(ang)[:, None, :]
    xf = x.astype(jnp.float32)
    x1, x2 = xf[..., :half], xf[..., half:]
    return jnp.concatenate([x1 * cos - x2 * sin, x2 * cos + x1 * sin], axis=-1).astype(x.dtype)


def sink_attention(q, k, v, mask, sinks):
    s = jnp.einsum('...qhgd,...khd->...hgqk', q, k, preferred_element_type=jnp.float32) * (HEAD_DIM ** -0.5)
    s = jnp.where(mask, s, NEG_INF)
    sink = jnp.broadcast_to(sinks.astype(jnp.float32).reshape(N_KV_HEADS, GROUP, 1, 1), s.shape[:-1] + (1,))
    p = jax.nn.softmax(jnp.concatenate([s, sink], axis=-1), axis=-1)[..., :-1]
    return jnp.einsum('...hgqk,...khd->...qhgd', p.astype(v.dtype), v)


def swa_band(q, k, v, sinks):
    B, L = q.shape[:2]
    blk = WINDOW
    nb = L // blk
    qb = q.reshape(B, nb, blk, N_KV_HEADS, GROUP, HEAD_DIM)

    def band(t):
        tb = jnp.pad(t, ((0, 0), (blk, 0), (0, 0), (0, 0))).reshape(B, nb + 1, blk, N_KV_HEADS, HEAD_DIM)
        return jnp.concatenate([tb[:, :-1], tb[:, 1:]], axis=2)

    kb, vb = band(k), band(v)
    qi = jnp.arange(blk)[:, None] + blk
    kj = jnp.arange(2 * blk)[None, :]
    rel = qi - kj
    in_window = (rel >= 0) & (rel < WINDOW)
    real_key = (jnp.arange(nb)[:, None, None] * blk + kj[None] - blk) >= 0
    mask = (in_window[None] & real_key)[:, None, None]
    o = sink_attention(qb, kb, vb, mask, sinks)
    return o.reshape(B, L, ATTN_Q)


def spatial_gating(u, vn, sg_w, sg_b, chunk):
    B, L, _ = u.shape
    nc = L // chunk
    u5 = u.reshape(B, nc, chunk, SG_GROUPS, SG_GROUP_DIM)
    v5 = vn.reshape(B, nc, chunk, SG_GROUPS, SG_GROUP_DIM)
    w = jnp.tril(sg_w[:, :chunk, :chunk])
    mixed = jnp.einsum('gts,bnsgc->bntgc', w, v5) + sg_b[:, :chunk].T[:, :, None]
    return (u5 * mixed).reshape(B, L, SG_WIDTH)


def memory_kv(mem, g, w_mem_kv):
    B = mem.shape[0]
    kv = jnp.einsum('bmd,de->bme', rms_norm(mem, g), w_mem_kv)
    k, v = jnp.split(kv, 2, axis=-1)
    return (k.reshape(B, -1, MEM_HEADS, MEM_HEAD_DIM), v.reshape(B, -1, MEM_HEADS, MEM_HEAD_DIM))


def mem_attention(q, mk, mv):
    s = jnp.einsum('blhd,bmhd->bhlm', q, mk, preferred_element_type=jnp.float32) * (MEM_HEAD_DIM ** -0.5)
    p = jax.nn.softmax(s, axis=-1)
    return jnp.einsum('bhlm,bmhd->blhd', p.astype(mv.dtype), mv)


def conv_ffn(h, conv_past, w_up, conv_w, conv_b, w_down):
    L = h.shape[1]
    up = jnp.einsum('bld,df->blf', h, w_up)
    ext = jnp.concatenate([conv_past.astype(up.dtype), up], axis=1)
    c = conv_b
    for j in range(CONV_WIDTH):
        c = c + ext[:, j:j + L] * conv_w[j]
    gate, val = jnp.split(c, 2, axis=-1)
    act = jax.nn.gelu(gate, approximate=True) * val
    return jnp.einsum('blf,fd->bld', act, w_down), ext[:, L:]


def decoder_layer(x, start, win_k, win_v, mem_k, mem_v, conv_past, lp):
    B, L, _ = x.shape
    pos = start + jnp.arange(L, dtype=jnp.int32)
    h = rms_norm(x, lp['pre_mix_g'])
    z = jnp.einsum('bld,de->ble', h, lp['w_in'])
    q, k, v, sg_u, sg_v, mq, gate_logits = jnp.split(z, IN_OFFSETS, axis=-1)
    q = rope(q.reshape(B, L, N_Q_HEADS, HEAD_DIM), pos).reshape(B, L, N_KV_HEADS, GROUP, HEAD_DIM)
    k = rope(k.reshape(B, L, N_KV_HEADS, HEAD_DIM), pos)
    v = v.reshape(B, L, N_KV_HEADS, HEAD_DIM)
    if win_k is None:
        attn = swa_band(q, k, v, lp['sinks'])
        wb = min(WINDOW, L)
        new_wk, new_wv = k[:, L - wb:], v[:, L - wb:]
        chunk = CHUNK
        conv_past = jnp.zeros((B, CONV_WIDTH - 1, 2 * D_FF), x.dtype)
    else:
        wb = win_k.shape[1]
        kk = jnp.concatenate([win_k.astype(k.dtype), k], axis=1)
        vv = jnp.concatenate([win_v.astype(v.dtype), v], axis=1)
        kpos = start - wb + jnp.arange(wb + L, dtype=jnp.int32)
        rel = pos[:, None] - kpos[None, :]
        mask = (rel >= 0) & (rel < WINDOW)
        attn = sink_attention(q, kk, vv, mask, lp['sinks']).reshape(B, L, ATTN_Q)
        new_wk, new_wv = kk[:, L:], vv[:, L:]
        chunk = L
    u = jax.nn.gelu(sg_u, approximate=False)
    vn = layer_norm(jax.nn.gelu(sg_v, approximate=False), lp['sg_ln_g'], lp['sg_ln_b'])
    sg = spatial_gating(u, vn, lp['sg_w'], lp['sg_b'], chunk)
    memo = mem_attention(mq.reshape(B, L, MEM_HEADS, MEM_HEAD_DIM), mem_k.astype(x.dtype), mem_v.astype(x.dtype))
    branches = jnp.stack([attn, sg, memo.reshape(B, L, MEM_Q)], axis=2)
    proj = jnp.einsum('blnc,ncd->blnd', branches, lp['w_o'])
    gates = jax.nn.sigmoid(gate_logits.reshape(B, L, N_BRANCHES, D_MODEL))
    mixed = jnp.sum(gates * proj, axis=2)
    x = x + rms_norm(mixed, lp['post_mix_g'])
    f, new_conv = conv_ffn(rms_norm(x, lp['pre_ffn_g']), conv_past, lp['w_up'], lp['conv_w'], lp['conv_b'], lp['w_down'])
    x = x + rms_norm(f, lp['post_ffn_g'])
    return x, new_wk, new_wv, vn, new_conv


def setup_inputs(seed: int = 0) -> dict:
    key = jax.random.key(seed)
    ks = jax.random.split(key, 32)
    f32 = jnp.float32
    wb = min(WINDOW, PAST_LEN)

    def nrm(k, shape, scale=1.0):
        return jax.random.normal(k, shape, f32) * scale

    def gain(k, shape):
        return 1.0 + 0.05 * jax.random.normal(k, shape, f32)

    return {
        'x_prompt': nrm(ks[0], (BATCH, SEQ, D_MODEL)),
        'x_sample': nrm(ks[1], (DEC_BATCH, DEC_SEQ, D_MODEL)),
        'cache_win_k': nrm(ks[2], (DEPTH, DEC_BATCH, wb, N_KV_HEADS, HEAD_DIM)),
        'cache_win_v': nrm(ks[3], (DEPTH, DEC_BATCH, wb, N_KV_HEADS, HEAD_DIM)),
        'cache_mem_k': nrm(ks[4], (DEPTH, DEC_BATCH, N_MEM, MEM_HEADS, MEM_HEAD_DIM)),
        'cache_mem_v': nrm(ks[5], (DEPTH, DEC_BATCH, N_MEM, MEM_HEADS, MEM_HEAD_DIM)),
        'state_conv': nrm(ks[6], (DEPTH, DEC_BATCH, CONV_WIDTH - 1, 2 * D_FF)),
        'mem_prompt': nrm(ks[7], (BATCH, N_MEM, D_MODEL)),
        'pre_mix_g': gain(ks[8], (DEPTH, D_MODEL)),
        'w_in': nrm(ks[9], (DEPTH, D_MODEL, IN_WIDTH), D_MODEL ** -0.5),
        'attn_sinks': nrm(ks[10], (DEPTH, N_Q_HEADS), 0.5),
        'sg_ln_g': gain(ks[11], (DEPTH, SG_WIDTH)),
        'sg_ln_b': nrm(ks[12], (DEPTH, SG_WIDTH), 0.02),
        'sg_w': nrm(ks[13], (DEPTH, SG_GROUPS, CHUNK, CHUNK), CHUNK ** -0.5),
        'sg_b': 1.0 + nrm(ks[14], (DEPTH, SG_GROUPS, CHUNK), 0.1),
        'mem_norm_g': gain(ks[15], (DEPTH, D_MODEL)),
        'w_mem_kv': nrm(ks[16], (DEPTH, D_MODEL, 2 * MEM_Q), D_MODEL ** -0.5),
        'w_o': nrm(ks[17], (DEPTH, N_BRANCHES, BRANCH_WIDTH, D_MODEL), BRANCH_WIDTH ** -0.5),
        'post_mix_g': gain(ks[18], (DEPTH, D_MODEL)),
        'pre_ffn_g': gain(ks[19], (DEPTH, D_MODEL)),
        'w_up': nrm(ks[20], (DEPTH, D_MODEL, 2 * D_FF), D_MODEL ** -0.5),
        'conv_w': nrm(ks[21], (DEPTH, CONV_WIDTH, 2 * D_FF), CONV_WIDTH ** -0.5),
        'conv_b': nrm(ks[22], (DEPTH, 2 * D_FF), 0.01),
        'w_down': nrm(ks[23], (DEPTH, D_FF, D_MODEL), D_FF ** -0.5),
        'post_ffn_g': gain(ks[24], (DEPTH, D_MODEL)),
    }


def reference(x_prompt, x_sample, cache_win_k, cache_win_v, cache_mem_k, cache_mem_v, state_conv, mem_prompt,
              pre_mix_g, w_in, attn_sinks, sg_ln_g, sg_ln_b, sg_w, sg_b, mem_norm_g, w_mem_kv, w_o,
              post_mix_g, pre_ffn_g, w_up, conv_w, conv_b, w_down, post_ffn_g):
    y_p, y_s = x_prompt, x_sample
    wk_p, wv_p, mk_p, mv_p, cv_p = [], [], [], [], []
    wk_s, wv_s, sgv_s, cv_s = [], [], [], []
    for l in range(DEPTH):
        lp = {
            'pre_mix_g': pre_mix_g[l], 'w_in': w_in[l], 'sinks': attn_sinks[l],
            'sg_ln_g': sg_ln_g[l], 'sg_ln_b': sg_ln_b[l], 'sg_w': sg_w[l], 'sg_b': sg_b[l],
            'w_o': w_o[l], 'post_mix_g': post_mix_g[l], 'pre_ffn_g': pre_ffn_g[l],
            'w_up': w_up[l], 'conv_w': conv_w[l], 'conv_b': conv_b[l], 'w_down': w_down[l],
            'post_ffn_g': post_ffn_g[l],
        }
        mem_k_l, mem_v_l = memory_kv(mem_prompt, mem_norm_g[l], w_mem_kv[l])
        y_p, a_k, a_v, _, a_c = decoder_layer(y_p, 0, None, None, mem_k_l, mem_v_l, None, lp)
        wk_p.append(a_k); wv_p.append(a_v); mk_p.append(mem_k_l); mv_p.append(mem_v_l); cv_p.append(a_c)
        y_s, b_k, b_v, b_sg, b_c = decoder_layer(y_s, PAST_LEN, cache_win_k[l], cache_win_v[l],
                                                 cache_mem_k[l], cache_mem_v[l], state_conv[l], lp)
        wk_s.append(b_k); wv_s.append(b_v); sgv_s.append(b_sg); cv_s.append(b_c)
    return (y_p, y_s,
            jnp.stack(wk_p), jnp.stack(wv_p), jnp.stack(mk_p), jnp.stack(mv_p), jnp.stack(cv_p),
            jnp.stack(wk_s), jnp.stack(wv_s), jnp.stack(sgv_s), jnp.stack(cv_s))
```

```python
import functools
import math

import jax
import jax.numpy as jnp
import numpy as np
from jax import lax
from jax.experimental import pallas as pl
from jax.experimental.pallas import tpu as pltpu

F32 = jnp.float32
BF16 = jnp.bfloat16

D_MODEL = 1024
N_Q_HEADS = 8
N_KV_HEADS = 2
GROUP = N_Q_HEADS // N_KV_HEADS
HEAD_DIM = 64
ATTN_Q = N_Q_HEADS * HEAD_DIM
ATTN_KV = N_KV_HEADS * HEAD_DIM
WINDOW = 128
ROPE_THETA = 10000.0
CHUNK = 128
SG_GROUPS = 4
SG_GROUP_DIM = 128
SG_WIDTH = SG_GROUPS * SG_GROUP_DIM
N_MEM = 256
MEM_HEADS = 4
MEM_HEAD_DIM = 128
MEM_Q = MEM_HEADS * MEM_HEAD_DIM
N_BRANCHES = 3
D_FF = 2816
CONV_WIDTH = 3
EPS = 1e-6
NEG_INF = -1e30
PAST_LEN = 16384

OFF_Q = 0
OFF_K = ATTN_Q
OFF_V = OFF_K + ATTN_KV
OFF_SGU = OFF_V + ATTN_KV
OFF_SGV = OFF_SGU + SG_WIDTH
OFF_MQ = OFF_SGV + SG_WIDTH
OFF_GATE = OFF_MQ + MEM_Q
IN_WIDTH = OFF_GATE + N_BRANCHES * D_MODEL

LANES = 128
SUBLANES = 8
VMEM_LIMIT_BYTES = 56 * 1024 * 1024

PROMPT_ROWS = 256
SAMPLE_SEQS = 8
SAMPLE_FFN_SEQS = 32


def _rms(x, g):
    return x * lax.rsqrt(jnp.mean(x * x, axis=-1, keepdims=True) + EPS) * g


def _dot(a, b):
    return jnp.dot(a, b, preferred_element_type=F32)


def _gelu_erf(x):
    return 0.5 * x * (1.0 + lax.erf(x * np.float32(math.sqrt(0.5))))


def _gelu_tanh(x):
    c = np.float32(math.sqrt(2.0 / math.pi))
    return x * (0.5 * (1.0 + jnp.tanh(c * (x + 0.044715 * (x * x * x)))))


def _lane_iota(shape):
    return lax.broadcasted_iota(jnp.int32, shape, len(shape) - 1)


def _rope(x, cos, sin_signed):
    w = x.shape[1]
    reps = w // LANES
    c = jnp.concatenate([cos] * reps, axis=1) if reps > 1 else cos
    s = jnp.concatenate([sin_signed] * reps, axis=1) if reps > 1 else sin_signed
    first_half = (_lane_iota(x.shape) % HEAD_DIM) < (HEAD_DIM // 2)
    swapped = jnp.where(first_half, pltpu.roll(x, w - HEAD_DIM // 2, 1), pltpu.roll(x, HEAD_DIM // 2, 1))
    return x * c + swapped * s


def _dup_head(kv, g):
    rolled = pltpu.roll(kv, HEAD_DIM, 1)
    low = _lane_iota(kv.shape) < HEAD_DIM
    return jnp.where(low, kv, rolled) if g == 0 else jnp.where(low, rolled, kv)


def _stack_group_queries(q, g):
    low = _lane_iota(q.shape[:2] + (LANES,)) < HEAD_DIM
    parts = []
    for i in range(GROUP):
        h = GROUP * g + i
        pair = q[:, :, (h // 2) * LANES:(h // 2 + 1) * LANES]
        parts.append(jnp.where(low if h % 2 == 0 else ~low, pair, 0.0))
    return jnp.concatenate(parts, axis=1)


def _unstack_group_outputs(o, rows):
    low = _lane_iota((o.shape[0], rows, LANES)) < HEAD_DIM
    pairs = []
    for p in range(GROUP // 2):
        even = o[:, (2 * p) * rows:(2 * p + 1) * rows]
        odd = o[:, (2 * p + 1) * rows:(2 * p + 2) * rows]
        pairs.append(jnp.where(low, even, odd))
    return jnp.concatenate(pairs, axis=2)


def _sink_softmax_pv(s, mask, sink, vals):
    s = jnp.where(mask, s, NEG_INF)
    m = jnp.maximum(jnp.max(s, axis=-1, keepdims=True), sink)
    e = jnp.exp(s - m)
    denom = jnp.sum(e, axis=-1, keepdims=True) + jnp.exp(sink - m)
    p = (e * (1.0 / denom)).astype(BF16)
    return jnp.einsum('bqk,bkd->bqd', p, vals, preferred_element_type=F32)


def _sink_column(sinks_ref, g, rows):
    r = lax.broadcasted_iota(jnp.int32, (1, GROUP * rows, 1), 1)
    col = jnp.full((1, GROUP * rows, 1), sinks_ref[GROUP * g + GROUP - 1], F32)
    for i in range(GROUP - 2, -1, -1):
        col = jnp.where(r < (i + 1) * rows, sinks_ref[GROUP * g + i], col)
    return col


def _mix_kernel(*refs, sample):
    if sample:
        (sinks_ref, x_ref, cos_ref, sin_ref, g_pre_ref, w_in_ref, ln_g_ref, ln_b_ref, wmix_ref, sgb_ref,
         mk_ref, mv_ref, w_o_ref, g_post_ref, ck_ref, cv_ref,
         x1_ref, wk_ref, wv_ref, sgv_ref) = refs
    else:
        (sinks_ref, x_ref, cos_ref, sin_ref, g_pre_ref, w_in_ref, ln_g_ref, ln_b_ref, wmix_ref, sgb_ref,
         mk_ref, mv_ref, w_o_ref, g_post_ref,
         x1_ref, wk_ref, wv_ref, kcar_ref, vcar_ref) = refs
    rows = x_ref.shape[0]
    step = pl.program_id(0)

    x = x_ref[...]
    h = _rms(x, g_pre_ref[...]).astype(BF16)

    zqkv = _dot(h, w_in_ref[:, OFF_Q:OFF_SGU])
    qk = _rope(zqkv[:, :OFF_V], cos_ref[...], sin_ref[...])
    q = qk[:, :ATTN_Q]
    k = qk[:, OFF_K:OFF_V]
    v = zqkv[:, OFF_V:OFF_SGU]

    if sample:
        seqs = rows // SUBLANES
        past = ck_ref.shape[1]
        k3 = k.reshape(seqs, SUBLANES, ATTN_KV)
        v3 = v.reshape(seqs, SUBLANES, ATTN_KV)
        ck = ck_ref[...]
        cv = cv_ref[...]
        wk_ref[:, :past - SUBLANES, :] = ck[:, SUBLANES:, :]
        wk_ref[:, past - SUBLANES:, :] = k3
        wv_ref[:, :past - SUBLANES, :] = cv[:, SUBLANES:, :]
        wv_ref[:, past - SUBLANES:, :] = v3
        n_keys = past + 2 * SUBLANES
        pad = jnp.zeros((seqs, SUBLANES, ATTN_KV), F32)
        kk = jnp.concatenate([ck, k3, pad], axis=1).reshape(seqs * n_keys, ATTN_KV)
        vv = jnp.concatenate([cv, v3, pad], axis=1).reshape(seqs * n_keys, ATTN_KV)
        q3 = q.reshape(seqs, SUBLANES, ATTN_Q)
        qi = lax.broadcasted_iota(jnp.int32, (1, GROUP * SUBLANES, n_keys), 1) % SUBLANES
        kj = lax.broadcasted_iota(jnp.int32, (1, GROUP * SUBLANES, n_keys), 2)
        mask = (kj > qi + (past - WINDOW)) & (kj <= qi + past)
        outs = []
        for g in range(N_KV_HEADS):
            kd = _dup_head(kk, g).astype(BF16).reshape(seqs, n_keys, LANES)
            vd = _dup_head(vv, g).astype(BF16).reshape(seqs, n_keys, LANES)
            qs = _stack_group_queries(q3, g).astype(BF16)
            s = jnp.einsum('bqd,bkd->bqk', qs, kd, preferred_element_type=F32) * (HEAD_DIM ** -0.5)
            o = _sink_softmax_pv(s, mask, _sink_column(sinks_ref, g, SUBLANES), vd)
            outs.append(_unstack_group_outputs(o, SUBLANES))
        attn = jnp.concatenate(outs, axis=2).reshape(rows, ATTN_Q)
    else:
        nb = rows // WINDOW

        @pl.when(step == 0)
        def _():
            kcar_ref[...] = jnp.zeros_like(kcar_ref)
            vcar_ref[...] = jnp.zeros_like(vcar_ref)

        wk_ref[...] = k[rows - WINDOW:]
        wv_ref[...] = v[rows - WINDOW:]
        kk = jnp.concatenate([kcar_ref[...], k], axis=0)
        vv = jnp.concatenate([vcar_ref[...], v], axis=0)
        kcar_ref[...] = k[rows - WINDOW:]
        vcar_ref[...] = v[rows - WINDOW:]
        q3 = q.reshape(nb, WINDOW, ATTN_Q)
        shape = (nb, GROUP * WINDOW, 2 * WINDOW)
        qi = lax.broadcasted_iota(jnp.int32, shape, 1) % WINDOW
        kj = lax.broadcasted_iota(jnp.int32, shape, 2)
        blk = lax.broadcasted_iota(jnp.int32, shape, 0)
        first_key = jnp.where((blk == 0) & (step == 0), WINDOW, 0)
        mask = (kj > qi) & (kj <= qi + WINDOW) & (kj >= first_key)
        outs = []
        for g in range(N_KV_HEADS):
            kd = _dup_head(kk, g).astype(BF16)
            vd = _dup_head(vv, g).astype(BF16)
            kb = jnp.stack([kd[n * WINDOW:(n + 2) * WINDOW] for n in range(nb)], axis=0)
            vb = jnp.stack([vd[n * WINDOW:(n + 2) * WINDOW] for n in range(nb)], axis=0)
            qs = _stack_group_queries(q3, g).astype(BF16)
            s = jnp.einsum('bqd,bkd->bqk', qs, kb, preferred_element_type=F32) * (HEAD_DIM ** -0.5)
            o = _sink_softmax_pv(s, mask, _sink_column(sinks_ref, g, WINDOW), vb)
            outs.append(_unstack_group_outputs(o, WINDOW))
        attn = jnp.concatenate(outs, axis=2).reshape(rows, ATTN_Q)

    zsg = _dot(h, w_in_ref[:, OFF_SGU:OFF_MQ])
    u = _gelu_erf(zsg[:, :SG_WIDTH])
    gv = _gelu_erf(zsg[:, SG_WIDTH:])
    gc = gv - jnp.mean(gv, axis=-1, keepdims=True)
    var = jnp.mean(gc * gc, axis=-1, keepdims=True)
    vn = gc * lax.rsqrt(var + EPS) * ln_g_ref[...] + ln_b_ref[...]
    if sample:
        sgv_ref[...] = vn
    chunk = wmix_ref.shape[1]
    t_idx = lax.broadcasted_iota(jnp.int32, (chunk, chunk), 0)
    s_idx = lax.broadcasted_iota(jnp.int32, (chunk, chunk), 1)
    keep = t_idx >= s_idx
    if sample:
        keep = keep & ((t_idx // SUBLANES) == (s_idx // SUBLANES))
    vn_b = vn.astype(BF16)
    sg_cols = []
    for g in range(SG_GROUPS):
        wm = jnp.where(keep, wmix_ref[g], 0.0).astype(BF16)
        blocks = [_dot(wm, vn_b[c * chunk:(c + 1) * chunk, g * SG_GROUP_DIM:(g + 1) * SG_GROUP_DIM])
                  for c in range(rows // chunk)]
        sg_cols.append(jnp.concatenate(blocks, axis=0) if len(blocks) > 1 else blocks[0])
    mixed_sg = jnp.concatenate(sg_cols, axis=1)
    bias = sgb_ref[...]
    if rows // chunk > 1:
        bias = jnp.concatenate([bias] * (rows // chunk), axis=0)
    sg = u * (mixed_sg + bias)

    mq = _dot(h, w_in_ref[:, OFF_MQ:OFF_GATE])
    mem_scale = MEM_HEAD_DIM ** -0.5
    mem_cols = []
    for hd in range(MEM_HEADS):
        sl = slice(hd * MEM_HEAD_DIM, (hd + 1) * MEM_HEAD_DIM)
        if sample:
            seqs = rows // SUBLANES
            qh = mq[:, sl].reshape(seqs, SUBLANES, MEM_HEAD_DIM).astype(BF16)
            kh = mk_ref[:, :, sl].astype(BF16)
            vh = mv_ref[:, :, sl].astype(BF16)
            s = jnp.einsum('bqd,bkd->bqk', qh, kh, preferred_element_type=F32) * mem_scale
        else:
            qh = mq[:, sl].astype(BF16)
            s = lax.dot_general(qh, mk_ref[:, sl], (((1,), (1,)), ((), ())), preferred_element_type=F32) * mem_scale
        m = jnp.max(s, axis=-1, keepdims=True)
        e = jnp.exp(s - m)
        p = (e * (1.0 / jnp.sum(e, axis=-1, keepdims=True))).astype(BF16)
        if sample:
            o = jnp.einsum('bqk,bkd->bqd', p, vh, preferred_element_type=F32).reshape(rows, MEM_HEAD_DIM)
        else:
            o = _dot(p, mv_ref[:, sl])
        mem_cols.append(o)
    memo = jnp.concatenate(mem_cols, axis=1)

    mixed = None
    for n, br in enumerate((attn, sg, memo)):
        proj = _dot(br.astype(BF16), w_o_ref[n * ATTN_Q:(n + 1) * ATTN_Q, :])
        logits = _dot(h, w_in_ref[:, OFF_GATE + n * D_MODEL:OFF_GATE + (n + 1) * D_MODEL])
        term = (1.0 / (1.0 + jnp.exp(-logits))) * proj
        mixed = term if mixed is None else mixed + term
    x1_ref[...] = x + _rms(mixed, g_post_ref[...])


def _ffn_kernel(*refs, sample):
    if sample:
        (x_ref, g_pre_ref, w_up_ref, cw_ref, cb_ref, w_down_ref, g_post_ref, st_ref,
         y_ref, nc_ref) = refs
    else:
        (x_ref, g_pre_ref, w_up_ref, cw_ref, cb_ref, w_down_ref, g_post_ref,
         y_ref, nc_ref, car_ref) = refs
    rows = x_ref.shape[0]
    step = pl.program_id(0)
    x = x_ref[...]
    h = _rms(x, g_pre_ref[...]).astype(BF16)

    def conv_half(col0):
        cols = slice(col0, col0 + D_FF)
        up = _dot(h, w_up_ref[:, cols])
        w0 = cw_ref[0:1, cols]
        w1 = cw_ref[1:2, cols]
        w2 = cw_ref[2:3, cols]
        if sample:
            seqs = rows // SUBLANES
            up3 = up.reshape(seqs, SUBLANES, D_FF)
            nc_ref[:, :, cols] = up3[:, SUBLANES - (CONV_WIDTH - 1):, :]
            st = st_ref[:, :, cols]
            st0 = jnp.broadcast_to(st[:, 0:1, :], up3.shape)
            st1 = jnp.broadcast_to(st[:, 1:2, :], up3.shape)
            r = lax.broadcasted_iota(jnp.int32, up3.shape, 1)
            m1 = jnp.where(r == 0, st1, pltpu.roll(up3, 1, 1))
            m2 = jnp.where(r == 0, st0, jnp.where(r == 1, st1, pltpu.roll(up3, 2, 1)))
            c = cb_ref[:, cols] + m2 * w0 + m1 * w1 + up3 * w2
            return c.reshape(rows, D_FF)
        nc_ref[:, cols] = up[rows - (CONV_WIDTH - 1):]
        prev = car_ref[:, cols]
        r8 = lax.broadcasted_iota(jnp.int32, (SUBLANES, D_FF), 0)
        r1 = pltpu.roll(up, 1, 0)
        r2 = pltpu.roll(up, 2, 0)
        head1 = jnp.where(r8 < 1, pltpu.roll(prev, 1, 0), r1[:SUBLANES])
        head2 = jnp.where(r8 < 2, pltpu.roll(prev, 2, 0), r2[:SUBLANES])
        m1 = jnp.concatenate([head1, r1[SUBLANES:]], axis=0)
        m2 = jnp.concatenate([head2, r2[SUBLANES:]], axis=0)
        car_ref[:, cols] = up[rows - SUBLANES:]
        return cb_ref[:, cols] + m2 * w0 + m1 * w1 + up * w2

    if not sample:
        @pl.when(step == 0)
        def _():
            car_ref[...] = jnp.zeros_like(car_ref)

    gate = conv_half(0)
    val = conv_half(D_FF)
    act = (_gelu_tanh(gate) * val).astype(BF16)
    f = _dot(act, w_down_ref[...])
    y_ref[...] = x + _rms(f, g_post_ref[...])


def _memkv_kernel(mem_ref, g_ref, w_ref, k_ref, v_ref):
    h = _rms(mem_ref[...], g_ref[...]).astype(BF16)
    kv = _dot(h, w_ref[...])
    k_ref[...] = kv[:, :MEM_Q]
    v_ref[...] = kv[:, MEM_Q:]


def _resident():
    return pl.BlockSpec(memory_space=pltpu.VMEM)


def _row_spec(rows, width):
    return pl.BlockSpec((rows, width), lambda i: (i, 0))


def _const_spec(shape):
    nd = len(shape)
    return pl.BlockSpec(shape, lambda i: (0,) * nd)


def _compiler_params():
    return pltpu.CompilerParams(dimension_semantics=("arbitrary",), vmem_limit_bytes=VMEM_LIMIT_BYTES)


def _rope_tables(pos):
    half = HEAD_DIM // 2
    inv_freq = ROPE_THETA ** (-jnp.arange(half, dtype=F32) / half)
    ang = pos.astype(F32)[:, None] * inv_freq[None, :]
    cos = jnp.cos(ang)
    sin = jnp.sin(ang)
    cos128 = jnp.concatenate([cos] * (LANES // half), axis=1)
    sin128 = jnp.concatenate([-sin, sin] * (LANES // HEAD_DIM), axis=1)
    return cos128, sin128


def _mix_call(x2d, start, seq_len, lp, mem_k, mem_v, caches):
    n_rows = x2d.shape[0]
    sample = caches is not None
    rows = SAMPLE_SEQS * seq_len if sample else PROMPT_ROWS
    steps = n_rows // rows
    if sample:
        pos = start + (jnp.arange(rows, dtype=jnp.int32) % seq_len)
        cos, sin = _rope_tables(pos)
        table_spec = _const_spec((rows, LANES))
        chunk = rows
        wmix = jnp.tile(lp['sg_w'][:, :seq_len, :seq_len], (1, rows // seq_len, rows // seq_len))
        sgb = jnp.repeat(jnp.tile(lp['sg_b'][:, :seq_len], (1, rows // seq_len)).T, SG_GROUP_DIM, axis=1)
    else:
        pos = start + jnp.arange(n_rows, dtype=jnp.int32)
        cos, sin = _rope_tables(pos)
        table_spec = _row_spec(rows, LANES)
        chunk = CHUNK
        wmix = lp['sg_w'][:, :chunk, :chunk]
        sgb = jnp.repeat(lp['sg_b'][:, :chunk].T, SG_GROUP_DIM, axis=1)

    in_specs = [
        pl.BlockSpec(memory_space=pltpu.SMEM),
        _row_spec(rows, D_MODEL),
        table_spec, table_spec,
        _const_spec((1, D_MODEL)),
        _resident(),
        _const_spec((1, SG_WIDTH)), _const_spec((1, SG_WIDTH)),
        _const_spec((SG_GROUPS, chunk, chunk)),
        _const_spec((chunk, SG_WIDTH)),
    ]
    args = [lp['sinks'], x2d, cos, sin, lp['pre_mix_g'].reshape(1, D_MODEL), lp['w_in'],
            lp['sg_ln_g'].reshape(1, SG_WIDTH), lp['sg_ln_b'].reshape(1, SG_WIDTH), wmix, sgb]
    if sample:
        seqs = SAMPLE_SEQS
        in_specs += [pl.BlockSpec((seqs, N_MEM, MEM_Q), lambda i: (i, 0, 0))] * 2
    else:
        in_specs += [_const_spec((N_MEM, MEM_Q))] * 2
    args += [mem_k, mem_v]
    in_specs += [_resident(), _const_spec((1, D_MODEL))]
    args += [lp['w_o'], lp['post_mix_g'].reshape(1, D_MODEL)]

    out_shape = [jax.ShapeDtypeStruct((n_rows, D_MODEL), F32)]
    out_specs = [_row_spec(rows, D_MODEL)]
    scratch = []
    if sample:
        win_k, win_v = caches
        n_seq, past = win_k.shape[0], win_k.shape[1]
        cache_spec = pl.BlockSpec((SAMPLE_SEQS, past, ATTN_KV), lambda i: (i, 0, 0))
        in_specs += [cache_spec, cache_spec]
        args += [win_k, win_v]
        out_shape += [jax.ShapeDtypeStruct((n_seq, past, ATTN_KV), F32)] * 2
        out_specs += [cache_spec, cache_spec]
        out_shape += [jax.ShapeDtypeStruct((n_rows, SG_WIDTH), F32)]
        out_specs += [_row_spec(rows, SG_WIDTH)]
    else:
        out_shape += [jax.ShapeDtypeStruct((WINDOW, ATTN_KV), F32)] * 2
        out_specs += [_const_spec((WINDOW, ATTN_KV))] * 2
        scratch = [pltpu.VMEM((WINDOW, ATTN_KV), F32), pltpu.VMEM((WINDOW, ATTN_KV), F32)]

    return pl.pallas_call(
        functools.partial(_mix_kernel, sample=sample),
        grid=(steps,),
        in_specs=in_specs,
        out_specs=out_specs,
        out_shape=out_shape,
        scratch_shapes=scratch,
        compiler_params=_compiler_params(),
        name="mix_sample" if sample else "mix_prompt",
    )(*args)


def _ffn_call(x2d, seq_len, lp, conv_state):
    n_rows = x2d.shape[0]
    sample = conv_state is not None
    rows = SAMPLE_FFN_SEQS * seq_len if sample else PROMPT_ROWS
    steps = n_rows // rows
    in_specs = [
        _row_spec(rows, D_MODEL),
        _const_spec((1, D_MODEL)),
        _resident(),
        _const_spec((CONV_WIDTH, 2 * D_FF)),
        _const_spec((1, 2 * D_FF)),
        _resident(),
        _const_spec((1, D_MODEL)),
    ]
    args = [x2d, lp['pre_ffn_g'].reshape(1, D_MODEL), lp['w_up'], lp['conv_w'],
            lp['conv_b'].reshape(1, 2 * D_FF), lp['w_down'], lp['post_ffn_g'].reshape(1, D_MODEL)]
    out_shape = [jax.ShapeDtypeStruct((n_rows, D_MODEL), F32)]
    out_specs = [_row_spec(rows, D_MODEL)]
    scratch = []
    if sample:
        n_seq = conv_state.shape[0]
        st_spec = pl.BlockSpec((SAMPLE_FFN_SEQS, CONV_WIDTH - 1, 2 * D_FF), lambda i: (i, 0, 0))
        in_specs += [st_spec]
        args += [conv_state]
        out_shape += [jax.ShapeDtypeStruct((n_seq, CONV_WIDTH - 1, 2 * D_FF), F32)]
        out_specs += [st_spec]
    else:
        out_shape += [jax.ShapeDtypeStruct((CONV_WIDTH - 1, 2 * D_FF), F32)]
        out_specs += [_const_spec((CONV_WIDTH - 1, 2 * D_FF))]
        scratch = [pltpu.VMEM((SUBLANES, 2 * D_FF), F32)]
    return pl.pallas_call(
        functools.partial(_ffn_kernel, sample=sample),
        grid=(steps,),
        in_specs=in_specs,
        out_specs=out_specs,
        out_shape=out_shape,
        scratch_shapes=scratch,
        compiler_params=_compiler_params(),
        name="ffn_sample" if sample else "ffn_prompt",
    )(*args)


def _memkv_call(mem2d, g, w_bf16):
    return pl.pallas_call(
        _memkv_kernel,
        out_shape=[jax.ShapeDtypeStruct((mem2d.shape[0], MEM_Q), F32)] * 2,
        name="mem_kv",
    )(mem2d, g.reshape(1, D_MODEL), w_bf16)


def kernel(x_prompt, x_sample, cache_win_k, cache_win_v, cache_mem_k, cache_mem_v, state_conv, mem_prompt,
           pre_mix_g, w_in, attn_sinks, sg_ln_g, sg_ln_b, sg_w, sg_b, mem_norm_g, w_mem_kv, w_o,
           post_mix_g, pre_ffn_g, w_up, conv_w, conv_b, w_down, post_ffn_g):
    depth = w_in.shape[0]
    batch, seq, _ = x_prompt.shape
    dec_batch, dec_seq, _ = x_sample.shape
    past_len = PAST_LEN
    assert batch == 1 and depth == 1

    yp = x_prompt.reshape(batch * seq, D_MODEL)
    ys = x_sample.reshape(dec_batch * dec_seq, D_MODEL)
    outs = {name: [] for name in ('wk_p', 'wv_p', 'mk_p', 'mv_p', 'cv_p', 'wk_s', 'wv_s', 'sgv_s', 'cv_s')}
    for l in range(depth):
        lp = {
            'sinks': attn_sinks[l], 'pre_mix_g': pre_mix_g[l], 'w_in': w_in[l].astype(BF16),
            'sg_ln_g': sg_ln_g[l], 'sg_ln_b': sg_ln_b[l], 'sg_w': sg_w[l], 'sg_b': sg_b[l],
            'w_o': w_o[l].reshape(N_BRANCHES * ATTN_Q, D_MODEL).astype(BF16), 'post_mix_g': post_mix_g[l],
            'pre_ffn_g': pre_ffn_g[l], 'w_up': w_up[l].astype(BF16), 'conv_w': conv_w[l], 'conv_b': conv_b[l],
            'w_down': w_down[l].astype(BF16), 'post_ffn_g': post_ffn_g[l],
        }
        mem_k, mem_v = _memkv_call(mem_prompt.reshape(batch * N_MEM, D_MODEL), mem_norm_g[l],
                                   w_mem_kv[l].astype(BF16))
        outs['mk_p'].append(mem_k.reshape(batch, N_MEM, MEM_HEADS, MEM_HEAD_DIM))
        outs['mv_p'].append(mem_v.reshape(batch, N_MEM, MEM_HEADS, MEM_HEAD_DIM))

        x1, wk, wv = _mix_call(yp, 0, seq, lp, mem_k.astype(BF16), mem_v.astype(BF16), None)
        yp, nc = _ffn_call(x1, seq, lp, None)
        outs['wk_p'].append(wk.reshape(batch, WINDOW, N_KV_HEADS, HEAD_DIM))
        outs['wv_p'].append(wv.reshape(batch, WINDOW, N_KV_HEADS, HEAD_DIM))
        outs['cv_p'].append(nc.reshape(batch, CONV_WIDTH - 1, 2 * D_FF))

        past = cache_win_k.shape[2]
        x1s, wks, wvs, sgv = _mix_call(
            ys, past_len, dec_seq, lp,
            cache_mem_k[l].reshape(dec_batch, N_MEM, MEM_Q), cache_mem_v[l].reshape(dec_batch, N_MEM, MEM_Q),
            (cache_win_k[l].reshape(dec_batch, past, ATTN_KV), cache_win_v[l].reshape(dec_batch, past, ATTN_KV)))
        ys, ncs = _ffn_call(x1s, dec_seq, lp, state_conv[l])
        outs['wk_s'].append(wks.reshape(dec_batch, past, N_KV_HEADS, HEAD_DIM))
        outs['wv_s'].append(wvs.reshape(dec_batch, past, N_KV_HEADS, HEAD_DIM))
        outs['sgv_s'].append(sgv.reshape(dec_batch, dec_seq, SG_WIDTH))
        outs['cv_s'].append(ncs)

    return (yp.reshape(batch, seq, D_MODEL), ys.reshape(dec_batch, dec_seq, D_MODEL),
            jnp.stack(outs['wk_p']), jnp.stack(outs['wv_p']), jnp.stack(outs['mk_p']), jnp.stack(outs['mv_p']),
            jnp.stack(outs['cv_p']),
            jnp.stack(outs['wk_s']), jnp.stack(outs['wv_s']), jnp.stack(outs['sgv_s']), jnp.stack(outs['cv_s']))
```

```python
import functools
import math

import jax
import jax.numpy as jnp
import numpy as np
from jax import lax
from jax.experimental import pallas as pl
from jax.experimental.pallas import tpu as pltpu

F32 = jnp.float32
BF16 = jnp.bfloat16

D_MODEL = 1024
N_Q_HEADS = 8
N_KV_HEADS = 2
GROUP = N_Q_HEADS // N_KV_HEADS
HEAD_DIM = 64
ATTN_Q = N_Q_HEADS * HEAD_DIM
ATTN_KV = N_KV_HEADS * HEAD_DIM
WINDOW = 128
ROPE_THETA = 10000.0
CHUNK = 128
SG_GROUPS = 4
SG_GROUP_DIM = 128
SG_WIDTH = SG_GROUPS * SG_GROUP_DIM
N_MEM = 256
MEM_HEADS = 4
MEM_HEAD_DIM = 128
MEM_Q = MEM_HEADS * MEM_HEAD_DIM
N_BRANCHES = 3
D_FF = 2816
CONV_WIDTH = 3
EPS = 1e-6
NEG_INF = -1e30
PAST_LEN = 16384

OFF_Q = 0
OFF_K = ATTN_Q
OFF_V = OFF_K + ATTN_KV
OFF_SGU = OFF_V + ATTN_KV
OFF_SGV = OFF_SGU + SG_WIDTH
OFF_MQ = OFF_SGV + SG_WIDTH
OFF_GATE = OFF_MQ + MEM_Q
IN_WIDTH = OFF_GATE + N_BRANCHES * D_MODEL

LANES = 128
SUBLANES = 8
VMEM_LIMIT_BYTES = 56 * 1024 * 1024

PROMPT_ROWS = 256
SAMPLE_SEQS = 8
SAMPLE_FFN_SEQS = 32


def _rms(x, g):
    return x * lax.rsqrt(jnp.mean(x * x, axis=-1, keepdims=True) + EPS) * g


def _dot(a, b):
    return jnp.dot(a, b, preferred_element_type=F32)


def _gelu_erf(x):
    return 0.5 * x * (1.0 + lax.erf(x * np.float32(math.sqrt(0.5))))


def _gelu_tanh(x):
    c = np.float32(math.sqrt(2.0 / math.pi))
    return x * (0.5 * (1.0 + jnp.tanh(c * (x + 0.044715 * (x * x * x)))))


def _lane_iota(shape):
    return lax.broadcasted_iota(jnp.int32, shape, len(shape) - 1)


def _rope(x, cos, sin_signed):
    w = x.shape[1]
    reps = w // LANES
    c = jnp.concatenate([cos] * reps, axis=1) if reps > 1 else cos
    s = jnp.concatenate([sin_signed] * reps, axis=1) if reps > 1 else sin_signed
    first_half = (_lane_iota(x.shape) % HEAD_DIM) < (HEAD_DIM // 2)
    swapped = jnp.where(first_half, pltpu.roll(x, w - HEAD_DIM // 2, 1), pltpu.roll(x, HEAD_DIM // 2, 1))
    return x * c + swapped * s


def _dup_head(kv, g):
    rolled = pltpu.roll(kv, HEAD_DIM, 1)
    low = _lane_iota(kv.shape) < HEAD_DIM
    return jnp.where(low, kv, rolled) if g == 0 else jnp.where(low, rolled, kv)


def _stack_group_queries(q, g):
    low = _lane_iota(q.shape[:2] + (LANES,)) < HEAD_DIM
    parts = []
    for i in range(GROUP):
        h = GROUP * g + i
        pair = q[:, :, (h // 2) * LANES:(h // 2 + 1) * LANES]
        parts.append(jnp.where(low if h % 2 == 0 else ~low, pair, 0.0))
    return jnp.concatenate(parts, axis=1)


def _unstack_group_outputs(o, rows):
    low = _lane_iota((o.shape[0], rows, LANES)) < HEAD_DIM
    pairs = []
    for p in range(GROUP // 2):
        even = o[:, (2 * p) * rows:(2 * p + 1) * rows]
        odd = o[:, (2 * p + 1) * rows:(2 * p + 2) * rows]
        pairs.append(jnp.where(low, even, odd))
    return jnp.concatenate(pairs, axis=2)


def _sink_softmax_pv(s, mask, sink, vals):
    s = jnp.where(mask, s, NEG_INF)
    m = jnp.maximum(jnp.max(s, axis=-1, keepdims=True), sink)
    e = jnp.exp(s - m)
    denom = jnp.sum(e, axis=-1, keepdims=True) + jnp.exp(sink - m)
    p = (e * (1.0 / denom)).astype(BF16)
    return jnp.einsum('bqk,bkd->bqd', p, vals, preferred_element_type=F32)


def _sink_column(sinks_ref, g, rows):
    r = lax.broadcasted_iota(jnp.int32, (1, GROUP * rows, 1), 1)
    col = jnp.full((1, GROUP * rows, 1), sinks_ref[GROUP * g + GROUP - 1], F32)
    for i in range(GROUP - 2, -1, -1):
        col = jnp.where(r < (i + 1) * rows, sinks_ref[GROUP * g + i], col)
    return col


def _mix_kernel(*refs, sample):
    if sample:
        (sinks_ref, x_ref, cos_ref, sin_ref, g_pre_ref, w_in_ref, ln_g_ref, ln_b_ref, wmix_ref, sgb_ref,
         mk_ref, mv_ref, w_o_ref, g_post_ref, ck_ref, cv_ref,
         x1_ref, wk_ref, wv_ref, sgv_ref) = refs
    else:
        (sinks_ref, x_ref, cos_ref, sin_ref, cos_s_ref, sin_s_ref, cos_b_ref, sin_b_ref,
         g_pre_ref, w_in_ref, ln_g_ref, ln_b_ref, wmix_ref, sgb_ref,
         mk_ref, mv_ref, w_o_ref, g_post_ref,
         x1_ref, wk_ref, wv_ref, kcar_ref, vcar_ref) = refs
    rows = x_ref.shape[0]
    step = pl.program_id(0)

    x = x_ref[...]
    h = _rms(x, g_pre_ref[...]).astype(BF16)

    zqkv = _dot(h, w_in_ref[:, OFF_Q:OFF_SGU])
    if sample:
        cos, sin_signed = cos_ref[...], sin_ref[...]
    else:
        cb = cos_b_ref[pl.ds(step, 1), :]
        sb = sin_b_ref[pl.ds(step, 1), :]
        cos = cb * cos_ref[...] - sb * sin_ref[...]
        sin_signed = sb * cos_s_ref[...] + cb * sin_s_ref[...]
    qk = _rope(zqkv[:, :OFF_V], cos, sin_signed)
    q = qk[:, :ATTN_Q]
    k = qk[:, OFF_K:OFF_V]
    v = zqkv[:, OFF_V:OFF_SGU]

    if sample:
        seqs = rows // SUBLANES
        past = ck_ref.shape[1]
        k3 = k.reshape(seqs, SUBLANES, ATTN_KV)
        v3 = v.reshape(seqs, SUBLANES, ATTN_KV)
        ck = ck_ref[...]
        cv = cv_ref[...]
        wk_ref[:, :past - SUBLANES, :] = ck[:, SUBLANES:, :]
        wk_ref[:, past - SUBLANES:, :] = k3
        wv_ref[:, :past - SUBLANES, :] = cv[:, SUBLANES:, :]
        wv_ref[:, past - SUBLANES:, :] = v3
        n_keys = past + 2 * SUBLANES
        pad = jnp.zeros((seqs, SUBLANES, ATTN_KV), F32)
        kk = jnp.concatenate([ck, k3, pad], axis=1).reshape(seqs * n_keys, ATTN_KV)
        vv = jnp.concatenate([cv, v3, pad], axis=1).reshape(seqs * n_keys, ATTN_KV)
        q3 = q.reshape(seqs, SUBLANES, ATTN_Q)
        qi = lax.broadcasted_iota(jnp.int32, (1, GROUP * SUBLANES, n_keys), 1) % SUBLANES
        kj = lax.broadcasted_iota(jnp.int32, (1, GROUP * SUBLANES, n_keys), 2)
        mask = (kj > qi + (past - WINDOW)) & (kj <= qi + past)
        outs = []
        for g in range(N_KV_HEADS):
            kd = _dup_head(kk, g).astype(BF16).reshape(seqs, n_keys, LANES)
            vd = _dup_head(vv, g).astype(BF16).reshape(seqs, n_keys, LANES)
            qs = _stack_group_queries(q3, g).astype(BF16)
            s = jnp.einsum('bqd,bkd->bqk', qs, kd, preferred_element_type=F32) * (HEAD_DIM ** -0.5)
            o = _sink_softmax_pv(s, mask, _sink_column(sinks_ref, g, SUBLANES), vd)
            outs.append(_unstack_group_outputs(o, SUBLANES))
        attn = jnp.concatenate(outs, axis=2).reshape(rows, ATTN_Q)
    else:
        nb = rows // WINDOW

        @pl.when(step == 0)
        def _():
            kcar_ref[...] = jnp.zeros_like(kcar_ref)
            vcar_ref[...] = jnp.zeros_like(vcar_ref)

        wk_ref[...] = k[rows - WINDOW:]
        wv_ref[...] = v[rows - WINDOW:]
        kk = jnp.concatenate([kcar_ref[...], k], axis=0)
        vv = jnp.concatenate([vcar_ref[...], v], axis=0)
        kcar_ref[...] = k[rows - WINDOW:]
        vcar_ref[...] = v[rows - WINDOW:]
        q3 = q.reshape(nb, WINDOW, ATTN_Q)
        shape = (nb, GROUP * WINDOW, 2 * WINDOW)
        qi = lax.broadcasted_iota(jnp.int32, shape, 1) % WINDOW
        kj = lax.broadcasted_iota(jnp.int32, shape, 2)
        blk = lax.broadcasted_iota(jnp.int32, shape, 0)
        first_key = jnp.where((blk == 0) & (step == 0), WINDOW, 0)
        mask = (kj > qi) & (kj <= qi + WINDOW) & (kj >= first_key)
        outs = []
        for g in range(N_KV_HEADS):
            kd = _dup_head(kk, g).astype(BF16)
            vd = _dup_head(vv, g).astype(BF16)
            kb = jnp.stack([kd[n * WINDOW:(n + 2) * WINDOW] for n in range(nb)], axis=0)
            vb = jnp.stack([vd[n * WINDOW:(n + 2) * WINDOW] for n in range(nb)], axis=0)
            qs = _stack_group_queries(q3, g).astype(BF16)
            s = jnp.einsum('bqd,bkd->bqk', qs, kb, preferred_element_type=F32) * (HEAD_DIM ** -0.5)
            o = _sink_softmax_pv(s, mask, _sink_column(sinks_ref, g, WINDOW), vb)
            outs.append(_unstack_group_outputs(o, WINDOW))
        attn = jnp.concatenate(outs, axis=2).reshape(rows, ATTN_Q)

    zsg = _dot(h, w_in_ref[:, OFF_SGU:OFF_MQ])
    u = _gelu_erf(zsg[:, :SG_WIDTH])
    gv = _gelu_erf(zsg[:, SG_WIDTH:])
    gc = gv - jnp.mean(gv, axis=-1, keepdims=True)
    var = jnp.mean(gc * gc, axis=-1, keepdims=True)
    vn = gc * lax.rsqrt(var + EPS) * ln_g_ref[...] + ln_b_ref[...]
    if sample:
        sgv_ref[...] = vn
    chunk = wmix_ref.shape[1]
    t_idx = lax.broadcasted_iota(jnp.int32, (chunk, chunk), 0)
    s_idx = lax.broadcasted_iota(jnp.int32, (chunk, chunk), 1)
    keep = t_idx >= s_idx
    if sample:
        keep = keep & ((t_idx // SUBLANES) == (s_idx // SUBLANES))
    vn_b = vn.astype(BF16)
    sg_cols = []
    for g in range(SG_GROUPS):
        wm = jnp.where(keep, wmix_ref[g], 0.0).astype(BF16)
        blocks = [_dot(wm, vn_b[c * chunk:(c + 1) * chunk, g * SG_GROUP_DIM:(g + 1) * SG_GROUP_DIM])
                  for c in range(rows // chunk)]
        sg_cols.append(jnp.concatenate(blocks, axis=0) if len(blocks) > 1 else blocks[0])
    mixed_sg = jnp.concatenate(sg_cols, axis=1)
    bias = sgb_ref[...]
    if rows // chunk > 1:
        bias = jnp.concatenate([bias] * (rows // chunk), axis=0)
    sg = u * (mixed_sg + bias)

    mq = _dot(h, w_in_ref[:, OFF_MQ:OFF_GATE])
    mem_scale = MEM_HEAD_DIM ** -0.5
    mem_cols = []
    for hd in range(MEM_HEADS):
        sl = slice(hd * MEM_HEAD_DIM, (hd + 1) * MEM_HEAD_DIM)
        if sample:
            seqs = rows // SUBLANES
            qh = mq[:, sl].reshape(seqs, SUBLANES, MEM_HEAD_DIM).astype(BF16)
            kh = mk_ref[:, pl.ds(hd, N_MEM, stride=MEM_HEADS), :].astype(BF16)
            vh = mv_ref[:, pl.ds(hd, N_MEM, stride=MEM_HEADS), :].astype(BF16)
            s = jnp.einsum('bqd,bkd->bqk', qh, kh, preferred_element_type=F32) * mem_scale
        else:
            qh = mq[:, sl].astype(BF16)
            s = lax.dot_general(qh, mk_ref[:, sl], (((1,), (1,)), ((), ())), preferred_element_type=F32) * mem_scale
        m = jnp.max(s, axis=-1, keepdims=True)
        e = jnp.exp(s - m)
        p = (e * (1.0 / jnp.sum(e, axis=-1, keepdims=True))).astype(BF16)
        if sample:
            o = jnp.einsum('bqk,bkd->bqd', p, vh, preferred_element_type=F32).reshape(rows, MEM_HEAD_DIM)
        else:
            o = _dot(p, mv_ref[:, sl])
        mem_cols.append(o)
    memo = jnp.concatenate(mem_cols, axis=1)

    mixed = None
    for n, br in enumerate((attn, sg, memo)):
        proj = _dot(br.astype(BF16), w_o_ref[n * ATTN_Q:(n + 1) * ATTN_Q, :])
        logits = _dot(h, w_in_ref[:, OFF_GATE + n * D_MODEL:OFF_GATE + (n + 1) * D_MODEL])
        term = (1.0 / (1.0 + jnp.exp(-logits))) * proj
        mixed = term if mixed is None else mixed + term
    x1_ref[...] = x + _rms(mixed, g_post_ref[...])


def _ffn_kernel(*refs, sample):
    if sample:
        (x_ref, g_pre_ref, w_up_ref, cw_ref, cb_ref, w_down_ref, g_post_ref, st_ref,
         y_ref, nc_ref) = refs
    else:
        (x_ref, g_pre_ref, w_up_ref, cw_ref, cb_ref, w_down_ref, g_post_ref,
         y_ref, nc_ref, car_ref) = refs
    rows = x_ref.shape[0]
    step = pl.program_id(0)
    x = x_ref[...]
    h = _rms(x, g_pre_ref[...]).astype(BF16)

    def conv_half(col0):
        cols = slice(col0, col0 + D_FF)
        up = _dot(h, w_up_ref[:, cols])
        w0 = cw_ref[0:1, cols]
        w1 = cw_ref[1:2, cols]
        w2 = cw_ref[2:3, cols]
        if sample:
            seqs = rows // SUBLANES
            up3 = up.reshape(seqs, SUBLANES, D_FF)
            nc_ref[:, :, cols] = up3[:, SUBLANES - (CONV_WIDTH - 1):, :]
            st = st_ref[:, :, cols]
            st0 = jnp.broadcast_to(st[:, 0:1, :], up3.shape)
            st1 = jnp.broadcast_to(st[:, 1:2, :], up3.shape)
            r = lax.broadcasted_iota(jnp.int32, up3.shape, 1)
            m1 = jnp.where(r == 0, st1, pltpu.roll(up3, 1, 1))
            m2 = jnp.where(r == 0, st0, jnp.where(r == 1, st1, pltpu.roll(up3, 2, 1)))
            c = cb_ref[:, cols] + m2 * w0 + m1 * w1 + up3 * w2
            return c.reshape(rows, D_FF)
        nc_ref[:, cols] = up[rows - (CONV_WIDTH - 1):]
        prev = car_ref[:, cols]
        r8 = lax.broadcasted_iota(jnp.int32, (SUBLANES, D_FF), 0)
        r1 = pltpu.roll(up, 1, 0)
        r2 = pltpu.roll(up, 2, 0)
        head1 = jnp.where(r8 < 1, pltpu.roll(prev, 1, 0), r1[:SUBLANES])
        head2 = jnp.where(r8 < 2, pltpu.roll(prev, 2, 0), r2[:SUBLANES])
        m1 = jnp.concatenate([head1, r1[SUBLANES:]], axis=0)
        m2 = jnp.concatenate([head2, r2[SUBLANES:]], axis=0)
        car_ref[:, cols] = up[rows - SUBLANES:]
        return cb_ref[:, cols] + m2 * w0 + m1 * w1 + up * w2

    if not sample:
        @pl.when(step == 0)
        def _():
            car_ref[...] = jnp.zeros_like(car_ref)

    gate = conv_half(0)
    val = conv_half(D_FF)
    act = (_gelu_tanh(gate) * val).astype(BF16)
    f = _dot(act, w_down_ref[...])
    y_ref[...] = x + _rms(f, g_post_ref[...])


def _memkv_kernel(mem_ref, g_ref, w_ref, k_ref, v_ref):
    h = _rms(mem_ref[...], g_ref[...]).astype(BF16)
    kv = _dot(h, w_ref[...])
    k_ref[...] = kv[:, :MEM_Q]
    v_ref[...] = kv[:, MEM_Q:]


def _resident():
    return pl.BlockSpec(memory_space=pltpu.VMEM)


def _row_spec(rows, width):
    return pl.BlockSpec((rows, width), lambda i: (i, 0))


def _const_spec(shape):
    nd = len(shape)
    return pl.BlockSpec(shape, lambda i: (0,) * nd)


def _compiler_params():
    return pltpu.CompilerParams(dimension_semantics=("arbitrary",), vmem_limit_bytes=VMEM_LIMIT_BYTES)


def _rope_tables(pos):
    half = HEAD_DIM // 2
    inv_freq = ROPE_THETA ** (-jnp.arange(half, dtype=F32) / half)
    ang = pos.astype(F32)[:, None] * inv_freq[None, :]
    cos = jnp.concatenate([jnp.cos(ang)] * (LANES // half), axis=1)
    sin = jnp.concatenate([jnp.sin(ang)] * (LANES // half), axis=1)
    sign = jnp.concatenate([-jnp.ones((1, half), F32), jnp.ones((1, half), F32)] * (LANES // HEAD_DIM), axis=1)
    return cos, sin, cos * sign, sin * sign


def _mix_call(x2d, start, seq_len, lp, mem_k, mem_v, caches):
    n_rows = x2d.shape[0]
    sample = caches is not None
    rows = SAMPLE_SEQS * seq_len if sample else PROMPT_ROWS
    steps = n_rows // rows
    if sample:
        pos = start + (jnp.arange(rows, dtype=jnp.int32) % seq_len)
        cos, _, _, sin_signed = _rope_tables(pos)
        rope_args = [cos, sin_signed]
        rope_specs = [_const_spec((rows, LANES))] * 2
        chunk = rows
        wmix = jnp.tile(lp['sg_w'][:, :seq_len, :seq_len], (1, rows // seq_len, rows // seq_len))
        sgb = jnp.repeat(jnp.tile(lp['sg_b'][:, :seq_len], (1, rows // seq_len)).T, SG_GROUP_DIM, axis=1)
    else:
        in_step = _rope_tables(jnp.arange(rows, dtype=jnp.int32))
        cos_b, sin_b, _, _ = _rope_tables(start + rows * jnp.arange(steps, dtype=jnp.int32))
        rope_args = list(in_step) + [cos_b, sin_b]
        rope_specs = [_const_spec((rows, LANES))] * 4 + [_const_spec((steps, LANES))] * 2
        chunk = CHUNK
        wmix = lp['sg_w'][:, :chunk, :chunk]
        sgb = jnp.repeat(lp['sg_b'][:, :chunk].T, SG_GROUP_DIM, axis=1)

    in_specs = [
        pl.BlockSpec(memory_space=pltpu.SMEM),
        _row_spec(rows, D_MODEL),
        *rope_specs,
        _const_spec((1, D_MODEL)),
        _resident(),
        _const_spec((1, SG_WIDTH)), _const_spec((1, SG_WIDTH)),
        _const_spec((SG_GROUPS, chunk, chunk)),
        _const_spec((chunk, SG_WIDTH)),
    ]
    args = [lp['sinks'], x2d, *rope_args, lp['pre_mix_g'].reshape(1, D_MODEL), lp['w_in'],
            lp['sg_ln_g'].reshape(1, SG_WIDTH), lp['sg_ln_b'].reshape(1, SG_WIDTH), wmix, sgb]
    if sample:
        seqs = SAMPLE_SEQS
        in_specs += [pl.BlockSpec((seqs, N_MEM * MEM_HEADS, MEM_HEAD_DIM), lambda i: (i, 0, 0))] * 2
    else:
        in_specs += [_const_spec((N_MEM, MEM_Q))] * 2
    args += [mem_k, mem_v]
    in_specs += [_resident(), _const_spec((1, D_MODEL))]
    args += [lp['w_o'], lp['post_mix_g'].reshape(1, D_MODEL)]

    out_shape = [jax.ShapeDtypeStruct((n_rows, D_MODEL), F32)]
    out_specs = [_row_spec(rows, D_MODEL)]
    scratch = []
    if sample:
        win_k, win_v = caches
        n_seq, past = win_k.shape[0], win_k.shape[1]
        cache_spec = pl.BlockSpec((SAMPLE_SEQS, past, ATTN_KV), lambda i: (i, 0, 0))
        in_specs += [cache_spec, cache_spec]
        args += [win_k, win_v]
        out_shape += [jax.ShapeDtypeStruct((n_seq, past, ATTN_KV), F32)] * 2
        out_specs += [cache_spec, cache_spec]
        out_shape += [jax.ShapeDtypeStruct((n_rows, SG_WIDTH), F32)]
        out_specs += [_row_spec(rows, SG_WIDTH)]
    else:
        out_shape += [jax.ShapeDtypeStruct((WINDOW, ATTN_KV), F32)] * 2
        out_specs += [_const_spec((WINDOW, ATTN_KV))] * 2
        scratch = [pltpu.VMEM((WINDOW, ATTN_KV), F32), pltpu.VMEM((WINDOW, ATTN_KV), F32)]

    return pl.pallas_call(
        functools.partial(_mix_kernel, sample=sample),
        grid=(steps,),
        in_specs=in_specs,
        out_specs=out_specs,
        out_shape=out_shape,
        scratch_shapes=scratch,
        compiler_params=_compiler_params(),
        name="mix_sample" if sample else "mix_prompt",
    )(*args)


def _ffn_call(x2d, seq_len, lp, conv_state):
    n_rows = x2d.shape[0]
    sample = conv_state is not None
    rows = SAMPLE_FFN_SEQS * seq_len if sample else PROMPT_ROWS
    steps = n_rows // rows
    in_specs = [
        _row_spec(rows, D_MODEL),
        _const_spec((1, D_MODEL)),
        _resident(),
        _const_spec((CONV_WIDTH, 2 * D_FF)),
        _const_spec((1, 2 * D_FF)),
        _resident(),
        _const_spec((1, D_MODEL)),
    ]
    args = [x2d, lp['pre_ffn_g'].reshape(1, D_MODEL), lp['w_up'], lp['conv_w'],
            lp['conv_b'].reshape(1, 2 * D_FF), lp['w_down'], lp['post_ffn_g'].reshape(1, D_MODEL)]
    out_shape = [jax.ShapeDtypeStruct((n_rows, D_MODEL), F32)]
    out_specs = [_row_spec(rows, D_MODEL)]
    scratch = []
    if sample:
        n_seq = conv_state.shape[0]
        st_spec = pl.BlockSpec((SAMPLE_FFN_SEQS, CONV_WIDTH - 1, 2 * D_FF), lambda i: (i, 0, 0))
        in_specs += [st_spec]
        args += [conv_state]
        out_shape += [jax.ShapeDtypeStruct((n_seq, CONV_WIDTH - 1, 2 * D_FF), F32)]
        out_specs += [st_spec]
    else:
        out_shape += [jax.ShapeDtypeStruct((CONV_WIDTH - 1, 2 * D_FF), F32)]
        out_specs += [_const_spec((CONV_WIDTH - 1, 2 * D_FF))]
        scratch = [pltpu.VMEM((SUBLANES, 2 * D_FF), F32)]
    return pl.pallas_call(
        functools.partial(_ffn_kernel, sample=sample),
        grid=(steps,),
        in_specs=in_specs,
        out_specs=out_specs,
        out_shape=out_shape,
        scratch_shapes=scratch,
        compiler_params=_compiler_params(),
        name="ffn_sample" if sample else "ffn_prompt",
    )(*args)


def _memkv_call(mem2d, g, w_bf16):
    return pl.pallas_call(
        _memkv_kernel,
        out_shape=[jax.ShapeDtypeStruct((mem2d.shape[0], MEM_Q), F32)] * 2,
        name="mem_kv",
    )(mem2d, g.reshape(1, D_MODEL), w_bf16)


def kernel(x_prompt, x_sample, cache_win_k, cache_win_v, cache_mem_k, cache_mem_v, state_conv, mem_prompt,
           pre_mix_g, w_in, attn_sinks, sg_ln_g, sg_ln_b, sg_w, sg_b, mem_norm_g, w_mem_kv, w_o,
           post_mix_g, pre_ffn_g, w_up, conv_w, conv_b, w_down, post_ffn_g):
    depth = w_in.shape[0]
    batch, seq, _ = x_prompt.shape
    dec_batch, dec_seq, _ = x_sample.shape
    past_len = PAST_LEN
    assert batch == 1 and depth == 1

    yp = x_prompt.reshape(batch * seq, D_MODEL)
    ys = x_sample.reshape(dec_batch * dec_seq, D_MODEL)
    outs = {name: [] for name in ('wk_p', 'wv_p', 'mk_p', 'mv_p', 'cv_p', 'wk_s', 'wv_s', 'sgv_s', 'cv_s')}
    for l in range(depth):
        lp = {
            'sinks': attn_sinks[l], 'pre_mix_g': pre_mix_g[l], 'w_in': w_in[l].astype(BF16),
            'sg_ln_g': sg_ln_g[l], 'sg_ln_b': sg_ln_b[l], 'sg_w': sg_w[l], 'sg_b': sg_b[l],
            'w_o': w_o[l].reshape(N_BRANCHES * ATTN_Q, D_MODEL).astype(BF16), 'post_mix_g': post_mix_g[l],
            'pre_ffn_g': pre_ffn_g[l], 'w_up': w_up[l].astype(BF16), 'conv_w': conv_w[l], 'conv_b': conv_b[l],
            'w_down': w_down[l].astype(BF16), 'post_ffn_g': post_ffn_g[l],
        }
        mem_k, mem_v = _memkv_call(mem_prompt.reshape(batch * N_MEM, D_MODEL), mem_norm_g[l],
                                   w_mem_kv[l].astype(BF16))
        outs['mk_p'].append(mem_k.reshape(batch, N_MEM, MEM_HEADS, MEM_HEAD_DIM))
        outs['mv_p'].append(mem_v.reshape(batch, N_MEM, MEM_HEADS, MEM_HEAD_DIM))

        x1, wk, wv = _mix_call(yp, 0, seq, lp, mem_k.astype(BF16), mem_v.astype(BF16), None)
        yp, nc = _ffn_call(x1, seq, lp, None)
        outs['wk_p'].append(wk.reshape(batch, WINDOW, N_KV_HEADS, HEAD_DIM))
        outs['wv_p'].append(wv.reshape(batch, WINDOW, N_KV_HEADS, HEAD_DIM))
        outs['cv_p'].append(nc.reshape(batch, CONV_WIDTH - 1, 2 * D_FF))

        past = cache_win_k.shape[2]
        x1s, wks, wvs, sgv = _mix_call(
            ys, past_len, dec_seq, lp,
            cache_mem_k[l].reshape(dec_batch, N_MEM * MEM_HEADS, MEM_HEAD_DIM),
            cache_mem_v[l].reshape(dec_batch, N_MEM * MEM_HEADS, MEM_HEAD_DIM),
            (cache_win_k[l].reshape(dec_batch, past, ATTN_KV), cache_win_v[l].reshape(dec_batch, past, ATTN_KV)))
        ys, ncs = _ffn_call(x1s, dec_seq, lp, state_conv[l])
        outs['wk_s'].append(wks.reshape(dec_batch, past, N_KV_HEADS, HEAD_DIM))
        outs['wv_s'].append(wvs.reshape(dec_batch, past, N_KV_HEADS, HEAD_DIM))
        outs['sgv_s'].append(sgv.reshape(dec_batch, dec_seq, SG_WIDTH))
        outs['cv_s'].append(ncs)

    return (yp.reshape(batch, seq, D_MODEL), ys.reshape(dec_batch, dec_seq, D_MODEL),
            jnp.stack(outs['wk_p']), jnp.stack(outs['wv_p']), jnp.stack(outs['mk_p']), jnp.stack(outs['mv_p']),
            jnp.stack(outs['cv_p']),
            jnp.stack(outs['wk_s']), jnp.stack(outs['wv_s']), jnp.stack(outs['sgv_s']), jnp.stack(outs['cv_s']))
```

```python
import functools
import math

import jax
import jax.numpy as jnp
import numpy as np
from jax import lax
from jax.experimental import pallas as pl
from jax.experimental.pallas import tpu as pltpu

F32 = jnp.float32
BF16 = jnp.bfloat16

D_MODEL = 1024
N_Q_HEADS = 8
N_KV_HEADS = 2
GROUP = N_Q_HEADS // N_KV_HEADS
HEAD_DIM = 64
ATTN_Q = N_Q_HEADS * HEAD_DIM
ATTN_KV = N_KV_HEADS * HEAD_DIM
WINDOW = 128
ROPE_THETA = 10000.0
CHUNK = 128
SG_GROUPS = 4
SG_GROUP_DIM = 128
SG_WIDTH = SG_GROUPS * SG_GROUP_DIM
N_MEM = 256
MEM_HEADS = 4
MEM_HEAD_DIM = 128
MEM_Q = MEM_HEADS * MEM_HEAD_DIM
N_BRANCHES = 3
D_FF = 2816
CONV_WIDTH = 3
EPS = 1e-6
NEG_INF = -1e30
PAST_LEN = 16384

OFF_Q = 0
OFF_K = ATTN_Q
OFF_V = OFF_K + ATTN_KV
OFF_SGU = OFF_V + ATTN_KV
OFF_SGV = OFF_SGU + SG_WIDTH
OFF_MQ = OFF_SGV + SG_WIDTH
OFF_GATE = OFF_MQ + MEM_Q
IN_WIDTH = OFF_GATE + N_BRANCHES * D_MODEL

LANES = 128
SUBLANES = 8
VMEM_LIMIT_BYTES = 56 * 1024 * 1024

PROMPT_ROWS = 256
SAMPLE_SEQS = 8
SAMPLE_FFN_SEQS = 32
FFN_CHUNK = 512
GATE_CHUNK = 512


def _rms(x, g):
    return x * lax.rsqrt(jnp.mean(x * x, axis=-1, keepdims=True) + EPS) * g


def _dot(a, b):
    return jnp.dot(a, b, preferred_element_type=F32)


def _gelu_erf(x):
    return 0.5 * x * (1.0 + lax.erf(x * np.float32(math.sqrt(0.5))))


def _gelu_tanh(x):
    c = np.float32(math.sqrt(2.0 / math.pi))
    return x * (0.5 * (1.0 + jnp.tanh(c * (x + 0.044715 * (x * x * x)))))


def _lane_iota(shape):
    return lax.broadcasted_iota(jnp.int32, shape, len(shape) - 1)


def _rope(x, cos, sin_signed):
    w = x.shape[1]
    reps = w // LANES
    c = jnp.concatenate([cos] * reps, axis=1) if reps > 1 else cos
    s = jnp.concatenate([sin_signed] * reps, axis=1) if reps > 1 else sin_signed
    first_half = (_lane_iota(x.shape) % HEAD_DIM) < (HEAD_DIM // 2)
    swapped = jnp.where(first_half, pltpu.roll(x, w - HEAD_DIM // 2, 1), pltpu.roll(x, HEAD_DIM // 2, 1))
    return x * c + swapped * s


def _dup_head(kv, g):
    rolled = pltpu.roll(kv, HEAD_DIM, 1)
    low = _lane_iota(kv.shape) < HEAD_DIM
    return jnp.where(low, kv, rolled) if g == 0 else jnp.where(low, rolled, kv)


def _stack_group_queries(q, g):
    low = _lane_iota(q.shape[:2] + (LANES,)) < HEAD_DIM
    parts = []
    for i in range(GROUP):
        h = GROUP * g + i
        pair = q[:, :, (h // 2) * LANES:(h // 2 + 1) * LANES]
        parts.append(jnp.where(low if h % 2 == 0 else ~low, pair, 0.0))
    return jnp.concatenate(parts, axis=1)


def _unstack_group_outputs(o, rows):
    low = _lane_iota((o.shape[0], rows, LANES)) < HEAD_DIM
    pairs = []
    for p in range(GROUP // 2):
        even = o[:, (2 * p) * rows:(2 * p + 1) * rows]
        odd = o[:, (2 * p + 1) * rows:(2 * p + 2) * rows]
        pairs.append(jnp.where(low, even, odd))
    return jnp.concatenate(pairs, axis=2)


def _sink_softmax(s, mask, sink):
    s = jnp.where(mask, s, NEG_INF)
    m = jnp.maximum(jnp.max(s, axis=-1, keepdims=True), sink)
    e = jnp.exp(s - m)
    denom = jnp.sum(e, axis=-1, keepdims=True) + jnp.exp(sink - m)
    return (e * (1.0 / denom)).astype(BF16)


def _softmax(s):
    m = jnp.max(s, axis=-1, keepdims=True)
    e = jnp.exp(s - m)
    return (e * (1.0 / jnp.sum(e, axis=-1, keepdims=True))).astype(BF16)


def _sink_column(sinks_ref, g, rows):
    r = lax.broadcasted_iota(jnp.int32, (1, GROUP * rows, 1), 1)
    col = jnp.full((1, GROUP * rows, 1), sinks_ref[GROUP * g + GROUP - 1], F32)
    for i in range(GROUP - 2, -1, -1):
        col = jnp.where(r < (i + 1) * rows, sinks_ref[GROUP * g + i], col)
    return col


def _mix_kernel(*refs, sample):
    if sample:
        (sinks_ref, x_ref, cos_ref, sin_ref, g_pre_ref, w_in_ref, ln_g_ref, ln_b_ref, wmix_ref, sgb_ref,
         mk_ref, mv_ref, w_o_ref, g_post_ref, ck_ref, cv_ref,
         x1_ref, wk_ref, wv_ref, sgv_ref, gate_ref) = refs
    else:
        (sinks_ref, x_ref, cos_ref, sin_ref, cos_s_ref, sin_s_ref, cos_b_ref, sin_b_ref,
         g_pre_ref, w_in_ref, ln_g_ref, ln_b_ref, wmix_ref, sgb_ref,
         mk_ref, mv_ref, w_o_ref, g_post_ref,
         x1_ref, wk_ref, wv_ref, kcar_ref, vcar_ref, gate_ref) = refs
    rows = x_ref.shape[0]
    seqs = rows // SUBLANES
    step = pl.program_id(0)

    if not sample:
        @pl.when(step == 0)
        def _():
            kcar_ref[...] = jnp.zeros_like(kcar_ref)
            vcar_ref[...] = jnp.zeros_like(vcar_ref)

    x = x_ref[...]
    h = _rms(x, g_pre_ref[...]).astype(BF16)

    gate_pieces = [(n, c) for n in range(N_BRANCHES) for c in range(0, D_MODEL, GATE_CHUNK)]

    def emit_gates(count):
        for _ in range(min(count, len(gate_pieces))):
            n, c = gate_pieces.pop(0)
            col = OFF_GATE + n * D_MODEL + c
            gate_ref[n, :, c:c + GATE_CHUNK] = _dot(h, w_in_ref[:, col:col + GATE_CHUNK])

    zqkv = _dot(h, w_in_ref[:, OFF_Q:OFF_SGU])
    zsg = _dot(h, w_in_ref[:, OFF_SGU:OFF_MQ])
    mq = _dot(h, w_in_ref[:, OFF_MQ:OFF_GATE])

    if sample:
        cos, sin_signed = cos_ref[...], sin_ref[...]
    else:
        cb = cos_b_ref[pl.ds(step, 1), :]
        sb = sin_b_ref[pl.ds(step, 1), :]
        cos = cb * cos_ref[...] - sb * sin_ref[...]
        sin_signed = sb * cos_s_ref[...] + cb * sin_s_ref[...]
    qk = _rope(zqkv[:, :OFF_V], cos, sin_signed)
    q = qk[:, :ATTN_Q]
    k = qk[:, OFF_K:OFF_V]
    v = zqkv[:, OFF_V:OFF_SGU]

    if sample:
        past = ck_ref.shape[1]
        k3 = k.reshape(seqs, SUBLANES, ATTN_KV)
        v3 = v.reshape(seqs, SUBLANES, ATTN_KV)
        ck = ck_ref[...]
        cv = cv_ref[...]
        wk_ref[:, :past - SUBLANES, :] = ck[:, SUBLANES:, :]
        wk_ref[:, past - SUBLANES:, :] = k3
        wv_ref[:, :past - SUBLANES, :] = cv[:, SUBLANES:, :]
        wv_ref[:, past - SUBLANES:, :] = v3
        n_keys = past + 2 * SUBLANES
        pad = jnp.zeros((seqs, SUBLANES, ATTN_KV), F32)
        kk = jnp.concatenate([ck, k3, pad], axis=1).reshape(seqs * n_keys, ATTN_KV)
        vv = jnp.concatenate([cv, v3, pad], axis=1).reshape(seqs * n_keys, ATTN_KV)
        q3 = q.reshape(seqs, SUBLANES, ATTN_Q)
        q_rows = SUBLANES
        qi = lax.broadcasted_iota(jnp.int32, (1, GROUP * SUBLANES, n_keys), 1) % SUBLANES
        kj = lax.broadcasted_iota(jnp.int32, (1, GROUP * SUBLANES, n_keys), 2)
        mask = (kj > qi + (past - WINDOW)) & (kj <= qi + past)
        kds = [_dup_head(kk, g).astype(BF16).reshape(seqs, n_keys, LANES) for g in range(N_KV_HEADS)]
        vds = [_dup_head(vv, g).astype(BF16).reshape(seqs, n_keys, LANES) for g in range(N_KV_HEADS)]
    else:
        nb = rows // WINDOW
        wk_ref[...] = k[rows - WINDOW:]
        wv_ref[...] = v[rows - WINDOW:]
        kk = jnp.concatenate([kcar_ref[...], k], axis=0)
        vv = jnp.concatenate([vcar_ref[...], v], axis=0)
        kcar_ref[...] = k[rows - WINDOW:]
        vcar_ref[...] = v[rows - WINDOW:]
        q3 = q.reshape(nb, WINDOW, ATTN_Q)
        q_rows = WINDOW
        shape = (nb, GROUP * WINDOW, 2 * WINDOW)
        qi = lax.broadcasted_iota(jnp.int32, shape, 1) % WINDOW
        kj = lax.broadcasted_iota(jnp.int32, shape, 2)
        blk = lax.broadcasted_iota(jnp.int32, shape, 0)
        first_key = jnp.where((blk == 0) & (step == 0), WINDOW, 0)
        mask = (kj > qi) & (kj <= qi + WINDOW) & (kj >= first_key)
        kds, vds = [], []
        for g in range(N_KV_HEADS):
            kd = _dup_head(kk, g).astype(BF16)
            vd = _dup_head(vv, g).astype(BF16)
            kds.append(jnp.stack([kd[n * WINDOW:(n + 2) * WINDOW] for n in range(nb)], axis=0))
            vds.append(jnp.stack([vd[n * WINDOW:(n + 2) * WINDOW] for n in range(nb)], axis=0))
    qss = [_stack_group_queries(q3, g).astype(BF16) for g in range(N_KV_HEADS)]

    scores = [jnp.einsum('bqd,bkd->bqk', qss[g], kds[g], preferred_element_type=F32) * (HEAD_DIM ** -0.5)
              for g in range(N_KV_HEADS)]
    emit_gates(2)

    u = _gelu_erf(zsg[:, :SG_WIDTH])
    gv = _gelu_erf(zsg[:, SG_WIDTH:])
    gc = gv - jnp.mean(gv, axis=-1, keepdims=True)
    var = jnp.mean(gc * gc, axis=-1, keepdims=True)
    vn = gc * lax.rsqrt(var + EPS) * ln_g_ref[...] + ln_b_ref[...]
    if sample:
        sgv_ref[...] = vn
    vn_b = vn.astype(BF16)
    probs = [_sink_softmax(scores[g], mask, _sink_column(sinks_ref, g, q_rows)) for g in range(N_KV_HEADS)]

    attn_outs = [jnp.einsum('bqk,bkd->bqd', probs[g], vds[g], preferred_element_type=F32)
                 for g in range(N_KV_HEADS)]
    mem_scale = MEM_HEAD_DIM ** -0.5
    mem_scores, mem_vals = [], []
    for hd in range(MEM_HEADS):
        sl = slice(hd * MEM_HEAD_DIM, (hd + 1) * MEM_HEAD_DIM)
        if sample:
            qh = mq[:, sl].reshape(seqs, SUBLANES, MEM_HEAD_DIM).astype(BF16)
            kh = mk_ref[:, pl.ds(hd, N_MEM, stride=MEM_HEADS), :].astype(BF16)
            mem_vals.append(mv_ref[:, pl.ds(hd, N_MEM, stride=MEM_HEADS), :].astype(BF16))
            s = jnp.einsum('bqd,bkd->bqk', qh, kh, preferred_element_type=F32)
        else:
            qh = mq[:, sl].astype(BF16)
            mem_vals.append(mv_ref[:, sl])
            s = lax.dot_general(qh, mk_ref[:, sl], (((1,), (1,)), ((), ())), preferred_element_type=F32)
        mem_scores.append(s * mem_scale)
    chunk = wmix_ref.shape[1]
    t_idx = lax.broadcasted_iota(jnp.int32, (chunk, chunk), 0)
    s_idx = lax.broadcasted_iota(jnp.int32, (chunk, chunk), 1)
    keep = t_idx >= s_idx
    if sample:
        keep = keep & ((t_idx // SUBLANES) == (s_idx // SUBLANES))
    sg_cols = []
    for g in range(SG_GROUPS):
        wm = jnp.where(keep, wmix_ref[g], 0.0).astype(BF16)
        blocks = [_dot(wm, vn_b[c * chunk:(c + 1) * chunk, g * SG_GROUP_DIM:(g + 1) * SG_GROUP_DIM])
                  for c in range(rows // chunk)]
        sg_cols.append(jnp.concatenate(blocks, axis=0) if len(blocks) > 1 else blocks[0])
    emit_gates(2)

    mem_probs = [_softmax(s) for s in mem_scores]
    attn = jnp.concatenate([_unstack_group_outputs(o, q_rows) for o in attn_outs], axis=2).reshape(rows, ATTN_Q)
    mixed_sg = jnp.concatenate(sg_cols, axis=1)
    bias = sgb_ref[...]
    if rows // chunk > 1:
        bias = jnp.concatenate([bias] * (rows // chunk), axis=0)
    sg = u * (mixed_sg + bias)

    mem_cols = []
    for hd in range(MEM_HEADS):
        if sample:
            o = jnp.einsum('bqk,bkd->bqd', mem_probs[hd], mem_vals[hd], preferred_element_type=F32)
            mem_cols.append(o.reshape(rows, MEM_HEAD_DIM))
        else:
            mem_cols.append(_dot(mem_probs[hd], mem_vals[hd]))
    emit_gates(len(gate_pieces))
    memo = jnp.concatenate(mem_cols, axis=1)

    mixed = None
    for n, br in enumerate((attn, sg, memo)):
        proj = _dot(br.astype(BF16), w_o_ref[n * ATTN_Q:(n + 1) * ATTN_Q, :])
        term = (1.0 / (1.0 + jnp.exp(-gate_ref[n]))) * proj
        mixed = term if mixed is None else mixed + term
    x1_ref[...] = x + _rms(mixed, g_post_ref[...])


def _ffn_kernel(*refs, sample):
    if sample:
        (x_ref, g_pre_ref, w_up_ref, cw_ref, cb_ref, w_down_ref, g_post_ref, st_ref,
         y_ref, nc_ref, *up_s) = refs
    else:
        (x_ref, g_pre_ref, w_up_ref, cw_ref, cb_ref, w_down_ref, g_post_ref,
         y_ref, nc_ref, *up_s) = refs
    rows = x_ref.shape[0]
    step = pl.program_id(0)

    if not sample:
        @pl.when(step == 0)
        def _():
            for slab in up_s:
                slab[:SUBLANES, :] = jnp.zeros((SUBLANES, LANES), F32)

    x = x_ref[...]
    h = _rms(x, g_pre_ref[...]).astype(BF16)

    hist = CONV_WIDTH - 1
    seqs = rows // SUBLANES

    def up_cols(col0, width):
        up = jnp.dot(h, w_up_ref[:, col0:col0 + width], preferred_element_type=F32)
        for t in range(width // LANES):
            s = col0 // LANES + t
            blk = up[:, t * LANES:(t + 1) * LANES]
            if sample:
                up_s[s][:, SUBLANES:, :] = blk.reshape(seqs, SUBLANES, LANES)
                up_s[s][:, SUBLANES - hist:SUBLANES, :] = st_ref[:, :, s * LANES:(s + 1) * LANES]
            else:
                up_s[s][SUBLANES:, :] = blk

    def conv_slab(s):
        cols = slice(s * LANES, (s + 1) * LANES)
        if sample:
            taps = [up_s[s][:, SUBLANES - hist + j:2 * SUBLANES - hist + j, :] for j in range(CONV_WIDTH)]
            nc_ref[:, :, cols] = up_s[s][:, 2 * SUBLANES - hist:, :]
        else:
            taps = [up_s[s][SUBLANES - hist + j:SUBLANES - hist + j + rows, :] for j in range(CONV_WIDTH)]
            nc_ref[:, cols] = up_s[s][SUBLANES + rows - hist:, :]
            up_s[s][:SUBLANES, :] = up_s[s][rows:, :]
        c = cb_ref[:, cols]
        for j in range(CONV_WIDTH):
            c = c + taps[j] * cw_ref[j:j + 1, cols]
        return c.reshape(rows, LANES)

    chunks = [(c0, min(FFN_CHUNK, D_FF - c0)) for c0 in range(0, D_FF, FFN_CHUNK)]
    up_cols(0, chunks[0][1])
    up_cols(D_FF, chunks[0][1])
    f = None
    for k, (c0, width) in enumerate(chunks):
        if k + 1 < len(chunks):
            n0, nw = chunks[k + 1]
            up_cols(n0, nw)
            up_cols(D_FF + n0, nw)
        acts = []
        for t in range(width // LANES):
            gate = conv_slab(c0 // LANES + t)
            val = conv_slab((D_FF + c0) // LANES + t)
            acts.append((_gelu_tanh(gate) * val).astype(BF16))
        act = jnp.concatenate(acts, axis=1)
        part = jnp.dot(act, w_down_ref[c0:c0 + width, :], preferred_element_type=F32)
        f = part if f is None else f + part
    y_ref[...] = x + _rms(f, g_post_ref[...])


def _memkv_kernel(mem_ref, g_ref, w_ref, k_ref, v_ref):
    h = _rms(mem_ref[...], g_ref[...]).astype(BF16)
    kv = _dot(h, w_ref[...])
    k_ref[...] = kv[:, :MEM_Q]
    v_ref[...] = kv[:, MEM_Q:]


def _resident():
    return pl.BlockSpec(memory_space=pltpu.VMEM)


def _row_spec(rows, width):
    return pl.BlockSpec((rows, width), lambda i: (i, 0))


def _const_spec(shape):
    nd = len(shape)
    return pl.BlockSpec(shape, lambda i: (0,) * nd)


def _compiler_params():
    return pltpu.CompilerParams(dimension_semantics=("arbitrary",), vmem_limit_bytes=VMEM_LIMIT_BYTES)


def _rope_tables(pos):
    half = HEAD_DIM // 2
    inv_freq = ROPE_THETA ** (-jnp.arange(half, dtype=F32) / half)
    ang = pos.astype(F32)[:, None] * inv_freq[None, :]
    cos = jnp.concatenate([jnp.cos(ang)] * (LANES // half), axis=1)
    sin = jnp.concatenate([jnp.sin(ang)] * (LANES // half), axis=1)
    sign = jnp.concatenate([-jnp.ones((1, half), F32), jnp.ones((1, half), F32)] * (LANES // HEAD_DIM), axis=1)
    return cos, sin, cos * sign, sin * sign


def _mix_call(x2d, start, seq_len, lp, mem_k, mem_v, caches):
    n_rows = x2d.shape[0]
    sample = caches is not None
    rows = SAMPLE_SEQS * seq_len if sample else PROMPT_ROWS
    steps = n_rows // rows
    if sample:
        pos = start + (jnp.arange(rows, dtype=jnp.int32) % seq_len)
        cos, _, _, sin_signed = _rope_tables(pos)
        rope_args = [cos, sin_signed]
        rope_specs = [_const_spec((rows, LANES))] * 2
        chunk = rows
        wmix = jnp.tile(lp['sg_w'][:, :seq_len, :seq_len], (1, rows // seq_len, rows // seq_len))
        sgb = jnp.repeat(jnp.tile(lp['sg_b'][:, :seq_len], (1, rows // seq_len)).T, SG_GROUP_DIM, axis=1)
    else:
        in_step = _rope_tables(jnp.arange(rows, dtype=jnp.int32))
        cos_b, sin_b, _, _ = _rope_tables(start + rows * jnp.arange(steps, dtype=jnp.int32))
        rope_args = list(in_step) + [cos_b, sin_b]
        rope_specs = [_const_spec((rows, LANES))] * 4 + [_const_spec((steps, LANES))] * 2
        chunk = CHUNK
        wmix = lp['sg_w'][:, :chunk, :chunk]
        sgb = jnp.repeat(lp['sg_b'][:, :chunk].T, SG_GROUP_DIM, axis=1)

    in_specs = [
        pl.BlockSpec(memory_space=pltpu.SMEM),
        _row_spec(rows, D_MODEL),
        *rope_specs,
        _const_spec((1, D_MODEL)),
        _resident(),
        _const_spec((1, SG_WIDTH)), _const_spec((1, SG_WIDTH)),
        _const_spec((SG_GROUPS, chunk, chunk)),
        _const_spec((chunk, SG_WIDTH)),
    ]
    args = [lp['sinks'], x2d, *rope_args, lp['pre_mix_g'].reshape(1, D_MODEL), lp['w_in'],
            lp['sg_ln_g'].reshape(1, SG_WIDTH), lp['sg_ln_b'].reshape(1, SG_WIDTH), wmix, sgb]
    if sample:
        seqs = SAMPLE_SEQS
        in_specs += [pl.BlockSpec((seqs, N_MEM * MEM_HEADS, MEM_HEAD_DIM), lambda i: (i, 0, 0))] * 2
    else:
        in_specs += [_const_spec((N_MEM, MEM_Q))] * 2
    args += [mem_k, mem_v]
    in_specs += [_resident(), _const_spec((1, D_MODEL))]
    args += [lp['w_o'], lp['post_mix_g'].reshape(1, D_MODEL)]

    out_shape = [jax.ShapeDtypeStruct((n_rows, D_MODEL), F32)]
    out_specs = [_row_spec(rows, D_MODEL)]
    scratch = []
    if sample:
        win_k, win_v = caches
        n_seq, past = win_k.shape[0], win_k.shape[1]
        cache_spec = pl.BlockSpec((SAMPLE_SEQS, past, ATTN_KV), lambda i: (i, 0, 0))
        in_specs += [cache_spec, cache_spec]
        args += [win_k, win_v]
        out_shape += [jax.ShapeDtypeStruct((n_seq, past, ATTN_KV), F32)] * 2
        out_specs += [cache_spec, cache_spec]
        out_shape += [jax.ShapeDtypeStruct((n_rows, SG_WIDTH), F32)]
        out_specs += [_row_spec(rows, SG_WIDTH)]
    else:
        out_shape += [jax.ShapeDtypeStruct((WINDOW, ATTN_KV), F32)] * 2
        out_specs += [_const_spec((WINDOW, ATTN_KV))] * 2
        scratch = [pltpu.VMEM((WINDOW, ATTN_KV), F32), pltpu.VMEM((WINDOW, ATTN_KV), F32)]
    scratch += [pltpu.VMEM((N_BRANCHES, rows, D_MODEL), F32)]

    return pl.pallas_call(
        functools.partial(_mix_kernel, sample=sample),
        grid=(steps,),
        in_specs=in_specs,
        out_specs=out_specs,
        out_shape=out_shape,
        scratch_shapes=scratch,
        compiler_params=_compiler_params(),
        name="mix_sample" if sample else "mix_prompt",
    )(*args)


def _ffn_call(x2d, seq_len, lp, conv_state):
    n_rows = x2d.shape[0]
    sample = conv_state is not None
    rows = SAMPLE_FFN_SEQS * seq_len if sample else PROMPT_ROWS
    steps = n_rows // rows
    in_specs = [
        _row_spec(rows, D_MODEL),
        _const_spec((1, D_MODEL)),
        _resident(),
        _const_spec((CONV_WIDTH, 2 * D_FF)),
        _const_spec((1, 2 * D_FF)),
        _resident(),
        _const_spec((1, D_MODEL)),
    ]
    args = [x2d, lp['pre_ffn_g'].reshape(1, D_MODEL), lp['w_up'], lp['conv_w'],
            lp['conv_b'].reshape(1, 2 * D_FF), lp['w_down'], lp['post_ffn_g'].reshape(1, D_MODEL)]
    out_shape = [jax.ShapeDtypeStruct((n_rows, D_MODEL), F32)]
    out_specs = [_row_spec(rows, D_MODEL)]
    n_slabs = 2 * D_FF // LANES
    if sample:
        n_seq = conv_state.shape[0]
        st_spec = pl.BlockSpec((SAMPLE_FFN_SEQS, CONV_WIDTH - 1, 2 * D_FF), lambda i: (i, 0, 0))
        in_specs += [st_spec]
        args += [conv_state]
        out_shape += [jax.ShapeDtypeStruct((n_seq, CONV_WIDTH - 1, 2 * D_FF), F32)]
        out_specs += [st_spec]
        scratch = [pltpu.VMEM((SAMPLE_FFN_SEQS, 2 * SUBLANES, LANES), F32)] * n_slabs
    else:
        out_shape += [jax.ShapeDtypeStruct((CONV_WIDTH - 1, 2 * D_FF), F32)]
        out_specs += [_const_spec((CONV_WIDTH - 1, 2 * D_FF))]
        scratch = [pltpu.VMEM((SUBLANES + rows, LANES), F32)] * n_slabs
    return pl.pallas_call(
        functools.partial(_ffn_kernel, sample=sample),
        grid=(steps,),
        in_specs=in_specs,
        out_specs=out_specs,
        out_shape=out_shape,
        scratch_shapes=scratch,
        compiler_params=_compiler_params(),
        name="ffn_sample" if sample else "ffn_prompt",
    )(*args)


def _memkv_call(mem2d, g, w_bf16):
    return pl.pallas_call(
        _memkv_kernel,
        out_shape=[jax.ShapeDtypeStruct((mem2d.shape[0], MEM_Q), F32)] * 2,
        name="mem_kv",
    )(mem2d, g.reshape(1, D_MODEL), w_bf16)


def kernel(x_prompt, x_sample, cache_win_k, cache_win_v, cache_mem_k, cache_mem_v, state_conv, mem_prompt,
           pre_mix_g, w_in, attn_sinks, sg_ln_g, sg_ln_b, sg_w, sg_b, mem_norm_g, w_mem_kv, w_o,
           post_mix_g, pre_ffn_g, w_up, conv_w, conv_b, w_down, post_ffn_g):
    depth = w_in.shape[0]
    batch, seq, _ = x_prompt.shape
    dec_batch, dec_seq, _ = x_sample.shape
    past_len = PAST_LEN
    assert batch == 1 and depth == 1

    yp = x_prompt.reshape(batch * seq, D_MODEL)
    ys = x_sample.reshape(dec_batch * dec_seq, D_MODEL)
    outs = {name: [] for name in ('wk_p', 'wv_p', 'mk_p', 'mv_p', 'cv_p', 'wk_s', 'wv_s', 'sgv_s', 'cv_s')}
    for l in range(depth):
        lp = {
            'sinks': attn_sinks[l], 'pre_mix_g': pre_mix_g[l], 'w_in': w_in[l].astype(BF16),
            'sg_ln_g': sg_ln_g[l], 'sg_ln_b': sg_ln_b[l], 'sg_w': sg_w[l], 'sg_b': sg_b[l],
            'w_o': w_o[l].reshape(N_BRANCHES * ATTN_Q, D_MODEL).astype(BF16), 'post_mix_g': post_mix_g[l],
            'pre_ffn_g': pre_ffn_g[l], 'w_up': w_up[l].astype(BF16), 'conv_w': conv_w[l], 'conv_b': conv_b[l],
            'w_down': w_down[l].astype(BF16), 'post_ffn_g': post_ffn_g[l],
        }
        mem_k, mem_v = _memkv_call(mem_prompt.reshape(batch * N_MEM, D_MODEL), mem_norm_g[l],
                                   w_mem_kv[l].astype(BF16))
        outs['mk_p'].append(mem_k.reshape(batch, N_MEM, MEM_HEADS, MEM_HEAD_DIM))
        outs['mv_p'].append(mem_v.reshape(batch, N_MEM, MEM_HEADS, MEM_HEAD_DIM))

        x1, wk, wv = _mix_call(yp, 0, seq, lp, mem_k.astype(BF16), mem_v.astype(BF16), None)
        yp, nc = _ffn_call(x1, seq, lp, None)
        outs['wk_p'].append(wk.reshape(batch, WINDOW, N_KV_HEADS, HEAD_DIM))
        outs['wv_p'].append(wv.reshape(batch, WINDOW, N_KV_HEADS, HEAD_DIM))
        outs['cv_p'].append(nc.reshape(batch, CONV_WIDTH - 1, 2 * D_FF))

        past = cache_win_k.shape[2]
        x1s, wks, wvs, sgv = _mix_call(
            ys, past_len, dec_seq, lp,
            cache_mem_k[l].reshape(dec_batch, N_MEM * MEM_HEADS, MEM_HEAD_DIM),
            cache_mem_v[l].reshape(dec_batch, N_MEM * MEM_HEADS, MEM_HEAD_DIM),
            (cache_win_k[l].reshape(dec_batch, past, ATTN_KV), cache_win_v[l].reshape(dec_batch, past, ATTN_KV)))
        ys, ncs = _ffn_call(x1s, dec_seq, lp, state_conv[l])
        outs['wk_s'].append(wks.reshape(dec_batch, past, N_KV_HEADS, HEAD_DIM))
        outs['wv_s'].append(wvs.reshape(dec_batch, past, N_KV_HEADS, HEAD_DIM))
        outs['sgv_s'].append(sgv.reshape(dec_batch, dec_seq, SG_WIDTH))
        outs['cv_s'].append(ncs)

    return (yp.reshape(batch, seq, D_MODEL), ys.reshape(dec_batch, dec_seq, D_MODEL),
            jnp.stack(outs['wk_p']), jnp.stack(outs['wv_p']), jnp.stack(outs['mk_p']), jnp.stack(outs['mv_p']),
            jnp.stack(outs['cv_p']),
            jnp.stack(outs['wk_s']), jnp.stack(outs['wv_s']), jnp.stack(outs['sgv_s']), jnp.stack(outs['cv_s']))
```

```python
import functools
import math

import jax
import jax.numpy as jnp
import numpy as np
from jax import lax
from jax.experimental import pallas as pl
from jax.experimental.pallas import tpu as pltpu

F32 = jnp.float32
BF16 = jnp.bfloat16

D_MODEL = 1024
N_Q_HEADS = 8
N_KV_HEADS = 2
GROUP = N_Q_HEADS // N_KV_HEADS
HEAD_DIM = 64
ATTN_Q = N_Q_HEADS * HEAD_DIM
ATTN_KV = N_KV_HEADS * HEAD_DIM
WINDOW = 128
ROPE_THETA = 10000.0
CHUNK = 128
SG_GROUPS = 4
SG_GROUP_DIM = 128
SG_WIDTH = SG_GROUPS * SG_GROUP_DIM
N_MEM = 256
MEM_HEADS = 4
MEM_HEAD_DIM = 128
MEM_Q = MEM_HEADS * MEM_HEAD_DIM
N_BRANCHES = 3
D_FF = 2816
CONV_WIDTH = 3
EPS = 1e-6
NEG_INF = -1e30
PAST_LEN = 16384

OFF_Q = 0
OFF_K = ATTN_Q
OFF_V = OFF_K + ATTN_KV
OFF_SGU = OFF_V + ATTN_KV
OFF_SGV = OFF_SGU + SG_WIDTH
OFF_MQ = OFF_SGV + SG_WIDTH
OFF_GATE = OFF_MQ + MEM_Q
IN_WIDTH = OFF_GATE + N_BRANCHES * D_MODEL

LANES = 128
SUBLANES = 8
VMEM_LIMIT_BYTES = 56 * 1024 * 1024

PROMPT_ROWS = 256
PROMPT_MIX_ROWS = 512
MIX_SCHEDULE_ONE = ("headA inprojA ropeA scoresA gates2A geluA softmaxA valuesA gates2A outsA memvA mergeA")
MIX_SCHEDULE_TWO = ("headA inprojA ropeA scoresA headB inprojB geluA softmaxA valuesA gates3A ropeB scoresB "
                    "outsA memvA geluB gates3A softmaxB mergeA valuesB gates3B outsB memvB mergeB")
SAMPLE_SEQS = 8
SAMPLE_FFN_SEQS = 32
FFN_CHUNK = 256
FFN_UP_CHUNK = 256
GATE_CHUNK = 512


def _rms(x, g):
    return x * lax.rsqrt(jnp.mean(x * x, axis=-1, keepdims=True) + EPS) * g


def _dot(a, b):
    return jnp.dot(a, b, preferred_element_type=F32)


def _gelu_erf(x):
    return 0.5 * x * (1.0 + lax.erf(x * np.float32(math.sqrt(0.5))))


def _gelu_tanh(x):
    c = np.float32(math.sqrt(2.0 / math.pi))
    return x * (0.5 * (1.0 + jnp.tanh(c * (x + 0.044715 * (x * x * x)))))


def _lane_iota(shape):
    return lax.broadcasted_iota(jnp.int32, shape, len(shape) - 1)


def _rope(x, cos, sin_signed):
    w = x.shape[1]
    reps = w // LANES
    c = jnp.concatenate([cos] * reps, axis=1) if reps > 1 else cos
    s = jnp.concatenate([sin_signed] * reps, axis=1) if reps > 1 else sin_signed
    first_half = (_lane_iota(x.shape) % HEAD_DIM) < (HEAD_DIM // 2)
    swapped = jnp.where(first_half, pltpu.roll(x, w - HEAD_DIM // 2, 1), pltpu.roll(x, HEAD_DIM // 2, 1))
    return x * c + swapped * s


def _dup_head(kv, g):
    rolled = pltpu.roll(kv, HEAD_DIM, 1)
    low = _lane_iota(kv.shape) < HEAD_DIM
    return jnp.where(low, kv, rolled) if g == 0 else jnp.where(low, rolled, kv)


def _stack_group_queries(q, g):
    low = _lane_iota(q.shape[:2] + (LANES,)) < HEAD_DIM
    parts = []
    for i in range(GROUP):
        h = GROUP * g + i
        pair = q[:, :, (h // 2) * LANES:(h // 2 + 1) * LANES]
        parts.append(jnp.where(low if h % 2 == 0 else ~low, pair, 0.0))
    return jnp.concatenate(parts, axis=1)


def _unstack_group_outputs(o, rows):
    low = _lane_iota((o.shape[0], rows, LANES)) < HEAD_DIM
    pairs = []
    for p in range(GROUP // 2):
        even = o[:, (2 * p) * rows:(2 * p + 1) * rows]
        odd = o[:, (2 * p + 1) * rows:(2 * p + 2) * rows]
        pairs.append(jnp.where(low, even, odd))
    return jnp.concatenate(pairs, axis=2)


def _sink_softmax(s, mask, sink):
    s = jnp.where(mask, s, NEG_INF)
    m = jnp.maximum(jnp.max(s, axis=-1, keepdims=True), sink)
    e = jnp.exp(s - m)
    denom = jnp.sum(e, axis=-1, keepdims=True) + jnp.exp(sink - m)
    return (e * (1.0 / denom)).astype(BF16)


def _softmax(s):
    m = jnp.max(s, axis=-1, keepdims=True)
    e = jnp.exp(s - m)
    return (e * (1.0 / jnp.sum(e, axis=-1, keepdims=True))).astype(BF16)


def _sink_column(sinks_ref, g, rows):
    r = lax.broadcasted_iota(jnp.int32, (1, GROUP * rows, 1), 1)
    col = jnp.full((1, GROUP * rows, 1), sinks_ref[GROUP * g + GROUP - 1], F32)
    for i in range(GROUP - 2, -1, -1):
        col = jnp.where(r < (i + 1) * rows, sinks_ref[GROUP * g + i], col)
    return col


def _mix_kernel_tiles(*refs, sample, schedule):
    if sample:
        (sinks_ref, x_ref, cos_ref, sin_ref, g_pre_ref, w_in_ref, ln_g_ref, ln_b_ref, wmix_ref, sgb_ref,
         mk_ref, mv_ref, w_o_ref, g_post_ref, ck_ref, cv_ref,
         x1_ref, wk_ref, wv_ref, sgv_ref, gate_ref) = refs
    else:
        (sinks_ref, x_ref, cos_ref, sin_ref, cos_s_ref, sin_s_ref, cos_b_ref, sin_b_ref,
         g_pre_ref, w_in_ref, ln_g_ref, ln_b_ref, wmix_ref, sgb_ref,
         mk_ref, mv_ref, w_o_ref, g_post_ref,
         x1_ref, wk_ref, wv_ref, kcar_ref, vcar_ref, gate_ref) = refs
    tokens = schedule.split()
    n_sub = 1 + max(ord(tok[-1]) - ord('A') for tok in tokens)
    rows = x_ref.shape[0] // n_sub
    seqs = rows // SUBLANES
    step = pl.program_id(0)
    assert not (sample and n_sub > 1)

    if not sample:
        @pl.when(step == 0)
        def _():
            kcar_ref[...] = jnp.zeros_like(kcar_ref)
            vcar_ref[...] = jnp.zeros_like(vcar_ref)

    st = [dict() for _ in range(n_sub)]
    chunk = wmix_ref.shape[1]
    mem_scale = MEM_HEAD_DIM ** -0.5

    def rows_of(t):
        return slice(t * rows, (t + 1) * rows)

    def head(t):
        s = st[t]
        s['x'] = x_ref[rows_of(t), :]
        s['h'] = _rms(s['x'], g_pre_ref[...]).astype(BF16)
        s['gate_pieces'] = [(n, c) for n in range(N_BRANCHES) for c in range(0, D_MODEL, GATE_CHUNK)]

    def gates(t, count):
        s = st[t]
        for _ in range(min(count, len(s['gate_pieces']))):
            n, c = s['gate_pieces'].pop(0)
            col = OFF_GATE + n * D_MODEL + c
            gate_ref[n, rows_of(t), c:c + GATE_CHUNK] = _dot(s['h'], w_in_ref[:, col:col + GATE_CHUNK])

    def in_proj(t):
        s = st[t]
        s['zqkv'] = _dot(s['h'], w_in_ref[:, OFF_Q:OFF_SGU])
        s['zsg'] = _dot(s['h'], w_in_ref[:, OFF_SGU:OFF_MQ])
        s['mq'] = _dot(s['h'], w_in_ref[:, OFF_MQ:OFF_GATE])

    def rope_stage(t):
        s = st[t]
        zqkv = s.pop('zqkv')
        if sample:
            cos, sin_signed = cos_ref[...], sin_ref[...]
        else:
            cb = cos_b_ref[pl.ds(step, 1), :]
            sb = sin_b_ref[pl.ds(step, 1), :]
            cos = cb * cos_ref[rows_of(t), :] - sb * sin_ref[rows_of(t), :]
            sin_signed = sb * cos_s_ref[rows_of(t), :] + cb * sin_s_ref[rows_of(t), :]
        qk = _rope(zqkv[:, :OFF_V], cos, sin_signed)
        q = qk[:, :ATTN_Q]
        k = qk[:, OFF_K:OFF_V]
        v = zqkv[:, OFF_V:OFF_SGU]
        if sample:
            past = ck_ref.shape[1]
            k3 = k.reshape(seqs, SUBLANES, ATTN_KV)
            v3 = v.reshape(seqs, SUBLANES, ATTN_KV)
            ck = ck_ref[...]
            cv = cv_ref[...]
            wk_ref[:, :past - SUBLANES, :] = ck[:, SUBLANES:, :]
            wk_ref[:, past - SUBLANES:, :] = k3
            wv_ref[:, :past - SUBLANES, :] = cv[:, SUBLANES:, :]
            wv_ref[:, past - SUBLANES:, :] = v3
            n_keys = past + 2 * SUBLANES
            pad = jnp.zeros((seqs, SUBLANES, ATTN_KV), F32)
            kk = jnp.concatenate([ck, k3, pad], axis=1).reshape(seqs * n_keys, ATTN_KV)
            vv = jnp.concatenate([cv, v3, pad], axis=1).reshape(seqs * n_keys, ATTN_KV)
            q3 = q.reshape(seqs, SUBLANES, ATTN_Q)
            s['q_rows'] = SUBLANES
            qi = lax.broadcasted_iota(jnp.int32, (1, GROUP * SUBLANES, n_keys), 1) % SUBLANES
            kj = lax.broadcasted_iota(jnp.int32, (1, GROUP * SUBLANES, n_keys), 2)
            s['mask'] = (kj > qi + (past - WINDOW)) & (kj <= qi + past)
            s['kds'] = [_dup_head(kk, g).astype(BF16).reshape(seqs, n_keys, LANES) for g in range(N_KV_HEADS)]
            s['vds'] = [_dup_head(vv, g).astype(BF16).reshape(seqs, n_keys, LANES) for g in range(N_KV_HEADS)]
        else:
            nb = rows // WINDOW
            k_prev = kcar_ref[...] if t == 0 else st[t - 1]['k_last']
            v_prev = vcar_ref[...] if t == 0 else st[t - 1]['v_last']
            s['k_last'] = k[rows - WINDOW:]
            s['v_last'] = v[rows - WINDOW:]
            if t == n_sub - 1:
                wk_ref[...] = s['k_last']
                wv_ref[...] = s['v_last']
                kcar_ref[...] = s['k_last']
                vcar_ref[...] = s['v_last']
            kk = jnp.concatenate([k_prev, k], axis=0)
            vv = jnp.concatenate([v_prev, v], axis=0)
            q3 = q.reshape(nb, WINDOW, ATTN_Q)
            s['q_rows'] = WINDOW
            shape = (nb, GROUP * WINDOW, 2 * WINDOW)
            qi = lax.broadcasted_iota(jnp.int32, shape, 1) % WINDOW
            kj = lax.broadcasted_iota(jnp.int32, shape, 2)
            s['mask'] = (kj > qi) & (kj <= qi + WINDOW)
            if t == 0:
                blk = lax.broadcasted_iota(jnp.int32, shape, 0)
                first_key = jnp.where((blk == 0) & (step == 0), WINDOW, 0)
                s['mask'] = s['mask'] & (kj >= first_key)
            s['kds'], s['vds'] = [], []
            for g in range(N_KV_HEADS):
                kd = _dup_head(kk, g).astype(BF16)
                vd = _dup_head(vv, g).astype(BF16)
                s['kds'].append(jnp.stack([kd[n * WINDOW:(n + 2) * WINDOW] for n in range(nb)], axis=0))
                s['vds'].append(jnp.stack([vd[n * WINDOW:(n + 2) * WINDOW] for n in range(nb)], axis=0))
        s['qss'] = [_stack_group_queries(q3, g).astype(BF16) for g in range(N_KV_HEADS)]

    def scores(t):
        s = st[t]
        qss, kds = s.pop('qss'), s.pop('kds')
        s['scores'] = [jnp.einsum('bqd,bkd->bqk', qss[g], kds[g], preferred_element_type=F32)
                       * (HEAD_DIM ** -0.5) for g in range(N_KV_HEADS)]

    def gelu_ln(t):
        s = st[t]
        zsg = s.pop('zsg')
        s['u'] = _gelu_erf(zsg[:, :SG_WIDTH])
        gv = _gelu_erf(zsg[:, SG_WIDTH:])
        gc = gv - jnp.mean(gv, axis=-1, keepdims=True)
        var = jnp.mean(gc * gc, axis=-1, keepdims=True)
        vn = gc * lax.rsqrt(var + EPS) * ln_g_ref[...] + ln_b_ref[...]
        if sample:
            sgv_ref[...] = vn
        s['vn_b'] = vn.astype(BF16)

    def softmax(t):
        s = st[t]
        sc, mask = s.pop('scores'), s.pop('mask')
        s['probs'] = [_sink_softmax(sc[g], mask, _sink_column(sinks_ref, g, s['q_rows']))
                      for g in range(N_KV_HEADS)]

    def values(t):
        s = st[t]
        probs, vds, mq, vn_b = s.pop('probs'), s.pop('vds'), s.pop('mq'), s.pop('vn_b')
        s['attn_outs'] = [jnp.einsum('bqk,bkd->bqd', probs[g], vds[g], preferred_element_type=F32)
                          for g in range(N_KV_HEADS)]
        s['mem_scores'], s['mem_vals'] = [], []
        for hd in range(MEM_HEADS):
            sl = slice(hd * MEM_HEAD_DIM, (hd + 1) * MEM_HEAD_DIM)
            if sample:
                qh = mq[:, sl].reshape(seqs, SUBLANES, MEM_HEAD_DIM).astype(BF16)
                kh = mk_ref[:, pl.ds(hd, N_MEM, stride=MEM_HEADS), :].astype(BF16)
                s['mem_vals'].append(mv_ref[:, pl.ds(hd, N_MEM, stride=MEM_HEADS), :].astype(BF16))
                sc = jnp.einsum('bqd,bkd->bqk', qh, kh, preferred_element_type=F32)
            else:
                qh = mq[:, sl].astype(BF16)
                s['mem_vals'].append(mv_ref[:, sl])
                sc = lax.dot_general(qh, mk_ref[:, sl], (((1,), (1,)), ((), ())), preferred_element_type=F32)
            s['mem_scores'].append(sc * mem_scale)
        t_idx = lax.broadcasted_iota(jnp.int32, (chunk, chunk), 0)
        s_idx = lax.broadcasted_iota(jnp.int32, (chunk, chunk), 1)
        keep = t_idx >= s_idx
        if sample:
            keep = keep & ((t_idx // SUBLANES) == (s_idx // SUBLANES))
        s['sg_cols'] = []
        for g in range(SG_GROUPS):
            wm = jnp.where(keep, wmix_ref[g], 0.0).astype(BF16)
            blocks = [_dot(wm, vn_b[c * chunk:(c + 1) * chunk, g * SG_GROUP_DIM:(g + 1) * SG_GROUP_DIM])
                      for c in range(rows // chunk)]
            s['sg_cols'].append(jnp.concatenate(blocks, axis=0) if len(blocks) > 1 else blocks[0])

    def branch_outs(t):
        s = st[t]
        s['mem_probs'] = [_softmax(sc) for sc in s.pop('mem_scores')]
        s['attn'] = jnp.concatenate([_unstack_group_outputs(o, s['q_rows']) for o in s.pop('attn_outs')],
                                    axis=2).reshape(rows, ATTN_Q).astype(BF16)
        mixed_sg = jnp.concatenate(s.pop('sg_cols'), axis=1)
        bias = sgb_ref[...]
        if rows // chunk > 1:
            bias = jnp.concatenate([bias] * (rows // chunk), axis=0)
        s['sg'] = (s.pop('u') * (mixed_sg + bias)).astype(BF16)

    def mem_values(t):
        s = st[t]
        mem_probs, mem_vals = s.pop('mem_probs'), s.pop('mem_vals')
        cols = []
        for hd in range(MEM_HEADS):
            if sample:
                o = jnp.einsum('bqk,bkd->bqd', mem_probs[hd], mem_vals[hd], preferred_element_type=F32)
                cols.append(o.reshape(rows, MEM_HEAD_DIM))
            else:
                cols.append(_dot(mem_probs[hd], mem_vals[hd]))
        s['memo'] = jnp.concatenate(cols, axis=1).astype(BF16)

    def merge(t):
        s = st[t]
        gates(t, len(s['gate_pieces']))
        mixed = None
        for n, name in enumerate(('attn', 'sg', 'memo')):
            proj = _dot(s.pop(name), w_o_ref[n * ATTN_Q:(n + 1) * ATTN_Q, :])
            term = (1.0 / (1.0 + jnp.exp(-gate_ref[n, rows_of(t), :]))) * proj
            mixed = term if mixed is None else mixed + term
        x1_ref[rows_of(t), :] = s.pop('x') + _rms(mixed, g_post_ref[...])

    stages = {'head': head, 'inproj': in_proj, 'rope': rope_stage, 'scores': scores, 'gelu': gelu_ln,
              'softmax': softmax, 'values': values, 'outs': branch_outs, 'memv': mem_values, 'merge': merge}
    for tok in tokens:
        name, t = tok[:-1], ord(tok[-1]) - ord('A')
        if name.startswith('gates'):
            gates(t, int(name[len('gates'):]))
        else:
            stages[name](t)


def _ffn_kernel(*refs, sample):
    if sample:
        (x_ref, g_pre_ref, w_up_ref, cw_ref, cb_ref, w_down_ref, g_post_ref, st_ref,
         y_ref, nc_ref, *up_s) = refs
    else:
        (x_ref, g_pre_ref, w_up_ref, cw_ref, cb_ref, w_down_ref, g_post_ref,
         y_ref, nc_ref, *up_s) = refs
    rows = x_ref.shape[0]
    step = pl.program_id(0)

    if not sample:
        @pl.when(step == 0)
        def _():
            for slab in up_s:
                slab[:SUBLANES, :] = jnp.zeros((SUBLANES, LANES), F32)

    x = x_ref[...]
    h = _rms(x, g_pre_ref[...]).astype(BF16)

    hist = CONV_WIDTH - 1
    seqs = rows // SUBLANES

    def up_cols(col0, width):
        up = jnp.dot(h, w_up_ref[:, col0:col0 + width], preferred_element_type=F32)
        for t in range(width // LANES):
            s = col0 // LANES + t
            blk = up[:, t * LANES:(t + 1) * LANES]
            if sample:
                up_s[s][:, SUBLANES:, :] = blk.reshape(seqs, SUBLANES, LANES)
                up_s[s][:, SUBLANES - hist:SUBLANES, :] = st_ref[:, :, s * LANES:(s + 1) * LANES]
            else:
                up_s[s][SUBLANES:, :] = blk

    def conv_slab(s):
        cols = slice(s * LANES, (s + 1) * LANES)
        if sample:
            taps = [up_s[s][:, SUBLANES - hist + j:2 * SUBLANES - hist + j, :] for j in range(CONV_WIDTH)]
            nc_ref[:, :, cols] = up_s[s][:, 2 * SUBLANES - hist:, :]
        else:
            taps = [up_s[s][SUBLANES - hist + j:SUBLANES - hist + j + rows, :] for j in range(CONV_WIDTH)]
            nc_ref[:, cols] = up_s[s][SUBLANES + rows - hist:, :]
            up_s[s][:SUBLANES, :] = up_s[s][rows:, :]
        c = cb_ref[:, cols]
        for j in range(CONV_WIDTH):
            c = c + taps[j] * cw_ref[j:j + 1, cols]
        return c.reshape(rows, LANES)

    for n0 in range(0, D_FF, FFN_UP_CHUNK):
        nw = min(FFN_UP_CHUNK, D_FF - n0)
        up_cols(n0, nw)
        up_cols(D_FF + n0, nw)
    chunks = [(c0, min(FFN_CHUNK, D_FF - c0)) for c0 in range(0, D_FF, FFN_CHUNK)]
    f = None
    for k, (c0, width) in enumerate(chunks):
        acts = []
        for t in range(width // LANES):
            gate = conv_slab(c0 // LANES + t)
            val = conv_slab((D_FF + c0) // LANES + t)
            acts.append((_gelu_tanh(gate) * val).astype(BF16))
        act = jnp.concatenate(acts, axis=1)
        part = jnp.dot(act, w_down_ref[c0:c0 + width, :], preferred_element_type=F32)
        f = part if f is None else f + part
    y_ref[...] = x + _rms(f, g_post_ref[...])


def _memkv_kernel(mem_ref, g_ref, w_ref, k_ref, v_ref):
    h = _rms(mem_ref[...], g_ref[...]).astype(BF16)
    kv = _dot(h, w_ref[...])
    k_ref[...] = kv[:, :MEM_Q]
    v_ref[...] = kv[:, MEM_Q:]


def _resident():
    return pl.BlockSpec(memory_space=pltpu.VMEM)


def _row_spec(rows, width):
    return pl.BlockSpec((rows, width), lambda i: (i, 0))


def _const_spec(shape):
    nd = len(shape)
    return pl.BlockSpec(shape, lambda i: (0,) * nd)


def _compiler_params():
    return pltpu.CompilerParams(dimension_semantics=("arbitrary",), vmem_limit_bytes=VMEM_LIMIT_BYTES)


def _rope_tables(pos):
    half = HEAD_DIM // 2
    inv_freq = ROPE_THETA ** (-jnp.arange(half, dtype=F32) / half)
    ang = pos.astype(F32)[:, None] * inv_freq[None, :]
    cos = jnp.concatenate([jnp.cos(ang)] * (LANES // half), axis=1)
    sin = jnp.concatenate([jnp.sin(ang)] * (LANES // half), axis=1)
    sign = jnp.concatenate([-jnp.ones((1, half), F32), jnp.ones((1, half), F32)] * (LANES // HEAD_DIM), axis=1)
    return cos, sin, cos * sign, sin * sign


def _mix_call(x2d, start, seq_len, lp, mem_k, mem_v, caches):
    n_rows = x2d.shape[0]
    sample = caches is not None
    rows = SAMPLE_SEQS * seq_len if sample else PROMPT_MIX_ROWS
    schedule = MIX_SCHEDULE_ONE if sample else MIX_SCHEDULE_TWO
    steps = n_rows // rows
    if sample:
        pos = start + (jnp.arange(rows, dtype=jnp.int32) % seq_len)
        cos, _, _, sin_signed = _rope_tables(pos)
        rope_args = [cos, sin_signed]
        rope_specs = [_const_spec((rows, LANES))] * 2
        chunk = rows
        wmix = jnp.tile(lp['sg_w'][:, :seq_len, :seq_len], (1, rows // seq_len, rows // seq_len))
        sgb = jnp.repeat(jnp.tile(lp['sg_b'][:, :seq_len], (1, rows // seq_len)).T, SG_GROUP_DIM, axis=1)
    else:
        in_step = _rope_tables(jnp.arange(rows, dtype=jnp.int32))
        cos_b, sin_b, _, _ = _rope_tables(start + rows * jnp.arange(steps, dtype=jnp.int32))
        rope_args = list(in_step) + [cos_b, sin_b]
        rope_specs = [_const_spec((rows, LANES))] * 4 + [_const_spec((steps, LANES))] * 2
        chunk = CHUNK
        wmix = lp['sg_w'][:, :chunk, :chunk]
        sgb = jnp.repeat(lp['sg_b'][:, :chunk].T, SG_GROUP_DIM, axis=1)

    in_specs = [
        pl.BlockSpec(memory_space=pltpu.SMEM),
        _row_spec(rows, D_MODEL),
        *rope_specs,
        _const_spec((1, D_MODEL)),
        _resident(),
        _const_spec((1, SG_WIDTH)), _const_spec((1, SG_WIDTH)),
        _const_spec((SG_GROUPS, chunk, chunk)),
        _const_spec((chunk, SG_WIDTH)),
    ]
    args = [lp['sinks'], x2d, *rope_args, lp['pre_mix_g'].reshape(1, D_MODEL), lp['w_in'],
            lp['sg_ln_g'].reshape(1, SG_WIDTH), lp['sg_ln_b'].reshape(1, SG_WIDTH), wmix, sgb]
    if sample:
        seqs = SAMPLE_SEQS
        in_specs += [pl.BlockSpec((seqs, N_MEM * MEM_HEADS, MEM_HEAD_DIM), lambda i: (i, 0, 0))] * 2
    else:
        in_specs += [_const_spec((N_MEM, MEM_Q))] * 2
    args += [mem_k, mem_v]
    in_specs += [_resident(), _const_spec((1, D_MODEL))]
    args += [lp['w_o'], lp['post_mix_g'].reshape(1, D_MODEL)]

    out_shape = [jax.ShapeDtypeStruct((n_rows, D_MODEL), F32)]
    out_specs = [_row_spec(rows, D_MODEL)]
    scratch = []
    if sample:
        win_k, win_v = caches
        n_seq, past = win_k.shape[0], win_k.shape[1]
        cache_spec = pl.BlockSpec((SAMPLE_SEQS, past, ATTN_KV), lambda i: (i, 0, 0))
        in_specs += [cache_spec, cache_spec]
        args += [win_k, win_v]
        out_shape += [jax.ShapeDtypeStruct((n_seq, past, ATTN_KV), F32)] * 2
        out_specs += [cache_spec, cache_spec]
        out_shape += [jax.ShapeDtypeStruct((n_rows, SG_WIDTH), F32)]
        out_specs += [_row_spec(rows, SG_WIDTH)]
    else:
        out_shape += [jax.ShapeDtypeStruct((WINDOW, ATTN_KV), F32)] * 2
        out_specs += [_const_spec((WINDOW, ATTN_KV))] * 2
        scratch = [pltpu.VMEM((WINDOW, ATTN_KV), F32), pltpu.VMEM((WINDOW, ATTN_KV), F32)]
    scratch += [pltpu.VMEM((N_BRANCHES, rows, D_MODEL), F32)]

    return pl.pallas_call(
        functools.partial(_mix_kernel_tiles, sample=sample, schedule=schedule),
        grid=(steps,),
        in_specs=in_specs,
        out_specs=out_specs,
        out_shape=out_shape,
        scratch_shapes=scratch,
        compiler_params=_compiler_params(),
        name="mix_sample" if sample else "mix_prompt",
    )(*args)


def _ffn_call(x2d, seq_len, lp, conv_state):
    n_rows = x2d.shape[0]
    sample = conv_state is not None
    rows = SAMPLE_FFN_SEQS * seq_len if sample else PROMPT_ROWS
    steps = n_rows // rows
    in_specs = [
        _row_spec(rows, D_MODEL),
        _const_spec((1, D_MODEL)),
        _resident(),
        _const_spec((CONV_WIDTH, 2 * D_FF)),
        _const_spec((1, 2 * D_FF)),
        _resident(),
        _const_spec((1, D_MODEL)),
    ]
    args = [x2d, lp['pre_ffn_g'].reshape(1, D_MODEL), lp['w_up'], lp['conv_w'],
            lp['conv_b'].reshape(1, 2 * D_FF), lp['w_down'], lp['post_ffn_g'].reshape(1, D_MODEL)]
    out_shape = [jax.ShapeDtypeStruct((n_rows, D_MODEL), F32)]
    out_specs = [_row_spec(rows, D_MODEL)]
    n_slabs = 2 * D_FF // LANES
    if sample:
        n_seq = conv_state.shape[0]
        st_spec = pl.BlockSpec((SAMPLE_FFN_SEQS, CONV_WIDTH - 1, 2 * D_FF), lambda i: (i, 0, 0))
        in_specs += [st_spec]
        args += [conv_state]
        out_shape += [jax.ShapeDtypeStruct((n_seq, CONV_WIDTH - 1, 2 * D_FF), F32)]
        out_specs += [st_spec]
        scratch = [pltpu.VMEM((SAMPLE_FFN_SEQS, 2 * SUBLANES, LANES), F32)] * n_slabs
    else:
        out_shape += [jax.ShapeDtypeStruct((CONV_WIDTH - 1, 2 * D_FF), F32)]
        out_specs += [_const_spec((CONV_WIDTH - 1, 2 * D_FF))]
        scratch = [pltpu.VMEM((SUBLANES + rows, LANES), F32)] * n_slabs
    return pl.pallas_call(
        functools.partial(_ffn_kernel, sample=sample),
        grid=(steps,),
        in_specs=in_specs,
        out_specs=out_specs,
        out_shape=out_shape,
        scratch_shapes=scratch,
        compiler_params=_compiler_params(),
        name="ffn_sample" if sample else "ffn_prompt",
    )(*args)


def _memkv_call(mem2d, g, w_bf16):
    return pl.pallas_call(
        _memkv_kernel,
        out_shape=[jax.ShapeDtypeStruct((mem2d.shape[0], MEM_Q), F32)] * 2,
        name="mem_kv",
    )(mem2d, g.reshape(1, D_MODEL), w_bf16)


def kernel(x_prompt, x_sample, cache_win_k, cache_win_v, cache_mem_k, cache_mem_v, state_conv, mem_prompt,
           pre_mix_g, w_in, attn_sinks, sg_ln_g, sg_ln_b, sg_w, sg_b, mem_norm_g, w_mem_kv, w_o,
           post_mix_g, pre_ffn_g, w_up, conv_w, conv_b, w_down, post_ffn_g):
    depth = w_in.shape[0]
    batch, seq, _ = x_prompt.shape
    dec_batch, dec_seq, _ = x_sample.shape
    past_len = PAST_LEN
    assert batch == 1 and depth == 1

    yp = x_prompt.reshape(batch * seq, D_MODEL)
    ys = x_sample.reshape(dec_batch * dec_seq, D_MODEL)
    outs = {name: [] for name in ('wk_p', 'wv_p', 'mk_p', 'mv_p', 'cv_p', 'wk_s', 'wv_s', 'sgv_s', 'cv_s')}
    for l in range(depth):
        lp = {
            'sinks': attn_sinks[l], 'pre_mix_g': pre_mix_g[l], 'w_in': w_in[l].astype(BF16),
            'sg_ln_g': sg_ln_g[l], 'sg_ln_b': sg_ln_b[l], 'sg_w': sg_w[l], 'sg_b': sg_b[l],
            'w_o': w_o[l].reshape(N_BRANCHES * ATTN_Q, D_MODEL).astype(BF16), 'post_mix_g': post_mix_g[l],
            'pre_ffn_g': pre_ffn_g[l], 'w_up': w_up[l].astype(BF16), 'conv_w': conv_w[l], 'conv_b': conv_b[l],
            'w_down': w_down[l].astype(BF16), 'post_ffn_g': post_ffn_g[l],
        }
        mem_k, mem_v = _memkv_call(mem_prompt.reshape(batch * N_MEM, D_MODEL), mem_norm_g[l],
                                   w_mem_kv[l].astype(BF16))
        outs['mk_p'].append(mem_k.reshape(batch, N_MEM, MEM_HEADS, MEM_HEAD_DIM))
        outs['mv_p'].append(mem_v.reshape(batch, N_MEM, MEM_HEADS, MEM_HEAD_DIM))

        x1, wk, wv = _mix_call(yp, 0, seq, lp, mem_k.astype(BF16), mem_v.astype(BF16), None)
        yp, nc = _ffn_call(x1, seq, lp, None)
        outs['wk_p'].append(wk.reshape(batch, WINDOW, N_KV_HEADS, HEAD_DIM))
        outs['wv_p'].append(wv.reshape(batch, WINDOW, N_KV_HEADS, HEAD_DIM))
        outs['cv_p'].append(nc.reshape(batch, CONV_WIDTH - 1, 2 * D_FF))

        past = cache_win_k.shape[2]
        x1s, wks, wvs, sgv = _mix_call(
            ys, past_len, dec_seq, lp,
            cache_mem_k[l].reshape(dec_batch, N_MEM * MEM_HEADS, MEM_HEAD_DIM),
            cache_mem_v[l].reshape(dec_batch, N_MEM * MEM_HEADS, MEM_HEAD_DIM),
            (cache_win_k[l].reshape(dec_batch, past, ATTN_KV), cache_win_v[l].reshape(dec_batch, past, ATTN_KV)))
        ys, ncs = _ffn_call(x1s, dec_seq, lp, state_conv[l])
        outs['wk_s'].append(wks.reshape(dec_batch, past, N_KV_HEADS, HEAD_DIM))
        outs['wv_s'].append(wvs.reshape(dec_batch, past, N_KV_HEADS, HEAD_DIM))
        outs['sgv_s'].append(sgv.reshape(dec_batch, dec_seq, SG_WIDTH))
        outs['cv_s'].append(ncs)

    return (yp.reshape(batch, seq, D_MODEL), ys.reshape(dec_batch, dec_seq, D_MODEL),
            jnp.stack(outs['wk_p']), jnp.stack(outs['wv_p']), jnp.stack(outs['mk_p']), jnp.stack(outs['mv_p']),
            jnp.stack(outs['cv_p']),
            jnp.stack(outs['wk_s']), jnp.stack(outs['wv_s']), jnp.stack(outs['sgv_s']), jnp.stack(outs['cv_s']))
```

```python
import functools
import math

import jax
import jax.numpy as jnp
import numpy as np
from jax import lax
from jax.experimental import pallas as pl
from jax.experimental.pallas import tpu as pltpu

F32 = jnp.float32
BF16 = jnp.bfloat16

D_MODEL = 1024
N_Q_HEADS = 8
N_KV_HEADS = 2
GROUP = N_Q_HEADS // N_KV_HEADS
HEAD_DIM = 64
ATTN_Q = N_Q_HEADS * HEAD_DIM
ATTN_KV = N_KV_HEADS * HEAD_DIM
WINDOW = 128
ROPE_THETA = 10000.0
CHUNK = 128
SG_GROUPS = 4
SG_GROUP_DIM = 128
SG_WIDTH = SG_GROUPS * SG_GROUP_DIM
N_MEM = 256
MEM_HEADS = 4
MEM_HEAD_DIM = 128
MEM_Q = MEM_HEADS * MEM_HEAD_DIM
N_BRANCHES = 3
D_FF = 2816
CONV_WIDTH = 3
EPS = 1e-6
NEG_INF = -1e30
LOG2E = math.log2(math.e)
PAST_LEN = 16384

OFF_Q = 0
OFF_K = ATTN_Q
OFF_V = OFF_K + ATTN_KV
OFF_SGU = OFF_V + ATTN_KV
OFF_SGV = OFF_SGU + SG_WIDTH
OFF_MQ = OFF_SGV + SG_WIDTH
OFF_GATE = OFF_MQ + MEM_Q
IN_WIDTH = OFF_GATE + N_BRANCHES * D_MODEL

LANES = 128
SUBLANES = 8
VMEM_LIMIT_BYTES = 56 * 1024 * 1024

PROMPT_ROWS = 512
PROMPT_MIX_ROWS = 512
MIX_SCHEDULE_ONE = ("headA inprojA ropeA scoresA gates2A geluA softmaxA valuesA gates2A outsA memvA mergeA")
MIX_SCHEDULE_TWO = ("headA inprojA ropeA scoresA headB inprojB geluA softmaxA valuesA gates3A ropeB scoresB "
                    "outsA memvA geluB gates3A softmaxB mergeA valuesB gates3B outsB memvB mergeB")
SAMPLE_SEQS = 8
SAMPLE_FFN_SEQS = 32
FFN_CHUNK = 256
FFN_UP_CHUNK = 256
GATE_CHUNK = 512


def _rms(x, g):
    return x * lax.rsqrt(jnp.mean(x * x, axis=-1, keepdims=True) + EPS) * g


def _dot(a, b):
    return jnp.dot(a, b, preferred_element_type=F32)


def _gelu_erf(x):
    return 0.5 * x * (1.0 + lax.erf(x * np.float32(math.sqrt(0.5))))


def _gelu_tanh(x):
    c = np.float32(math.sqrt(2.0 / math.pi))
    return x * (0.5 * (1.0 + jnp.tanh(c * (x + 0.044715 * (x * x * x)))))


def _lane_iota(shape):
    return lax.broadcasted_iota(jnp.int32, shape, len(shape) - 1)


def _rope(x, cos, sin_signed):
    w = x.shape[1]
    reps = w // LANES
    c = jnp.concatenate([cos] * reps, axis=1) if reps > 1 else cos
    s = jnp.concatenate([sin_signed] * reps, axis=1) if reps > 1 else sin_signed
    first_half = (_lane_iota(x.shape) % HEAD_DIM) < (HEAD_DIM // 2)
    swapped = jnp.where(first_half, pltpu.roll(x, w - HEAD_DIM // 2, 1), pltpu.roll(x, HEAD_DIM // 2, 1))
    return x * c + swapped * s


def _dup_head(kv, g):
    rolled = pltpu.roll(kv, HEAD_DIM, 1)
    low = _lane_iota(kv.shape) < HEAD_DIM
    return jnp.where(low, kv, rolled) if g == 0 else jnp.where(low, rolled, kv)


def _stack_group_queries(q, g):
    low = _lane_iota(q.shape[:2] + (LANES,)) < HEAD_DIM
    parts = []
    for i in range(GROUP):
        h = GROUP * g + i
        pair = q[:, :, (h // 2) * LANES:(h // 2 + 1) * LANES]
        parts.append(jnp.where(low if h % 2 == 0 else ~low, pair, 0.0))
    return jnp.concatenate(parts, axis=1)


def _unstack_group_outputs(o, rows):
    low = _lane_iota((o.shape[0], rows, LANES)) < HEAD_DIM
    pairs = []
    for p in range(GROUP // 2):
        even = o[:, (2 * p) * rows:(2 * p + 1) * rows]
        odd = o[:, (2 * p + 1) * rows:(2 * p + 2) * rows]
        pairs.append(jnp.where(low, even, odd))
    return jnp.concatenate(pairs, axis=2)


def _sink_softmax(s, scale, mask, sink):
    t = jnp.where(mask, s * np.float32(scale * LOG2E), NEG_INF)
    sink2 = sink * np.float32(LOG2E)
    m = jnp.maximum(jnp.max(t, axis=-1, keepdims=True), sink2)
    e = jnp.exp2(t - m)
    denom = jnp.sum(e, axis=-1, keepdims=True) + jnp.exp2(sink2 - m)
    return (e * (1.0 / denom)).astype(BF16)


def _softmax(s, scale):
    t = s * np.float32(scale * LOG2E)
    m = jnp.max(t, axis=-1, keepdims=True)
    e = jnp.exp2(t - m)
    return (e * (1.0 / jnp.sum(e, axis=-1, keepdims=True))).astype(BF16)


def _sigmoid(x):
    return 1.0 / (1.0 + jnp.exp2(x * np.float32(-LOG2E)))


def _sink_column(sinks_ref, g, rows):
    r = lax.broadcasted_iota(jnp.int32, (1, GROUP * rows, 1), 1)
    col = jnp.full((1, GROUP * rows, 1), sinks_ref[GROUP * g + GROUP - 1], F32)
    for i in range(GROUP - 2, -1, -1):
        col = jnp.where(r < (i + 1) * rows, sinks_ref[GROUP * g + i], col)
    return col


def _mix_kernel_tiles(*refs, sample, schedule):
    if sample:
        (sinks_ref, x_ref, cos_ref, sin_ref, g_pre_ref, w_in_ref, ln_g_ref, ln_b_ref, wmix_ref, sgb_ref,
         mk_ref, mv_ref, w_o_ref, g_post_ref, ck_ref, cv_ref,
         x1_ref, wk_ref, wv_ref, sgv_ref, gate_ref) = refs
    else:
        (sinks_ref, x_ref, cos_ref, sin_ref, cos_s_ref, sin_s_ref, cos_b_ref, sin_b_ref,
         g_pre_ref, w_in_ref, ln_g_ref, ln_b_ref, wmix_ref, sgb_ref,
         mk_ref, mv_ref, w_o_ref, g_post_ref,
         x1_ref, wk_ref, wv_ref, kcar_ref, vcar_ref, gate_ref) = refs
    tokens = schedule.split()
    n_sub = 1 + max(ord(tok[-1]) - ord('A') for tok in tokens)
    rows = x_ref.shape[0] // n_sub
    seqs = rows // SUBLANES
    step = pl.program_id(0)
    assert not (sample and n_sub > 1)

    if not sample:
        @pl.when(step == 0)
        def _():
            kcar_ref[...] = jnp.zeros_like(kcar_ref)
            vcar_ref[...] = jnp.zeros_like(vcar_ref)

    st = [dict() for _ in range(n_sub)]
    chunk = wmix_ref.shape[1]
    mem_scale = MEM_HEAD_DIM ** -0.5

    def rows_of(t):
        return slice(t * rows, (t + 1) * rows)

    def head(t):
        s = st[t]
        s['x'] = x_ref[rows_of(t), :]
        s['h'] = _rms(s['x'], g_pre_ref[...]).astype(BF16)
        s['gate_pieces'] = [(n, c) for n in range(N_BRANCHES) for c in range(0, D_MODEL, GATE_CHUNK)]

    def gates(t, count):
        s = st[t]
        for _ in range(min(count, len(s['gate_pieces']))):
            n, c = s['gate_pieces'].pop(0)
            col = OFF_GATE + n * D_MODEL + c
            gate_ref[n, rows_of(t), c:c + GATE_CHUNK] = _dot(s['h'], w_in_ref[:, col:col + GATE_CHUNK])

    def in_proj(t):
        s = st[t]
        s['zqkv'] = _dot(s['h'], w_in_ref[:, OFF_Q:OFF_SGU])
        s['zsg'] = _dot(s['h'], w_in_ref[:, OFF_SGU:OFF_MQ])
        s['mq'] = _dot(s['h'], w_in_ref[:, OFF_MQ:OFF_GATE])

    def rope_stage(t):
        s = st[t]
        zqkv = s.pop('zqkv')
        if sample:
            cos, sin_signed = cos_ref[...], sin_ref[...]
        else:
            cb = cos_b_ref[pl.ds(step, 1), :]
            sb = sin_b_ref[pl.ds(step, 1), :]
            cos = cb * cos_ref[rows_of(t), :] - sb * sin_ref[rows_of(t), :]
            sin_signed = sb * cos_s_ref[rows_of(t), :] + cb * sin_s_ref[rows_of(t), :]
        qk = _rope(zqkv[:, :OFF_V], cos, sin_signed)
        q = qk[:, :ATTN_Q]
        k = qk[:, OFF_K:OFF_V]
        v = zqkv[:, OFF_V:OFF_SGU]
        if sample:
            past = ck_ref.shape[1]
            k3 = k.reshape(seqs, SUBLANES, ATTN_KV)
            v3 = v.reshape(seqs, SUBLANES, ATTN_KV)
            ck = ck_ref[...]
            cv = cv_ref[...]
            wk_ref[:, :past - SUBLANES, :] = ck[:, SUBLANES:, :]
            wk_ref[:, past - SUBLANES:, :] = k3
            wv_ref[:, :past - SUBLANES, :] = cv[:, SUBLANES:, :]
            wv_ref[:, past - SUBLANES:, :] = v3
            n_keys = past + 2 * SUBLANES
            pad = jnp.zeros((seqs, SUBLANES, ATTN_KV), F32)
            kk = jnp.concatenate([ck, k3, pad], axis=1).reshape(seqs * n_keys, ATTN_KV)
            vv = jnp.concatenate([cv, v3, pad], axis=1).reshape(seqs * n_keys, ATTN_KV)
            q3 = q.reshape(seqs, SUBLANES, ATTN_Q)
            s['q_rows'] = SUBLANES
            qi = lax.broadcasted_iota(jnp.int32, (1, GROUP * SUBLANES, n_keys), 1) % SUBLANES
            kj = lax.broadcasted_iota(jnp.int32, (1, GROUP * SUBLANES, n_keys), 2)
            s['mask'] = (kj > qi + (past - WINDOW)) & (kj <= qi + past)
            s['kds'] = [_dup_head(kk, g).astype(BF16).reshape(seqs, n_keys, LANES) for g in range(N_KV_HEADS)]
            s['vds'] = [_dup_head(vv, g).astype(BF16).reshape(seqs, n_keys, LANES) for g in range(N_KV_HEADS)]
        else:
            nb = rows // WINDOW
            k_prev = kcar_ref[...] if t == 0 else st[t - 1]['k_last']
            v_prev = vcar_ref[...] if t == 0 else st[t - 1]['v_last']
            s['k_last'] = k[rows - WINDOW:]
            s['v_last'] = v[rows - WINDOW:]
            if t == n_sub - 1:
                wk_ref[...] = s['k_last'].T
                wv_ref[...] = s['v_last'].T
                kcar_ref[...] = s['k_last']
                vcar_ref[...] = s['v_last']
            kk = jnp.concatenate([k_prev, k], axis=0)
            vv = jnp.concatenate([v_prev, v], axis=0)
            q3 = q.reshape(nb, WINDOW, ATTN_Q)
            s['q_rows'] = WINDOW
            shape = (nb, GROUP * WINDOW, 2 * WINDOW)
            qi = lax.broadcasted_iota(jnp.int32, shape, 1) % WINDOW
            kj = lax.broadcasted_iota(jnp.int32, shape, 2)
            s['mask'] = (kj > qi) & (kj <= qi + WINDOW)
            if t == 0:
                blk = lax.broadcasted_iota(jnp.int32, shape, 0)
                first_key = jnp.where((blk == 0) & (step == 0), WINDOW, 0)
                s['mask'] = s['mask'] & (kj >= first_key)
            s['kds'], s['vds'] = [], []
            for g in range(N_KV_HEADS):
                kd = _dup_head(kk, g).astype(BF16)
                vd = _dup_head(vv, g).astype(BF16)
                s['kds'].append(jnp.stack([kd[n * WINDOW:(n + 2) * WINDOW] for n in range(nb)], axis=0))
                s['vds'].append(jnp.stack([vd[n * WINDOW:(n + 2) * WINDOW] for n in range(nb)], axis=0))
        s['qss'] = [_stack_group_queries(q3, g).astype(BF16) for g in range(N_KV_HEADS)]

    def scores(t):
        s = st[t]
        qss, kds = s.pop('qss'), s.pop('kds')
        s['scores'] = [jnp.einsum('bqd,bkd->bqk', qss[g], kds[g], preferred_element_type=F32)
                       for g in range(N_KV_HEADS)]

    def gelu_ln(t):
        s = st[t]
        zsg = s.pop('zsg')
        s['u'] = _gelu_erf(zsg[:, :SG_WIDTH])
        gv = _gelu_erf(zsg[:, SG_WIDTH:])
        gc = gv - jnp.mean(gv, axis=-1, keepdims=True)
        var = jnp.mean(gc * gc, axis=-1, keepdims=True)
        vn = gc * lax.rsqrt(var + EPS) * ln_g_ref[...] + ln_b_ref[...]
        if sample:
            sgv_ref[...] = vn
        s['vn_b'] = vn.astype(BF16)

    def softmax(t):
        s = st[t]
        sc, mask = s.pop('scores'), s.pop('mask')
        s['probs'] = [_sink_softmax(sc[g], HEAD_DIM ** -0.5, mask, _sink_column(sinks_ref, g, s['q_rows']))
                      for g in range(N_KV_HEADS)]

    def values(t):
        s = st[t]
        probs, vds, mq, vn_b = s.pop('probs'), s.pop('vds'), s.pop('mq'), s.pop('vn_b')
        s['attn_outs'] = [jnp.einsum('bqk,bkd->bqd', probs[g], vds[g], preferred_element_type=F32)
                          for g in range(N_KV_HEADS)]
        s['mem_scores'], s['mem_vals'] = [], []
        for hd in range(MEM_HEADS):
            sl = slice(hd * MEM_HEAD_DIM, (hd + 1) * MEM_HEAD_DIM)
            if sample:
                qh = mq[:, sl].reshape(seqs, SUBLANES, MEM_HEAD_DIM).astype(BF16)
                kh = mk_ref[:, pl.ds(hd, N_MEM, stride=MEM_HEADS), :].astype(BF16)
                s['mem_vals'].append(mv_ref[:, pl.ds(hd, N_MEM, stride=MEM_HEADS), :].astype(BF16))
                sc = jnp.einsum('bqd,bkd->bqk', qh, kh, preferred_element_type=F32)
            else:
                qh = mq[:, sl].astype(BF16)
                s['mem_vals'].append(mv_ref[:, sl])
                sc = lax.dot_general(qh, mk_ref[:, sl], (((1,), (1,)), ((), ())), preferred_element_type=F32)
            s['mem_scores'].append(sc)
        t_idx = lax.broadcasted_iota(jnp.int32, (chunk, chunk), 0)
        s_idx = lax.broadcasted_iota(jnp.int32, (chunk, chunk), 1)
        keep = t_idx >= s_idx
        if sample:
            keep = keep & ((t_idx // SUBLANES) == (s_idx // SUBLANES))
        s['sg_cols'] = []
        for g in range(SG_GROUPS):
            wm = jnp.where(keep, wmix_ref[g], 0.0).astype(BF16)
            blocks = [_dot(wm, vn_b[c * chunk:(c + 1) * chunk, g * SG_GROUP_DIM:(g + 1) * SG_GROUP_DIM])
                      for c in range(rows // chunk)]
            s['sg_cols'].append(jnp.concatenate(blocks, axis=0) if len(blocks) > 1 else blocks[0])

    def branch_outs(t):
        s = st[t]
        s['mem_probs'] = [_softmax(sc, mem_scale) for sc in s.pop('mem_scores')]
        s['attn'] = jnp.concatenate([_unstack_group_outputs(o, s['q_rows']) for o in s.pop('attn_outs')],
                                    axis=2).reshape(rows, ATTN_Q).astype(BF16)
        mixed_sg = jnp.concatenate(s.pop('sg_cols'), axis=1)
        bias = sgb_ref[...]
        if rows // chunk > 1:
            bias = jnp.concatenate([bias] * (rows // chunk), axis=0)
        s['sg'] = (s.pop('u') * (mixed_sg + bias)).astype(BF16)

    def mem_values(t):
        s = st[t]
        mem_probs, mem_vals = s.pop('mem_probs'), s.pop('mem_vals')
        cols = []
        for hd in range(MEM_HEADS):
            if sample:
                o = jnp.einsum('bqk,bkd->bqd', mem_probs[hd], mem_vals[hd], preferred_element_type=F32)
                cols.append(o.reshape(rows, MEM_HEAD_DIM))
            else:
                cols.append(_dot(mem_probs[hd], mem_vals[hd]))
        s['memo'] = jnp.concatenate(cols, axis=1).astype(BF16)

    def merge(t):
        s = st[t]
        gates(t, len(s['gate_pieces']))
        mixed = None
        for n, name in enumerate(('attn', 'sg', 'memo')):
            proj = _dot(s.pop(name), w_o_ref[n * ATTN_Q:(n + 1) * ATTN_Q, :])
            term = _sigmoid(gate_ref[n, rows_of(t), :]) * proj
            mixed = term if mixed is None else mixed + term
        x1_ref[rows_of(t), :] = s.pop('x') + _rms(mixed, g_post_ref[...])

    stages = {'head': head, 'inproj': in_proj, 'rope': rope_stage, 'scores': scores, 'gelu': gelu_ln,
              'softmax': softmax, 'values': values, 'outs': branch_outs, 'memv': mem_values, 'merge': merge}
    for tok in tokens:
        name, t = tok[:-1], ord(tok[-1]) - ord('A')
        if name.startswith('gates'):
            gates(t, int(name[len('gates'):]))
        else:
            stages[name](t)


def _ffn_kernel(*refs, sample):
    if sample:
        (x_ref, g_pre_ref, w_up_ref, cw_ref, cb_ref, w_down_ref, g_post_ref, st_ref,
         y_ref, nc_ref, *up_s) = refs
    else:
        (x_ref, g_pre_ref, w_up_ref, cw_ref, cb_ref, w_down_ref, g_post_ref,
         y_ref, nc_ref, *up_s) = refs
    rows = x_ref.shape[0]
    step = pl.program_id(0)

    if not sample:
        @pl.when(step == 0)
        def _():
            for slab in up_s:
                slab[:SUBLANES, :] = jnp.zeros((SUBLANES, LANES), F32)

    x = x_ref[...]
    h = _rms(x, g_pre_ref[...]).astype(BF16)

    hist = CONV_WIDTH - 1
    seqs = rows // SUBLANES

    def up_cols(col0, width):
        up = jnp.dot(h, w_up_ref[:, col0:col0 + width], preferred_element_type=F32)
        for t in range(width // LANES):
            s = col0 // LANES + t
            blk = up[:, t * LANES:(t + 1) * LANES]
            if sample:
                up_s[s][:, SUBLANES:, :] = blk.reshape(seqs, SUBLANES, LANES)
                up_s[s][:, SUBLANES - hist:SUBLANES, :] = st_ref[:, :, s * LANES:(s + 1) * LANES]
            else:
                up_s[s][SUBLANES:, :] = blk

    def conv_slab(s):
        cols = slice(s * LANES, (s + 1) * LANES)
        if sample:
            taps = [up_s[s][:, SUBLANES - hist + j:2 * SUBLANES - hist + j, :] for j in range(CONV_WIDTH)]
            nc_ref[:, :, cols] = up_s[s][:, 2 * SUBLANES - hist:, :]
        else:
            taps = [up_s[s][SUBLANES - hist + j:SUBLANES - hist + j + rows, :] for j in range(CONV_WIDTH)]
            nc_ref[:, cols] = up_s[s][SUBLANES + rows - hist:, :]
            up_s[s][:SUBLANES, :] = up_s[s][rows:, :]
        c = cb_ref[:, cols]
        for j in range(CONV_WIDTH):
            c = c + taps[j] * cw_ref[j:j + 1, cols]
        return c.reshape(rows, LANES)

    for n0 in range(0, D_FF, FFN_UP_CHUNK):
        nw = min(FFN_UP_CHUNK, D_FF - n0)
        up_cols(n0, nw)
        up_cols(D_FF + n0, nw)
    chunks = [(c0, min(FFN_CHUNK, D_FF - c0)) for c0 in range(0, D_FF, FFN_CHUNK)]
    f = None
    for k, (c0, width) in enumerate(chunks):
        acts = []
        for t in range(width // LANES):
            gate = conv_slab(c0 // LANES + t)
            val = conv_slab((D_FF + c0) // LANES + t)
            acts.append((_gelu_tanh(gate) * val).astype(BF16))
        act = jnp.concatenate(acts, axis=1)
        part = jnp.dot(act, w_down_ref[c0:c0 + width, :], preferred_element_type=F32)
        f = part if f is None else f + part
    y_ref[...] = x + _rms(f, g_post_ref[...])


def _memkv_kernel(mem_ref, g_ref, w_ref, k_ref, v_ref, kb_ref, vb_ref):
    h = _rms(mem_ref[...], g_ref[...]).astype(BF16)
    kv = _dot(h, w_ref[...].astype(BF16))
    kb_ref[...] = kv[:, :MEM_Q].astype(BF16)
    vb_ref[...] = kv[:, MEM_Q:].astype(BF16)
    for hd in range(MEM_HEADS):
        k_ref[pl.ds(hd, N_MEM, stride=MEM_HEADS), :] = kv[:, hd * MEM_HEAD_DIM:(hd + 1) * MEM_HEAD_DIM]
        v_ref[pl.ds(hd, N_MEM, stride=MEM_HEADS), :] = kv[:, MEM_Q + hd * MEM_HEAD_DIM:
                                                          MEM_Q + (hd + 1) * MEM_HEAD_DIM]


def _resident():
    return pl.BlockSpec(memory_space=pltpu.VMEM)


def _row_spec(rows, width):
    return pl.BlockSpec((rows, width), lambda i: (i, 0))


def _const_spec(shape):
    nd = len(shape)
    return pl.BlockSpec(shape, lambda i: (0,) * nd)


def _compiler_params():
    return pltpu.CompilerParams(dimension_semantics=("arbitrary",), vmem_limit_bytes=VMEM_LIMIT_BYTES)


def _rope_tables(pos):
    half = HEAD_DIM // 2
    inv_freq = np.float64(ROPE_THETA) ** (-np.arange(half, dtype=np.float64) / half)
    ang = pos.astype(np.float64)[:, None] * inv_freq[None, :]
    cos = np.concatenate([np.cos(ang)] * (LANES // half), axis=1).astype(np.float32)
    sin = np.concatenate([np.sin(ang)] * (LANES // half), axis=1).astype(np.float32)
    sign = np.concatenate([-np.ones((1, half), np.float32), np.ones((1, half), np.float32)]
                          * (LANES // HEAD_DIM), axis=1)
    return cos, sin, cos * sign, sin * sign


def _mix_call(x2d, start, seq_len, lp, mem_k, mem_v, caches):
    n_rows = x2d.shape[0]
    sample = caches is not None
    rows = SAMPLE_SEQS * seq_len if sample else PROMPT_MIX_ROWS
    schedule = MIX_SCHEDULE_ONE if sample else MIX_SCHEDULE_TWO
    steps = n_rows // rows
    if sample:
        pos = start + (np.arange(rows, dtype=np.int32) % seq_len)
        cos, _, _, sin_signed = _rope_tables(pos)
        rope_args = [cos, sin_signed]
        rope_specs = [_const_spec((rows, LANES))] * 2
        chunk = rows
        wmix = jnp.tile(lp['sg_w'][:, :seq_len, :seq_len], (1, rows // seq_len, rows // seq_len))
        sgb = jnp.repeat(jnp.tile(lp['sg_b'][:, :seq_len], (1, rows // seq_len)).T, SG_GROUP_DIM, axis=1)
    else:
        in_step = _rope_tables(np.arange(rows, dtype=np.int32))
        cos_b, sin_b, _, _ = _rope_tables(start + rows * np.arange(steps, dtype=np.int32))
        rope_args = list(in_step) + [cos_b, sin_b]
        rope_specs = [_const_spec((rows, LANES))] * 4 + [_const_spec((steps, LANES))] * 2
        chunk = CHUNK
        wmix = lp['sg_w'][:, :chunk, :chunk]
        sgb = jnp.repeat(lp['sg_b'][:, :chunk].T, SG_GROUP_DIM, axis=1)

    in_specs = [
        pl.BlockSpec(memory_space=pltpu.SMEM),
        _row_spec(rows, D_MODEL),
        *rope_specs,
        _const_spec((1, D_MODEL)),
        _resident(),
        _const_spec((1, SG_WIDTH)), _const_spec((1, SG_WIDTH)),
        _const_spec((SG_GROUPS, chunk, chunk)),
        _const_spec((chunk, SG_WIDTH)),
    ]
    args = [lp['sinks'], x2d, *rope_args, lp['pre_mix_g'].reshape(1, D_MODEL), lp['w_in'],
            lp['sg_ln_g'].reshape(1, SG_WIDTH), lp['sg_ln_b'].reshape(1, SG_WIDTH), wmix, sgb]
    if sample:
        seqs = SAMPLE_SEQS
        in_specs += [pl.BlockSpec((seqs, N_MEM * MEM_HEADS, MEM_HEAD_DIM), lambda i: (i, 0, 0))] * 2
    else:
        in_specs += [_const_spec((N_MEM, MEM_Q))] * 2
    args += [mem_k, mem_v]
    in_specs += [_resident(), _const_spec((1, D_MODEL))]
    args += [lp['w_o'], lp['post_mix_g'].reshape(1, D_MODEL)]

    out_shape = [jax.ShapeDtypeStruct((n_rows, D_MODEL), F32)]
    out_specs = [_row_spec(rows, D_MODEL)]
    scratch = []
    if sample:
        win_k, win_v = caches
        n_seq, past = win_k.shape[0], win_k.shape[1]
        cache_spec = pl.BlockSpec((SAMPLE_SEQS, past, ATTN_KV), lambda i: (i, 0, 0))
        in_specs += [cache_spec, cache_spec]
        args += [win_k, win_v]
        out_shape += [jax.ShapeDtypeStruct((n_seq, past, ATTN_KV), F32)] * 2
        out_specs += [cache_spec, cache_spec]
        out_shape += [jax.ShapeDtypeStruct((n_rows, SG_WIDTH), F32)]
        out_specs += [_row_spec(rows, SG_WIDTH)]
    else:
        out_shape += [jax.ShapeDtypeStruct((WINDOW, ATTN_KV), F32)] * 2
        out_specs += [_const_spec((WINDOW, ATTN_KV))] * 2
        scratch = [pltpu.VMEM((WINDOW, ATTN_KV), F32), pltpu.VMEM((WINDOW, ATTN_KV), F32)]
    scratch += [pltpu.VMEM((N_BRANCHES, rows, D_MODEL), F32)]

    return pl.pallas_call(
        functools.partial(_mix_kernel_tiles, sample=sample, schedule=schedule),
        grid=(steps,),
        in_specs=in_specs,
        out_specs=out_specs,
        out_shape=out_shape,
        scratch_shapes=scratch,
        compiler_params=_compiler_params(),
        name="mix_sample" if sample else "mix_prompt",
    )(*args)


def _ffn_call(x2d, seq_len, lp, conv_state):
    n_rows = x2d.shape[0]
    sample = conv_state is not None
    rows = SAMPLE_FFN_SEQS * seq_len if sample else PROMPT_ROWS
    steps = n_rows // rows
    in_specs = [
        _row_spec(rows, D_MODEL),
        _const_spec((1, D_MODEL)),
        _resident(),
        _const_spec((CONV_WIDTH, 2 * D_FF)),
        _const_spec((1, 2 * D_FF)),
        _resident(),
        _const_spec((1, D_MODEL)),
    ]
    args = [x2d, lp['pre_ffn_g'].reshape(1, D_MODEL), lp['w_up'], lp['conv_w'],
            lp['conv_b'].reshape(1, 2 * D_FF), lp['w_down'], lp['post_ffn_g'].reshape(1, D_MODEL)]
    out_shape = [jax.ShapeDtypeStruct((n_rows, D_MODEL), F32)]
    out_specs = [_row_spec(rows, D_MODEL)]
    n_slabs = 2 * D_FF // LANES
    if sample:
        n_seq = conv_state.shape[0]
        st_spec = pl.BlockSpec((SAMPLE_FFN_SEQS, CONV_WIDTH - 1, 2 * D_FF), lambda i: (i, 0, 0))
        in_specs += [st_spec]
        args += [conv_state]
        out_shape += [jax.ShapeDtypeStruct((n_seq, CONV_WIDTH - 1, 2 * D_FF), F32)]
        out_specs += [st_spec]
        scratch = [pltpu.VMEM((SAMPLE_FFN_SEQS, 2 * SUBLANES, LANES), F32)] * n_slabs
    else:
        out_shape += [jax.ShapeDtypeStruct((CONV_WIDTH - 1, 2 * D_FF), F32)]
        out_specs += [_const_spec((CONV_WIDTH - 1, 2 * D_FF))]
        scratch = [pltpu.VMEM((SUBLANES + rows, LANES), F32)] * n_slabs
    return pl.pallas_call(
        functools.partial(_ffn_kernel, sample=sample),
        grid=(steps,),
        in_specs=in_specs,
        out_specs=out_specs,
        out_shape=out_shape,
        scratch_shapes=scratch,
        compiler_params=_compiler_params(),
        name="ffn_sample" if sample else "ffn_prompt",
    )(*args)


def _memkv_call(mem2d, g, w_f32):
    return pl.pallas_call(
        _memkv_kernel,
        out_shape=[jax.ShapeDtypeStruct((mem2d.shape[0] * MEM_HEADS, MEM_HEAD_DIM), F32)] * 2
        + [jax.ShapeDtypeStruct((mem2d.shape[0], MEM_Q), BF16)] * 2,
        compiler_params=pltpu.CompilerParams(vmem_limit_bytes=VMEM_LIMIT_BYTES),
        name="mem_kv",
    )(mem2d, g.reshape(1, D_MODEL), w_f32)


def kernel(x_prompt, x_sample, cache_win_k, cache_win_v, cache_mem_k, cache_mem_v, state_conv, mem_prompt,
           pre_mix_g, w_in, attn_sinks, sg_ln_g, sg_ln_b, sg_w, sg_b, mem_norm_g, w_mem_kv, w_o,
           post_mix_g, pre_ffn_g, w_up, conv_w, conv_b, w_down, post_ffn_g):
    depth = w_in.shape[0]
    batch, seq, _ = x_prompt.shape
    dec_batch, dec_seq, _ = x_sample.shape
    past_len = PAST_LEN
    assert batch == 1 and depth == 1

    yp = x_prompt.reshape(batch * seq, D_MODEL)
    ys = x_sample.reshape(dec_batch * dec_seq, D_MODEL)
    outs = {name: [] for name in ('wk_p', 'wv_p', 'mk_p', 'mv_p', 'cv_p', 'wk_s', 'wv_s', 'sgv_s', 'cv_s')}
    for l in range(depth):
        lp = {
            'sinks': attn_sinks[l], 'pre_mix_g': pre_mix_g[l], 'w_in': w_in[l].astype(BF16),
            'sg_ln_g': sg_ln_g[l], 'sg_ln_b': sg_ln_b[l], 'sg_w': sg_w[l], 'sg_b': sg_b[l],
            'w_o': w_o[l].reshape(N_BRANCHES * ATTN_Q, D_MODEL).astype(BF16), 'post_mix_g': post_mix_g[l],
            'pre_ffn_g': pre_ffn_g[l], 'w_up': w_up[l].astype(BF16), 'conv_w': conv_w[l], 'conv_b': conv_b[l],
            'w_down': w_down[l].astype(BF16), 'post_ffn_g': post_ffn_g[l],
        }
        mem_k, mem_v, mem_k_b, mem_v_b = _memkv_call(mem_prompt.reshape(batch * N_MEM, D_MODEL), mem_norm_g[l],
                                                     w_mem_kv[l])
        outs['mk_p'].append(mem_k.reshape(batch, N_MEM, MEM_HEADS, MEM_HEAD_DIM))
        outs['mv_p'].append(mem_v.reshape(batch, N_MEM, MEM_HEADS, MEM_HEAD_DIM))

        x1, wk, wv = _mix_call(yp, 0, seq, lp, mem_k_b, mem_v_b, None)
        yp, nc = _ffn_call(x1, seq, lp, None)
        outs['wk_p'].append(wk.reshape(batch, N_KV_HEADS, HEAD_DIM, WINDOW).transpose(0, 3, 1, 2))
        outs['wv_p'].append(wv.reshape(batch, N_KV_HEADS, HEAD_DIM, WINDOW).transpose(0, 3, 1, 2))
        outs['cv_p'].append(nc.reshape(batch, CONV_WIDTH - 1, 2 * D_FF))

        past = cache_win_k.shape[2]
        x1s, wks, wvs, sgv = _mix_call(
            ys, past_len, dec_seq, lp,
            cache_mem_k[l].reshape(dec_batch, N_MEM * MEM_HEADS, MEM_HEAD_DIM),
            cache_mem_v[l].reshape(dec_batch, N_MEM * MEM_HEADS, MEM_HEAD_DIM),
            (cache_win_k[l].reshape(dec_batch, past, ATTN_KV), cache_win_v[l].reshape(dec_batch, past, ATTN_KV)))
        ys, ncs = _ffn_call(x1s, dec_seq, lp, state_conv[l])
        outs['wk_s'].append(wks.reshape(dec_batch, past, N_KV_HEADS, HEAD_DIM))
        outs['wv_s'].append(wvs.reshape(dec_batch, past, N_KV_HEADS, HEAD_DIM))
        outs['sgv_s'].append(sgv.reshape(dec_batch, dec_seq, SG_WIDTH))
        outs['cv_s'].append(ncs)

    return (yp.reshape(batch, seq, D_MODEL), ys.reshape(dec_batch, dec_seq, D_MODEL),
            jnp.stack(outs['wk_p']), jnp.stack(outs['wv_p']), jnp.stack(outs['mk_p']), jnp.stack(outs['mv_p']),
            jnp.stack(outs['cv_p']),
            jnp.stack(outs['wk_s']), jnp.stack(outs['wv_s']), jnp.stack(outs['sgv_s']), jnp.stack(outs['cv_s']))
```

```python
import functools
import math

import jax
import jax.numpy as jnp
import numpy as np
from jax import lax
from jax.experimental import pallas as pl
from jax.experimental.pallas import tpu as pltpu

F32 = jnp.float32
BF16 = jnp.bfloat16

D_MODEL = 1024
N_Q_HEADS = 8
N_KV_HEADS = 2
GROUP = N_Q_HEADS // N_KV_HEADS
HEAD_DIM = 64
ATTN_Q = N_Q_HEADS * HEAD_DIM
ATTN_KV = N_KV_HEADS * HEAD_DIM
WINDOW = 128
ROPE_THETA = 10000.0
CHUNK = 128
SG_GROUPS = 4
SG_GROUP_DIM = 128
SG_WIDTH = SG_GROUPS * SG_GROUP_DIM
N_MEM = 256
MEM_HEADS = 4
MEM_HEAD_DIM = 128
MEM_Q = MEM_HEADS * MEM_HEAD_DIM
N_BRANCHES = 3
D_FF = 2816
CONV_WIDTH = 3
EPS = 1e-6
NEG_INF = -1e30
LOG2E = math.log2(math.e)
PAST_LEN = 16384

OFF_Q = 0
OFF_K = ATTN_Q
OFF_V = OFF_K + ATTN_KV
OFF_SGU = OFF_V + ATTN_KV
OFF_SGV = OFF_SGU + SG_WIDTH
OFF_MQ = OFF_SGV + SG_WIDTH
OFF_GATE = OFF_MQ + MEM_Q
IN_WIDTH = OFF_GATE + N_BRANCHES * D_MODEL

LANES = 128
SUBLANES = 8
VMEM_LIMIT_BYTES = 56 * 1024 * 1024

PROMPT_ROWS = 512
PROMPT_MIX_ROWS = 512
MIX_SCHEDULE_ONE = ("headA inprojA ropeA scoresA gates2A geluA softmaxA valuesA gates2A outsA memvA mergeA")
MIX_SCHEDULE_TWO = ("headA inprojA ropeA scoresA headB inprojB geluA softmaxA valuesA gates3A ropeB scoresB "
                    "outsA sigA memvA geluB gates3A softmaxB sigA mergeA valuesB gates6B outsB memvB sigB mergeB")
SAMPLE_SEQS = 8
SAMPLE_FFN_SEQS = 32
FFN_CHUNK = 256
FFN_UP_CHUNK = 256
GATE_CHUNK = 512

def _rms(x, g):
    return x * lax.rsqrt(jnp.mean(x * x, axis=-1, keepdims=True) + EPS) * g


def _dot(a, b):
    return jnp.dot(a, b, preferred_element_type=F32)


def _gelu_erf(x):
    return 0.5 * x * (1.0 + lax.erf(x * np.float32(math.sqrt(0.5))))


def _gelu_tanh(x):
    c = np.float32(math.sqrt(2.0 / math.pi))
    return x * (0.5 * (1.0 + jnp.tanh(c * (x + 0.044715 * (x * x * x)))))


def _lane_iota(shape):
    return lax.broadcasted_iota(jnp.int32, shape, len(shape) - 1)


def _rope(x, cos, sin_signed):
    w = x.shape[1]
    reps = w // LANES
    c = jnp.concatenate([cos] * reps, axis=1) if reps > 1 else cos
    s = jnp.concatenate([sin_signed] * reps, axis=1) if reps > 1 else sin_signed
    first_half = (_lane_iota(x.shape) % HEAD_DIM) < (HEAD_DIM // 2)
    swapped = jnp.where(first_half, pltpu.roll(x, w - HEAD_DIM // 2, 1), pltpu.roll(x, HEAD_DIM // 2, 1))
    return x * c + swapped * s


def _dup_head(kv, g):
    rolled = pltpu.roll(kv, HEAD_DIM, 1)
    low = _lane_iota(kv.shape) < HEAD_DIM
    return jnp.where(low, kv, rolled) if g == 0 else jnp.where(low, rolled, kv)


def _stack_group_queries(q, g):
    low = _lane_iota(q.shape[:2] + (LANES,)) < HEAD_DIM
    parts = []
    for i in range(GROUP):
        h = GROUP * g + i
        pair = q[:, :, (h // 2) * LANES:(h // 2 + 1) * LANES]
        parts.append(jnp.where(low if h % 2 == 0 else ~low, pair, 0.0))
    return jnp.concatenate(parts, axis=1)


def _unstack_group_outputs(o, rows):
    low = _lane_iota((o.shape[0], rows, LANES)) < HEAD_DIM
    pairs = []
    for p in range(GROUP // 2):
        even = o[:, (2 * p) * rows:(2 * p + 1) * rows]
        odd = o[:, (2 * p + 1) * rows:(2 * p + 2) * rows]
        pairs.append(jnp.where(low, even, odd))
    return jnp.concatenate(pairs, axis=2)


def _sink_softmax(s, scale, mask, sink):
    t = jnp.where(mask, s * np.float32(scale * LOG2E), NEG_INF)
    sink2 = sink * np.float32(LOG2E)
    m = jnp.maximum(jnp.max(t, axis=-1, keepdims=True), sink2)
    e = jnp.exp2(t - m)
    denom = jnp.sum(e, axis=-1, keepdims=True) + jnp.exp2(sink2 - m)
    return (e * (1.0 / denom)).astype(BF16)


def _softmax(s, scale):
    t = s * np.float32(scale * LOG2E)
    m = jnp.max(t, axis=-1, keepdims=True)
    e = jnp.exp2(t - m)
    return (e * (1.0 / jnp.sum(e, axis=-1, keepdims=True))).astype(BF16)


def _sigmoid(x):
    return 1.0 / (1.0 + jnp.exp2(x * np.float32(-LOG2E)))


def _sink_column(sinks_ref, g, rows):
    r = lax.broadcasted_iota(jnp.int32, (1, GROUP * rows, 1), 1)
    col = jnp.full((1, GROUP * rows, 1), sinks_ref[GROUP * g + GROUP - 1], F32)
    for i in range(GROUP - 2, -1, -1):
        col = jnp.where(r < (i + 1) * rows, sinks_ref[GROUP * g + i], col)
    return col


def _mix_kernel_tiles(*refs, sample, schedule):
    if sample:
        (sinks_ref, x_ref, cos_ref, sin_ref, g_pre_ref, w_in_ref, ln_g_ref, ln_b_ref, wmix_ref, sgb_ref,
         mk_ref, mv_ref, w_o_ref, g_post_ref, ck_ref, cv_ref,
         x1_ref, wk_ref, wv_ref, sgv_ref, gate_ref) = refs
    else:
        (sinks_ref, x_ref, cos_ref, sin_ref, cos_s_ref, sin_s_ref, cos_b_ref, sin_b_ref,
         g_pre_ref, w_in_ref, ln_g_ref, ln_b_ref, wmix_ref, sgb_ref,
         mk_ref, mv_ref, w_o_ref, g_post_ref,
         x1_ref, wk_ref, wv_ref, kcar_ref, vcar_ref, gate_ref) = refs
    tokens = schedule.split()
    n_sub = 1 + max(ord(tok[-1]) - ord('A') for tok in tokens)
    rows = x_ref.shape[0] // n_sub
    seqs = rows // SUBLANES
    step = pl.program_id(0)
    assert not (sample and n_sub > 1)

    if not sample:
        @pl.when(step == 0)
        def _():
            kcar_ref[...] = jnp.zeros_like(kcar_ref)
            vcar_ref[...] = jnp.zeros_like(vcar_ref)

    st = [dict() for _ in range(n_sub)]
    chunk = wmix_ref.shape[1]
    mem_scale = MEM_HEAD_DIM ** -0.5

    def rows_of(t):
        return slice(t * rows, (t + 1) * rows)

    def head(t):
        s = st[t]
        s['x'] = x_ref[rows_of(t), :]
        s['h'] = _rms(s['x'], g_pre_ref[...]).astype(BF16)
        s['gate_pieces'] = [(n, c) for n in range(N_BRANCHES) for c in range(0, D_MODEL, GATE_CHUNK)]
        s['logit_pieces'] = []

    def gates(t, count):
        s = st[t]
        for _ in range(min(count, len(s['gate_pieces']))):
            n, c = s['gate_pieces'].pop(0)
            col = OFF_GATE + n * D_MODEL + c
            gate_ref[n, rows_of(t), c:c + GATE_CHUNK] = _dot(s['h'], w_in_ref[:, col:col + GATE_CHUNK])
            s['logit_pieces'].append((n, c))

    def gate_sigmoids(t):
        s = st[t]
        while s['logit_pieces']:
            n, c = s['logit_pieces'].pop(0)
            piece = (n, rows_of(t), slice(c, c + GATE_CHUNK))
            gate_ref[piece] = _sigmoid(gate_ref[piece])

    def in_proj(t):
        s = st[t]
        s['zqkv'] = _dot(s['h'], w_in_ref[:, OFF_Q:OFF_SGU])
        s['zsg'] = _dot(s['h'], w_in_ref[:, OFF_SGU:OFF_MQ])
        s['mq'] = _dot(s['h'], w_in_ref[:, OFF_MQ:OFF_GATE])

    def rope_stage(t):
        s = st[t]
        zqkv = s.pop('zqkv')
        if sample:
            cos, sin_signed = cos_ref[...], sin_ref[...]
        else:
            cb = cos_b_ref[pl.ds(step, 1), :]
            sb = sin_b_ref[pl.ds(step, 1), :]
            cos = cb * cos_ref[rows_of(t), :] - sb * sin_ref[rows_of(t), :]
            sin_signed = sb * cos_s_ref[rows_of(t), :] + cb * sin_s_ref[rows_of(t), :]
        qk = _rope(zqkv[:, :OFF_V], cos, sin_signed)
        q = qk[:, :ATTN_Q]
        k = qk[:, OFF_K:OFF_V]
        v = zqkv[:, OFF_V:OFF_SGU]
        if sample:
            past = ck_ref.shape[1]
            k3 = k.reshape(seqs, SUBLANES, ATTN_KV)
            v3 = v.reshape(seqs, SUBLANES, ATTN_KV)
            ck = ck_ref[...]
            cv = cv_ref[...]
            wk_ref[:, :past - SUBLANES, :] = ck[:, SUBLANES:, :]
            wk_ref[:, past - SUBLANES:, :] = k3
            wv_ref[:, :past - SUBLANES, :] = cv[:, SUBLANES:, :]
            wv_ref[:, past - SUBLANES:, :] = v3
            n_keys = past + 2 * SUBLANES
            pad = jnp.zeros((seqs, SUBLANES, ATTN_KV), F32)
            kk = jnp.concatenate([ck, k3, pad], axis=1).reshape(seqs * n_keys, ATTN_KV)
            vv = jnp.concatenate([cv, v3, pad], axis=1).reshape(seqs * n_keys, ATTN_KV)
            q3 = q.reshape(seqs, SUBLANES, ATTN_Q)
            s['q_rows'] = SUBLANES
            qi = lax.broadcasted_iota(jnp.int32, (1, GROUP * SUBLANES, n_keys), 1) % SUBLANES
            kj = lax.broadcasted_iota(jnp.int32, (1, GROUP * SUBLANES, n_keys), 2)
            s['mask'] = (kj > qi + (past - WINDOW)) & (kj <= qi + past)
            s['kds'] = [_dup_head(kk, g).astype(BF16).reshape(seqs, n_keys, LANES) for g in range(N_KV_HEADS)]
            s['vds'] = [_dup_head(vv, g).astype(BF16).reshape(seqs, n_keys, LANES) for g in range(N_KV_HEADS)]
        else:
            nb = rows // WINDOW
            k_prev = kcar_ref[...] if t == 0 else st[t - 1]['k_last']
            v_prev = vcar_ref[...] if t == 0 else st[t - 1]['v_last']
            s['k_last'] = k[rows - WINDOW:]
            s['v_last'] = v[rows - WINDOW:]
            if t == n_sub - 1:
                wk_ref[...] = s['k_last'].T
                wv_ref[...] = s['v_last'].T
                kcar_ref[...] = s['k_last']
                vcar_ref[...] = s['v_last']
            kk = jnp.concatenate([k_prev, k], axis=0)
            vv = jnp.concatenate([v_prev, v], axis=0)
            q3 = q.reshape(nb, WINDOW, ATTN_Q)
            s['q_rows'] = WINDOW
            shape = (nb, GROUP * WINDOW, 2 * WINDOW)
            qi = lax.broadcasted_iota(jnp.int32, shape, 1) % WINDOW
            kj = lax.broadcasted_iota(jnp.int32, shape, 2)
            s['mask'] = (kj > qi) & (kj <= qi + WINDOW)
            if t == 0:
                blk = lax.broadcasted_iota(jnp.int32, shape, 0)
                first_key = jnp.where((blk == 0) & (step == 0), WINDOW, 0)
                s['mask'] = s['mask'] & (kj >= first_key)
            s['kds'], s['vds'] = [], []
            for g in range(N_KV_HEADS):
                kd = _dup_head(kk, g).astype(BF16)
                vd = _dup_head(vv, g).astype(BF16)
                s['kds'].append(jnp.stack([kd[n * WINDOW:(n + 2) * WINDOW] for n in range(nb)], axis=0))
                s['vds'].append(jnp.stack([vd[n * WINDOW:(n + 2) * WINDOW] for n in range(nb)], axis=0))
        s['qss'] = [_stack_group_queries(q3, g).astype(BF16) for g in range(N_KV_HEADS)]
        s['sinks'] = [_sink_column(sinks_ref, g, s['q_rows']) for g in range(N_KV_HEADS)]

    def scores(t):
        s = st[t]
        qss, kds = s.pop('qss'), s.pop('kds')
        s['scores'] = [jnp.einsum('bqd,bkd->bqk', q, k, preferred_element_type=F32) for q, k in zip(qss, kds)]

    def gelu_ln(t, part=None):
        s = st[t]
        if part in (None, 0):
            s['u'] = _gelu_erf(s['zsg'][:, :SG_WIDTH])
        if part in (None, 1):
            gv = _gelu_erf(s['zsg'][:, SG_WIDTH:])
            gc = gv - jnp.mean(gv, axis=-1, keepdims=True)
            var = jnp.mean(gc * gc, axis=-1, keepdims=True)
            vn = gc * lax.rsqrt(var + EPS) * ln_g_ref[...] + ln_b_ref[...]
            if sample:
                sgv_ref[...] = vn
            s['vn_b'] = vn.astype(BF16)

    def softmax(t, g=None):
        s = st[t]
        for gi in range(len(s['scores'])) if g is None else (g,):
            s.setdefault('probs', {})[gi] = _sink_softmax(s['scores'][gi], HEAD_DIM ** -0.5, s['mask'],
                                                          s['sinks'][gi])

    def values(t):
        s = st[t]
        probs, vds, mq, vn_b = s.pop('probs'), s.pop('vds'), s.pop('mq'), s.pop('vn_b')
        s['attn_outs'] = [jnp.einsum('bqk,bkd->bqd', probs[g], vds[g], preferred_element_type=F32)
                          for g in range(len(vds))]
        s['mem_scores'], s['mem_vals'] = [], []
        if sample:
            mq3 = mq.reshape(seqs, SUBLANES, MEM_Q)
            lane_head = _lane_iota(mq3.shape) // MEM_HEAD_DIM
            q_all = jnp.concatenate([jnp.where(lane_head == hd, mq3, 0.0) for hd in range(MEM_HEADS)],
                                    axis=1).astype(BF16)
            k_all = jnp.concatenate([mk_ref[:, pl.ds(hd, N_MEM, stride=MEM_HEADS), :] for hd in range(MEM_HEADS)],
                                    axis=2).astype(BF16)
            v_all = jnp.concatenate([mv_ref[:, pl.ds(hd, N_MEM, stride=MEM_HEADS), :] for hd in range(MEM_HEADS)],
                                    axis=2).astype(BF16)
            s['mem_scores'].append(jnp.einsum('bqd,bkd->bqk', q_all, k_all, preferred_element_type=F32))
            s['mem_vals'].append(v_all)
        else:
            for hd in range(MEM_HEADS):
                sl = slice(hd * MEM_HEAD_DIM, (hd + 1) * MEM_HEAD_DIM)
                s['mem_vals'].append(mv_ref[:, sl])
                s['mem_scores'].append(lax.dot_general(mq[:, sl].astype(BF16), mk_ref[:, sl],
                                                       (((1,), (1,)), ((), ())), preferred_element_type=F32))
        t_idx = lax.broadcasted_iota(jnp.int32, (chunk, chunk), 0)
        s_idx = lax.broadcasted_iota(jnp.int32, (chunk, chunk), 1)
        keep = t_idx >= s_idx
        if sample:
            keep = keep & ((t_idx // SUBLANES) == (s_idx // SUBLANES))
        s['sg_cols'] = []
        for g in range(SG_GROUPS):
            wm = jnp.where(keep, wmix_ref[g], 0.0).astype(BF16)
            blocks = [_dot(wm, vn_b[c * chunk:(c + 1) * chunk, g * SG_GROUP_DIM:(g + 1) * SG_GROUP_DIM])
                      for c in range(rows // chunk)]
            s['sg_cols'].append(jnp.concatenate(blocks, axis=0) if len(blocks) > 1 else blocks[0])

    def branch_outs(t):
        s = st[t]
        s['mem_probs'] = [_softmax(sc, mem_scale) for sc in s.pop('mem_scores')]
        s['attn'] = jnp.concatenate([_unstack_group_outputs(o, s['q_rows']) for o in s.pop('attn_outs')],
                                    axis=2).reshape(rows, ATTN_Q).astype(BF16)
        mixed_sg = jnp.concatenate(s.pop('sg_cols'), axis=1)
        bias = sgb_ref[...]
        if rows // chunk > 1:
            bias = jnp.concatenate([bias] * (rows // chunk), axis=0)
        s['sg'] = (s.pop('u') * (mixed_sg + bias)).astype(BF16)

    def mem_values(t):
        s = st[t]
        mem_probs, mem_vals = s.pop('mem_probs'), s.pop('mem_vals')
        if sample:
            o = jnp.einsum('bqk,bkd->bqd', mem_probs[0], mem_vals[0], preferred_element_type=F32)
            cols = [o[:, hd * SUBLANES:(hd + 1) * SUBLANES, hd * MEM_HEAD_DIM:(hd + 1) * MEM_HEAD_DIM]
                    .reshape(rows, MEM_HEAD_DIM) for hd in range(MEM_HEADS)]
        else:
            cols = [_dot(mem_probs[hd], mem_vals[hd]) for hd in range(MEM_HEADS)]
        s['memo'] = jnp.concatenate(cols, axis=1).astype(BF16)

    def merge(t):
        s = st[t]
        gates(t, len(s['gate_pieces']))
        gate_sigmoids(t)
        mixed = None
        for n, name in enumerate(('attn', 'sg', 'memo')):
            proj = _dot(s.pop(name), w_o_ref[n * ATTN_Q:(n + 1) * ATTN_Q, :])
            term = gate_ref[n, rows_of(t), :] * proj
            mixed = term if mixed is None else mixed + term
        x1_ref[rows_of(t), :] = s.pop('x') + _rms(mixed, g_post_ref[...])

    stages = {'head': head, 'inproj': in_proj, 'rope': rope_stage, 'scores': scores, 'gelu': gelu_ln,
              'softmax': softmax, 'values': values, 'outs': branch_outs, 'memv': mem_values, 'merge': merge,
              'sig': gate_sigmoids}
    for tok in tokens:
        name, t = tok[:-1], ord(tok[-1]) - ord('A')
        if name.startswith('gates'):
            gates(t, int(name[len('gates'):]))
        elif name[-1].isdigit():
            stages[name[:-1]](t, int(name[-1]))
        else:
            stages[name](t)


def _ffn_kernel(*refs, sample):
    if sample:
        (x_ref, g_pre_ref, w_up_ref, cw_ref, cb_ref, w_down_ref, g_post_ref, st_ref,
         y_ref, nc_ref, *up_s) = refs
    else:
        (x_ref, g_pre_ref, w_up_ref, cw_ref, cb_ref, w_down_ref, g_post_ref,
         y_ref, nc_ref, *up_s) = refs
    rows = x_ref.shape[0]
    step = pl.program_id(0)

    if not sample:
        @pl.when(step == 0)
        def _():
            for slab in up_s:
                slab[:SUBLANES, :] = jnp.zeros((SUBLANES, LANES), F32)

    x = x_ref[...]
    h = _rms(x, g_pre_ref[...]).astype(BF16)

    hist = CONV_WIDTH - 1
    seqs = rows // SUBLANES

    def up_cols(col0, width):
        up = jnp.dot(h, w_up_ref[:, col0:col0 + width], preferred_element_type=F32)
        for t in range(width // LANES):
            s = col0 // LANES + t
            blk = up[:, t * LANES:(t + 1) * LANES]
            if sample:
                up_s[s][:, SUBLANES:, :] = blk.reshape(seqs, SUBLANES, LANES)
                up_s[s][:, SUBLANES - hist:SUBLANES, :] = st_ref[:, :, s * LANES:(s + 1) * LANES]
            else:
                up_s[s][SUBLANES:, :] = blk

    def conv_slab(s):
        cols = slice(s * LANES, (s + 1) * LANES)
        if sample:
            taps = [up_s[s][:, SUBLANES - hist + j:2 * SUBLANES - hist + j, :] for j in range(CONV_WIDTH)]
            nc_ref[:, :, cols] = up_s[s][:, 2 * SUBLANES - hist:, :]
        else:
            taps = [up_s[s][SUBLANES - hist + j:SUBLANES - hist + j + rows, :] for j in range(CONV_WIDTH)]
            nc_ref[:, cols] = up_s[s][SUBLANES + rows - hist:, :]
            up_s[s][:SUBLANES, :] = up_s[s][rows:, :]
        c = cb_ref[:, cols]
        for j in range(CONV_WIDTH):
            c = c + taps[j] * cw_ref[j:j + 1, cols]
        return c.reshape(rows, LANES)

    for n0 in range(0, D_FF, FFN_UP_CHUNK):
        nw = min(FFN_UP_CHUNK, D_FF - n0)
        up_cols(n0, nw)
        up_cols(D_FF + n0, nw)
    chunks = [(c0, min(FFN_CHUNK, D_FF - c0)) for c0 in range(0, D_FF, FFN_CHUNK)]
    f = None
    for k, (c0, width) in enumerate(chunks):
        acts = []
        for t in range(width // LANES):
            gate = conv_slab(c0 // LANES + t)
            val = conv_slab((D_FF + c0) // LANES + t)
            acts.append((_gelu_tanh(gate) * val).astype(BF16))
        act = jnp.concatenate(acts, axis=1)
        part = jnp.dot(act, w_down_ref[c0:c0 + width, :], preferred_element_type=F32)
        f = part if f is None else f + part
    y_ref[...] = x + _rms(f, g_post_ref[...])


def _memkv_kernel(mem_ref, g_ref, w_ref, k_ref, v_ref, kb_ref, vb_ref):
    h = _rms(mem_ref[...], g_ref[...]).astype(BF16)
    kv = _dot(h, w_ref[...].astype(BF16))
    kb_ref[...] = kv[:, :MEM_Q].astype(BF16)
    vb_ref[...] = kv[:, MEM_Q:].astype(BF16)
    for hd in range(MEM_HEADS):
        k_ref[pl.ds(hd, N_MEM, stride=MEM_HEADS), :] = kv[:, hd * MEM_HEAD_DIM:(hd + 1) * MEM_HEAD_DIM]
        v_ref[pl.ds(hd, N_MEM, stride=MEM_HEADS), :] = kv[:, MEM_Q + hd * MEM_HEAD_DIM:
                                                          MEM_Q + (hd + 1) * MEM_HEAD_DIM]


def _resident():
    return pl.BlockSpec(memory_space=pltpu.VMEM)


def _row_spec(rows, width):
    return pl.BlockSpec((rows, width), lambda i: (i, 0))


def _const_spec(shape):
    nd = len(shape)
    return pl.BlockSpec(shape, lambda i: (0,) * nd)


def _compiler_params():
    return pltpu.CompilerParams(dimension_semantics=("arbitrary",), vmem_limit_bytes=VMEM_LIMIT_BYTES)


def _rope_tables(pos):
    half = HEAD_DIM // 2
    inv_freq = np.float64(ROPE_THETA) ** (-np.arange(half, dtype=np.float64) / half)
    ang = pos.astype(np.float64)[:, None] * inv_freq[None, :]
    cos = np.concatenate([np.cos(ang)] * (LANES // half), axis=1).astype(np.float32)
    sin = np.concatenate([np.sin(ang)] * (LANES // half), axis=1).astype(np.float32)
    sign = np.concatenate([-np.ones((1, half), np.float32), np.ones((1, half), np.float32)]
                          * (LANES // HEAD_DIM), axis=1)
    return cos, sin, cos * sign, sin * sign


def _mix_call(x2d, start, seq_len, lp, mem_k, mem_v, caches):
    n_rows = x2d.shape[0]
    sample = caches is not None
    rows = SAMPLE_SEQS * seq_len if sample else PROMPT_MIX_ROWS
    schedule = MIX_SCHEDULE_ONE if sample else MIX_SCHEDULE_TWO
    steps = n_rows // rows
    if sample:
        pos = start + (np.arange(rows, dtype=np.int32) % seq_len)
        cos, _, _, sin_signed = _rope_tables(pos)
        rope_args = [cos, sin_signed]
        rope_specs = [_const_spec((rows, LANES))] * 2
        chunk = rows
        wmix = jnp.tile(lp['sg_w'][:, :seq_len, :seq_len], (1, rows // seq_len, rows // seq_len))
        sgb = jnp.repeat(jnp.tile(lp['sg_b'][:, :seq_len], (1, rows // seq_len)).T, SG_GROUP_DIM, axis=1)
    else:
        in_step = _rope_tables(np.arange(rows, dtype=np.int32))
        cos_b, sin_b, _, _ = _rope_tables(start + rows * np.arange(steps, dtype=np.int32))
        rope_args = list(in_step) + [cos_b, sin_b]
        rope_specs = [_const_spec((rows, LANES))] * 4 + [_const_spec((steps, LANES))] * 2
        chunk = CHUNK
        wmix = lp['sg_w'][:, :chunk, :chunk]
        sgb = jnp.repeat(lp['sg_b'][:, :chunk].T, SG_GROUP_DIM, axis=1)

    in_specs = [
        pl.BlockSpec(memory_space=pltpu.SMEM),
        _row_spec(rows, D_MODEL),
        *rope_specs,
        _const_spec((1, D_MODEL)),
        _resident(),
        _const_spec((1, SG_WIDTH)), _const_spec((1, SG_WIDTH)),
        _const_spec((SG_GROUPS, chunk, chunk)),
        _const_spec((chunk, SG_WIDTH)),
    ]
    args = [lp['sinks'], x2d, *rope_args, lp['pre_mix_g'].reshape(1, D_MODEL), lp['w_in'],
            lp['sg_ln_g'].reshape(1, SG_WIDTH), lp['sg_ln_b'].reshape(1, SG_WIDTH), wmix, sgb]
    if sample:
        seqs = SAMPLE_SEQS
        in_specs += [pl.BlockSpec((seqs, N_MEM * MEM_HEADS, MEM_HEAD_DIM), lambda i: (i, 0, 0))] * 2
    else:
        in_specs += [_const_spec((N_MEM, MEM_Q))] * 2
    args += [mem_k, mem_v]
    in_specs += [_resident(), _const_spec((1, D_MODEL))]
    args += [lp['w_o'], lp['post_mix_g'].reshape(1, D_MODEL)]

    out_shape = [jax.ShapeDtypeStruct((n_rows, D_MODEL), F32)]
    out_specs = [_row_spec(rows, D_MODEL)]
    scratch = []
    if sample:
        win_k, win_v = caches
        n_seq, past = win_k.shape[0], win_k.shape[1]
        cache_spec = pl.BlockSpec((SAMPLE_SEQS, past, ATTN_KV), lambda i: (i, 0, 0))
        in_specs += [cache_spec, cache_spec]
        args += [win_k, win_v]
        out_shape += [jax.ShapeDtypeStruct((n_seq, past, ATTN_KV), F32)] * 2
        out_specs += [cache_spec, cache_spec]
        out_shape += [jax.ShapeDtypeStruct((n_rows, SG_WIDTH), F32)]
        out_specs += [_row_spec(rows, SG_WIDTH)]
    else:
        out_shape += [jax.ShapeDtypeStruct((WINDOW, ATTN_KV), F32)] * 2
        out_specs += [_const_spec((WINDOW, ATTN_KV))] * 2
        scratch = [pltpu.VMEM((WINDOW, ATTN_KV), F32), pltpu.VMEM((WINDOW, ATTN_KV), F32)]
    scratch += [pltpu.VMEM((N_BRANCHES, rows, D_MODEL), F32)]

    return pl.pallas_call(
        functools.partial(_mix_kernel_tiles, sample=sample, schedule=schedule),
        grid=(steps,),
        in_specs=in_specs,
        out_specs=out_specs,
        out_shape=out_shape,
        scratch_shapes=scratch,
        compiler_params=_compiler_params(),
        name="mix_sample" if sample else "mix_prompt",
    )(*args)


def _ffn_call(x2d, seq_len, lp, conv_state):
    n_rows = x2d.shape[0]
    sample = conv_state is not None
    rows = SAMPLE_FFN_SEQS * seq_len if sample else PROMPT_ROWS
    steps = n_rows // rows
    in_specs = [
        _row_spec(rows, D_MODEL),
        _const_spec((1, D_MODEL)),
        _resident(),
        _const_spec((CONV_WIDTH, 2 * D_FF)),
        _const_spec((1, 2 * D_FF)),
        _resident(),
        _const_spec((1, D_MODEL)),
    ]
    args = [x2d, lp['pre_ffn_g'].reshape(1, D_MODEL), lp['w_up'], lp['conv_w'],
            lp['conv_b'].reshape(1, 2 * D_FF), lp['w_down'], lp['post_ffn_g'].reshape(1, D_MODEL)]
    out_shape = [jax.ShapeDtypeStruct((n_rows, D_MODEL), F32)]
    out_specs = [_row_spec(rows, D_MODEL)]
    n_slabs = 2 * D_FF // LANES
    if sample:
        n_seq = conv_state.shape[0]
        st_spec = pl.BlockSpec((SAMPLE_FFN_SEQS, CONV_WIDTH - 1, 2 * D_FF), lambda i: (i, 0, 0))
        in_specs += [st_spec]
        args += [conv_state]
        out_shape += [jax.ShapeDtypeStruct((n_seq, CONV_WIDTH - 1, 2 * D_FF), F32)]
        out_specs += [st_spec]
        scratch = [pltpu.VMEM((SAMPLE_FFN_SEQS, 2 * SUBLANES, LANES), F32)] * n_slabs
    else:
        out_shape += [jax.ShapeDtypeStruct((CONV_WIDTH - 1, 2 * D_FF), F32)]
        out_specs += [_const_spec((CONV_WIDTH - 1, 2 * D_FF))]
        scratch = [pltpu.VMEM((SUBLANES + rows, LANES), F32)] * n_slabs
    return pl.pallas_call(
        functools.partial(_ffn_kernel, sample=sample),
        grid=(steps,),
        in_specs=in_specs,
        out_specs=out_specs,
        out_shape=out_shape,
        scratch_shapes=scratch,
        compiler_params=_compiler_params(),
        name="ffn_sample" if sample else "ffn_prompt",
    )(*args)


def _memkv_call(mem2d, g, w_f32):
    return pl.pallas_call(
        _memkv_kernel,
        out_shape=[jax.ShapeDtypeStruct((mem2d.shape[0] * MEM_HEADS, MEM_HEAD_DIM), F32)] * 2
        + [jax.ShapeDtypeStruct((mem2d.shape[0], MEM_Q), BF16)] * 2,
        compiler_params=pltpu.CompilerParams(vmem_limit_bytes=VMEM_LIMIT_BYTES),
        name="mem_kv",
    )(mem2d, g.reshape(1, D_MODEL), w_f32)


def kernel(x_prompt, x_sample, cache_win_k, cache_win_v, cache_mem_k, cache_mem_v, state_conv, mem_prompt,
           pre_mix_g, w_in, attn_sinks, sg_ln_g, sg_ln_b, sg_w, sg_b, mem_norm_g, w_mem_kv, w_o,
           post_mix_g, pre_ffn_g, w_up, conv_w, conv_b, w_down, post_ffn_g):
    depth = w_in.shape[0]
    batch, seq, _ = x_prompt.shape
    dec_batch, dec_seq, _ = x_sample.shape
    past_len = PAST_LEN
    assert batch == 1 and depth == 1

    yp = x_prompt.reshape(batch * seq, D_MODEL)
    ys = x_sample.reshape(dec_batch * dec_seq, D_MODEL)
    outs = {name: [] for name in ('wk_p', 'wv_p', 'mk_p', 'mv_p', 'cv_p', 'wk_s', 'wv_s', 'sgv_s', 'cv_s')}
    for l in range(depth):
        lp = {
            'sinks': attn_sinks[l], 'pre_mix_g': pre_mix_g[l], 'w_in': w_in[l].astype(BF16),
            'sg_ln_g': sg_ln_g[l], 'sg_ln_b': sg_ln_b[l], 'sg_w': sg_w[l], 'sg_b': sg_b[l],
            'w_o': w_o[l].reshape(N_BRANCHES * ATTN_Q, D_MODEL).astype(BF16), 'post_mix_g': post_mix_g[l],
            'pre_ffn_g': pre_ffn_g[l], 'w_up': w_up[l].astype(BF16), 'conv_w': conv_w[l], 'conv_b': conv_b[l],
            'w_down': w_down[l].astype(BF16), 'post_ffn_g': post_ffn_g[l],
        }
        mem_k, mem_v, mem_k_b, mem_v_b = _memkv_call(mem_prompt.reshape(batch * N_MEM, D_MODEL), mem_norm_g[l],
                                                     w_mem_kv[l])
        outs['mk_p'].append(mem_k.reshape(batch, N_MEM, MEM_HEADS, MEM_HEAD_DIM))
        outs['mv_p'].append(mem_v.reshape(batch, N_MEM, MEM_HEADS, MEM_HEAD_DIM))

        x1, wk, wv = _mix_call(yp, 0, seq, lp, mem_k_b, mem_v_b, None)
        yp, nc = _ffn_call(x1, seq, lp, None)
        outs['wk_p'].append(wk.reshape(batch, N_KV_HEADS, HEAD_DIM, WINDOW).transpose(0, 3, 1, 2))
        outs['wv_p'].append(wv.reshape(batch, N_KV_HEADS, HEAD_DIM, WINDOW).transpose(0, 3, 1, 2))
        outs['cv_p'].append(nc.reshape(batch, CONV_WIDTH - 1, 2 * D_FF))

        past = cache_win_k.shape[2]
        x1s, wks, wvs, sgv = _mix_call(
            ys, past_len, dec_seq, lp,
            cache_mem_k[l].reshape(dec_batch, N_MEM * MEM_HEADS, MEM_HEAD_DIM),
            cache_mem_v[l].reshape(dec_batch, N_MEM * MEM_HEADS, MEM_HEAD_DIM),
            (cache_win_k[l].reshape(dec_batch, past, ATTN_KV), cache_win_v[l].reshape(dec_batch, past, ATTN_KV)))
        ys, ncs = _ffn_call(x1s, dec_seq, lp, state_conv[l])
        outs['wk_s'].append(wks.reshape(dec_batch, past, N_KV_HEADS, HEAD_DIM))
        outs['wv_s'].append(wvs.reshape(dec_batch, past, N_KV_HEADS, HEAD_DIM))
        outs['sgv_s'].append(sgv.reshape(dec_batch, dec_seq, SG_WIDTH))
        outs['cv_s'].append(ncs)

    return (yp.reshape(batch, seq, D_MODEL), ys.reshape(dec_batch, dec_seq, D_MODEL),
            jnp.stack(outs['wk_p']), jnp.stack(outs['wv_p']), jnp.stack(outs['mk_p']), jnp.stack(outs['mv_p']),
            jnp.stack(outs['cv_p']),
            jnp.stack(outs['wk_s']), jnp.stack(outs['wv_s']), jnp.stack(outs['sgv_s']), jnp.stack(outs['cv_s']))
```

```python
import functools
import math

import jax
import jax.numpy as jnp
import numpy as np
from jax import lax
from jax.experimental import pallas as pl
from jax.experimental.pallas import tpu as pltpu

F32 = jnp.float32
BF16 = jnp.bfloat16

D_MODEL = 1024
N_Q_HEADS = 8
N_KV_HEADS = 2
GROUP = N_Q_HEADS // N_KV_HEADS
HEAD_DIM = 64
ATTN_Q = N_Q_HEADS * HEAD_DIM
ATTN_KV = N_KV_HEADS * HEAD_DIM
WINDOW = 128
ROPE_THETA = 10000.0
CHUNK = 128
SG_GROUPS = 4
SG_GROUP_DIM = 128
SG_WIDTH = SG_GROUPS * SG_GROUP_DIM
N_MEM = 256
MEM_HEADS = 4
MEM_HEAD_DIM = 128
MEM_Q = MEM_HEADS * MEM_HEAD_DIM
N_BRANCHES = 3
D_FF = 2816
CONV_WIDTH = 3
EPS = 1e-6
NEG_INF = -1e30
LOG2E = math.log2(math.e)
PAST_LEN = 16384

OFF_Q = 0
OFF_K = ATTN_Q
OFF_V = OFF_K + ATTN_KV
OFF_SGU = OFF_V + ATTN_KV
OFF_SGV = OFF_SGU + SG_WIDTH
OFF_MQ = OFF_SGV + SG_WIDTH
OFF_GATE = OFF_MQ + MEM_Q
IN_WIDTH = OFF_GATE + N_BRANCHES * D_MODEL

LANES = 128
SUBLANES = 8
VMEM_LIMIT_BYTES = 56 * 1024 * 1024

PROMPT_ROWS = 512
PROMPT_MIX_ROWS = 1024
PROMPT_MIX_SUB_ROWS = 256
MIX_SCHEDULE_ONE = ("headA inprojA ropeA scoresA gates2A geluA softmaxA valuesA gates2A outsA memvA mergeA")


def _rolling_mix_schedule(n_sub):
    names = [chr(ord('A') + i) for i in range(n_sub)]
    toks = [f"head{names[0]}", f"inproj{names[0]}", f"memsc{names[0]}", f"rope{names[0]}", f"scores{names[0]}"]
    for i, x in enumerate(names):
        w = names[i - 1] if i > 0 else None
        y = names[i + 1] if i + 1 < n_sub else None
        if w is None and y:
            toks += [f"head{y}", f"inproj{y}"]
        toks += [f"gelu{x}", f"sgmix{x}"] + ([f"gates3{w}"] if w else [])
        toks += [f"memsm{x}", f"memv{x}", f"softmax{x}"]
        if w:
            toks += [f"sig{w}"] + ([f"head{y}", f"inproj{y}"] if y else []) + [f"merge{w}"]
        toks += [f"pv{x}", f"gates3{x}" if y else f"gates6{x}"]
        if y:
            toks += [f"rope{y}", f"memsc{y}", f"scores{y}"]
        toks += [f"sgout{x}", f"attnout{x}", f"sig{x}"]
    return " ".join(toks + [f"merge{names[-1]}"])
SAMPLE_SEQS = 8
SAMPLE_FFN_SEQS = 32
FFN_CHUNK = 256
FFN_UP_CHUNK = 256
FFN_SCHEDULE = "head up22 vd11 tail"
GATE_CHUNK = 512


def _rms(x, g):
    return x * lax.rsqrt(jnp.mean(x * x, axis=-1, keepdims=True) + EPS) * g


def _dot(a, b):
    return jnp.dot(a, b, preferred_element_type=F32)


def _gelu_erf(x):
    return 0.5 * x * (1.0 + lax.erf(x * np.float32(math.sqrt(0.5))))


def _gelu_tanh(x):
    c = np.float32(math.sqrt(2.0 / math.pi))
    return x * (0.5 * (1.0 + jnp.tanh(c * (x + 0.044715 * (x * x * x)))))


def _lane_iota(shape):
    return lax.broadcasted_iota(jnp.int32, shape, len(shape) - 1)


def _rope(x, cos, sin_signed):
    w = x.shape[1]
    reps = w // LANES
    c = jnp.concatenate([cos] * reps, axis=1) if reps > 1 else cos
    s = jnp.concatenate([sin_signed] * reps, axis=1) if reps > 1 else sin_signed
    first_half = (_lane_iota(x.shape) % HEAD_DIM) < (HEAD_DIM // 2)
    swapped = jnp.where(first_half, pltpu.roll(x, w - HEAD_DIM // 2, 1), pltpu.roll(x, HEAD_DIM // 2, 1))
    return x * c + swapped * s


def _dup_head(kv, g):
    rolled = pltpu.roll(kv, HEAD_DIM, 1)
    low = _lane_iota(kv.shape) < HEAD_DIM
    return jnp.where(low, kv, rolled) if g == 0 else jnp.where(low, rolled, kv)


def _stack_group_queries(q, g):
    low = _lane_iota(q.shape[:2] + (LANES,)) < HEAD_DIM
    parts = []
    for i in range(GROUP):
        h = GROUP * g + i
        pair = q[:, :, (h // 2) * LANES:(h // 2 + 1) * LANES]
        parts.append(jnp.where(low if h % 2 == 0 else ~low, pair, 0.0))
    return jnp.concatenate(parts, axis=1)


def _unstack_group_outputs(o, rows):
    low = _lane_iota((o.shape[0], rows, LANES)) < HEAD_DIM
    pairs = []
    for p in range(GROUP // 2):
        even = o[:, (2 * p) * rows:(2 * p + 1) * rows]
        odd = o[:, (2 * p + 1) * rows:(2 * p + 2) * rows]
        pairs.append(jnp.where(low, even, odd))
    return jnp.concatenate(pairs, axis=2)


def _sink_softmax(s, scale, mask, sink):
    t = jnp.where(mask, s * np.float32(scale * LOG2E), NEG_INF)
    sink2 = sink * np.float32(LOG2E)
    m = jnp.maximum(jnp.max(t, axis=-1, keepdims=True), sink2)
    e = jnp.exp2(t - m)
    denom = jnp.sum(e, axis=-1, keepdims=True) + jnp.exp2(sink2 - m)
    return (e * (1.0 / denom)).astype(BF16)


def _softmax(s, scale):
    t = s * np.float32(scale * LOG2E)
    m = jnp.max(t, axis=-1, keepdims=True)
    e = jnp.exp2(t - m)
    return (e * (1.0 / jnp.sum(e, axis=-1, keepdims=True))).astype(BF16)


def _sigmoid(x):
    return 1.0 / (1.0 + jnp.exp2(x * np.float32(-LOG2E)))


def _sink_column(sinks_ref, g, rows):
    r = lax.broadcasted_iota(jnp.int32, (1, GROUP * rows, 1), 1)
    col = jnp.full((1, GROUP * rows, 1), sinks_ref[GROUP * g + GROUP - 1], F32)
    for i in range(GROUP - 2, -1, -1):
        col = jnp.where(r < (i + 1) * rows, sinks_ref[GROUP * g + i], col)
    return col


def _sub_tiles(schedule):
    return 1 + max(ord(tok[-1]) - ord('A') for tok in schedule.split())


def _mix_program(refs, sample, n_sub):
    if sample:
        (sinks_ref, x_ref, cos_ref, sin_ref, g_pre_ref, w_in_ref, ln_g_ref, ln_b_ref, wmix_ref, sgb_ref,
         mk_ref, mv_ref, w_o_ref, g_post_ref, ck_ref, cv_ref,
         x1_ref, wk_ref, wv_ref, sgv_ref, gate_ref) = refs
    else:
        (sinks_ref, x_ref, cos_ref, sin_ref, cos_s_ref, sin_s_ref, cos_b_ref, sin_b_ref,
         g_pre_ref, w_in_ref, ln_g_ref, ln_b_ref, wmix_ref, sgb_ref,
         mk_ref, mv_ref, w_o_ref, g_post_ref,
         x1_ref, wk_ref, wv_ref, kcar_ref, vcar_ref, gate_ref) = refs
    rows = x_ref.shape[0] // n_sub
    seqs = rows // SUBLANES
    step = pl.program_id(0)
    assert not (sample and n_sub > 1)

    def init():
        if not sample:
            @pl.when(step == 0)
            def _():
                kcar_ref[...] = jnp.zeros_like(kcar_ref)
                vcar_ref[...] = jnp.zeros_like(vcar_ref)

    st = [dict() for _ in range(n_sub)]
    chunk = wmix_ref.shape[1]
    mem_scale = MEM_HEAD_DIM ** -0.5

    def rows_of(t):
        return slice(t * rows, (t + 1) * rows)

    def head(t):
        s = st[t]
        s['x'] = x_ref[rows_of(t), :]
        s['h'] = _rms(s['x'], g_pre_ref[...]).astype(BF16)
        s['gate_pieces'] = [(n, c) for n in range(N_BRANCHES) for c in range(0, D_MODEL, GATE_CHUNK)]
        s['logit_pieces'] = []

    def gates(t, count):
        s = st[t]
        for _ in range(min(count, len(s['gate_pieces']))):
            n, c = s['gate_pieces'].pop(0)
            col = OFF_GATE + n * D_MODEL + c
            gate_ref[n, rows_of(t), c:c + GATE_CHUNK] = _dot(s['h'], w_in_ref[:, col:col + GATE_CHUNK])
            s['logit_pieces'].append((n, c))

    def gate_sigmoids(t):
        s = st[t]
        while s['logit_pieces']:
            n, c = s['logit_pieces'].pop(0)
            piece = (n, rows_of(t), slice(c, c + GATE_CHUNK))
            gate_ref[piece] = _sigmoid(gate_ref[piece])

    def in_proj(t):
        s = st[t]
        s['zqkv'] = _dot(s['h'], w_in_ref[:, OFF_Q:OFF_SGU])
        s['zsg'] = _dot(s['h'], w_in_ref[:, OFF_SGU:OFF_MQ])
        s['mq'] = _dot(s['h'], w_in_ref[:, OFF_MQ:OFF_GATE])

    def rope_stage(t):
        s = st[t]
        zqkv = s.pop('zqkv')
        if sample:
            cos, sin_signed = cos_ref[...], sin_ref[...]
        else:
            cb = cos_b_ref[pl.ds(step, 1), :]
            sb = sin_b_ref[pl.ds(step, 1), :]
            cos = cb * cos_ref[rows_of(t), :] - sb * sin_ref[rows_of(t), :]
            sin_signed = sb * cos_s_ref[rows_of(t), :] + cb * sin_s_ref[rows_of(t), :]
        qk = _rope(zqkv[:, :OFF_V], cos, sin_signed)
        q = qk[:, :ATTN_Q]
        k = qk[:, OFF_K:OFF_V]
        v = zqkv[:, OFF_V:OFF_SGU]
        if sample:
            past = ck_ref.shape[1]
            k3 = k.reshape(seqs, SUBLANES, ATTN_KV)
            v3 = v.reshape(seqs, SUBLANES, ATTN_KV)
            ck = ck_ref[...]
            cv = cv_ref[...]
            wk_ref[:, :past - SUBLANES, :] = ck[:, SUBLANES:, :]
            wk_ref[:, past - SUBLANES:, :] = k3
            wv_ref[:, :past - SUBLANES, :] = cv[:, SUBLANES:, :]
            wv_ref[:, past - SUBLANES:, :] = v3
            n_keys = past + 2 * SUBLANES
            pad = jnp.zeros((seqs, SUBLANES, ATTN_KV), F32)
            kk = jnp.concatenate([ck, k3, pad], axis=1).reshape(seqs * n_keys, ATTN_KV)
            vv = jnp.concatenate([cv, v3, pad], axis=1).reshape(seqs * n_keys, ATTN_KV)
            q3 = q.reshape(seqs, SUBLANES, ATTN_Q)
            s['q_rows'] = SUBLANES
            qi = lax.broadcasted_iota(jnp.int32, (1, GROUP * SUBLANES, n_keys), 1) % SUBLANES
            kj = lax.broadcasted_iota(jnp.int32, (1, GROUP * SUBLANES, n_keys), 2)
            s['mask'] = (kj > qi + (past - WINDOW)) & (kj <= qi + past)
            s['kds'] = [_dup_head(kk, g).astype(BF16).reshape(seqs, n_keys, LANES) for g in range(N_KV_HEADS)]
            s['vds'] = [_dup_head(vv, g).astype(BF16).reshape(seqs, n_keys, LANES) for g in range(N_KV_HEADS)]
        else:
            nb = rows // WINDOW
            k_prev = kcar_ref[...] if t == 0 else st[t - 1]['k_last']
            v_prev = vcar_ref[...] if t == 0 else st[t - 1]['v_last']
            s['k_last'] = k[rows - WINDOW:]
            s['v_last'] = v[rows - WINDOW:]
            if t == n_sub - 1:
                wk_ref[...] = s['k_last'].T
                wv_ref[...] = s['v_last'].T
                kcar_ref[...] = s['k_last']
                vcar_ref[...] = s['v_last']
            kk = jnp.concatenate([k_prev, k], axis=0)
            vv = jnp.concatenate([v_prev, v], axis=0)
            q3 = q.reshape(nb, WINDOW, ATTN_Q)
            s['q_rows'] = WINDOW
            shape = (nb, GROUP * WINDOW, 2 * WINDOW)
            qi = lax.broadcasted_iota(jnp.int32, shape, 1) % WINDOW
            kj = lax.broadcasted_iota(jnp.int32, shape, 2)
            s['mask'] = (kj > qi) & (kj <= qi + WINDOW)
            if t == 0:
                blk = lax.broadcasted_iota(jnp.int32, shape, 0)
                first_key = jnp.where((blk == 0) & (step == 0), WINDOW, 0)
                s['mask'] = s['mask'] & (kj >= first_key)
            s['kds'], s['vds'] = [], []
            for g in range(N_KV_HEADS):
                kd = _dup_head(kk, g).astype(BF16)
                vd = _dup_head(vv, g).astype(BF16)
                s['kds'].append(jnp.stack([kd[n * WINDOW:(n + 2) * WINDOW] for n in range(nb)], axis=0))
                s['vds'].append(jnp.stack([vd[n * WINDOW:(n + 2) * WINDOW] for n in range(nb)], axis=0))
        s['qss'] = [_stack_group_queries(q3, g).astype(BF16) for g in range(N_KV_HEADS)]
        s['sinks'] = [_sink_column(sinks_ref, g, s['q_rows']) for g in range(N_KV_HEADS)]

    def scores(t):
        s = st[t]
        qss, kds = s.pop('qss'), s.pop('kds')
        s['scores'] = [jnp.einsum('bqd,bkd->bqk', q, k, preferred_element_type=F32) for q, k in zip(qss, kds)]

    def gelu_ln(t, part=None):
        s = st[t]
        if part in (None, 0):
            s['u'] = _gelu_erf(s['zsg'][:, :SG_WIDTH])
        if part in (None, 1):
            gv = _gelu_erf(s['zsg'][:, SG_WIDTH:])
            gc = gv - jnp.mean(gv, axis=-1, keepdims=True)
            var = jnp.mean(gc * gc, axis=-1, keepdims=True)
            vn = gc * lax.rsqrt(var + EPS) * ln_g_ref[...] + ln_b_ref[...]
            if sample:
                sgv_ref[...] = vn
            s['vn_b'] = vn.astype(BF16)

    def softmax(t, g=None):
        s = st[t]
        for gi in range(N_KV_HEADS) if g is None else (g,):
            s.setdefault('probs', {})[gi] = _sink_softmax(s['scores'][gi], HEAD_DIM ** -0.5, s['mask'],
                                                          s['sinks'][gi])

    def values(t):
        attn_values(t)
        mem_scores(t)
        spatial_mix(t)

    def attn_values(t):
        s = st[t]
        probs, vds = s.pop('probs'), s.pop('vds')
        s['attn_outs'] = [jnp.einsum('bqk,bkd->bqd', probs[g], vds[g], preferred_element_type=F32)
                          for g in range(len(vds))]

    def mem_scores(t):
        s = st[t]
        mq = s.pop('mq')
        s['mem_scores'], s['mem_vals'] = [], []
        if sample:
            mq3 = mq.reshape(seqs, SUBLANES, MEM_Q)
            lane_head = _lane_iota(mq3.shape) // MEM_HEAD_DIM
            q_all = jnp.concatenate([jnp.where(lane_head == hd, mq3, 0.0) for hd in range(MEM_HEADS)],
                                    axis=1).astype(BF16)
            k_all = jnp.concatenate([mk_ref[:, pl.ds(hd, N_MEM, stride=MEM_HEADS), :] for hd in range(MEM_HEADS)],
                                    axis=2).astype(BF16)
            v_all = jnp.concatenate([mv_ref[:, pl.ds(hd, N_MEM, stride=MEM_HEADS), :] for hd in range(MEM_HEADS)],
                                    axis=2).astype(BF16)
            s['mem_scores'].append(jnp.einsum('bqd,bkd->bqk', q_all, k_all, preferred_element_type=F32))
            s['mem_vals'].append(v_all)
        else:
            for hd in range(MEM_HEADS):
                sl = slice(hd * MEM_HEAD_DIM, (hd + 1) * MEM_HEAD_DIM)
                s['mem_vals'].append(mv_ref[:, sl])
                s['mem_scores'].append(lax.dot_general(mq[:, sl].astype(BF16), mk_ref[:, sl],
                                                       (((1,), (1,)), ((), ())), preferred_element_type=F32))

    def spatial_mix(t):
        s = st[t]
        vn_b = s.pop('vn_b')
        t_idx = lax.broadcasted_iota(jnp.int32, (chunk, chunk), 0)
        s_idx = lax.broadcasted_iota(jnp.int32, (chunk, chunk), 1)
        keep = t_idx >= s_idx
        if sample:
            keep = keep & ((t_idx // SUBLANES) == (s_idx // SUBLANES))
        s['sg_cols'] = []
        for g in range(SG_GROUPS):
            wm = jnp.where(keep, wmix_ref[g], 0.0).astype(BF16)
            blocks = [_dot(wm, vn_b[c * chunk:(c + 1) * chunk, g * SG_GROUP_DIM:(g + 1) * SG_GROUP_DIM])
                      for c in range(rows // chunk)]
            s['sg_cols'].append(jnp.concatenate(blocks, axis=0) if len(blocks) > 1 else blocks[0])

    def branch_outs(t):
        mem_softmax(t)
        attn_out(t)
        spatial_out(t)

    def mem_softmax(t):
        s = st[t]
        s['mem_probs'] = [_softmax(sc, mem_scale) for sc in s.pop('mem_scores')]

    def attn_out(t):
        s = st[t]
        s['attn'] = jnp.concatenate([_unstack_group_outputs(o, s['q_rows']) for o in s.pop('attn_outs')],
                                    axis=2).reshape(rows, ATTN_Q).astype(BF16)

    def spatial_out(t):
        s = st[t]
        mixed_sg = jnp.concatenate(s.pop('sg_cols'), axis=1)
        bias = sgb_ref[...]
        if rows // chunk > 1:
            bias = jnp.concatenate([bias] * (rows // chunk), axis=0)
        s['sg'] = (s.pop('u') * (mixed_sg + bias)).astype(BF16)

    def mem_values(t):
        s = st[t]
        mem_probs, mem_vals = s.pop('mem_probs'), s.pop('mem_vals')
        if sample:
            o = jnp.einsum('bqk,bkd->bqd', mem_probs[0], mem_vals[0], preferred_element_type=F32)
            cols = [o[:, hd * SUBLANES:(hd + 1) * SUBLANES, hd * MEM_HEAD_DIM:(hd + 1) * MEM_HEAD_DIM]
                    .reshape(rows, MEM_HEAD_DIM) for hd in range(MEM_HEADS)]
        else:
            cols = [_dot(mem_probs[hd], mem_vals[hd]) for hd in range(MEM_HEADS)]
        s['memo'] = jnp.concatenate(cols, axis=1).astype(BF16)

    def merge(t):
        s = st[t]
        gates(t, len(s['gate_pieces']))
        gate_sigmoids(t)
        mixed = None
        for n, name in enumerate(('attn', 'sg', 'memo')):
            proj = _dot(s.pop(name), w_o_ref[n * ATTN_Q:(n + 1) * ATTN_Q, :])
            term = gate_ref[n, rows_of(t), :] * proj
            mixed = term if mixed is None else mixed + term
        x1_ref[rows_of(t), :] = s.pop('x') + _rms(mixed, g_post_ref[...])

    stages = {'head': head, 'inproj': in_proj, 'rope': rope_stage, 'scores': scores, 'gelu': gelu_ln,
              'softmax': softmax, 'values': values, 'outs': branch_outs, 'memv': mem_values, 'merge': merge,
              'sig': gate_sigmoids, 'pv': attn_values, 'memsc': mem_scores, 'sgmix': spatial_mix,
              'memsm': mem_softmax, 'attnout': attn_out, 'sgout': spatial_out}
    def run(tok):
        name, t = tok[:-1], ord(tok[-1]) - ord('A')
        if name.startswith('gates'):
            gates(t, int(name[len('gates'):]))
        elif name[-1].isdigit():
            stages[name[:-1]](t, int(name[-1]))
        else:
            stages[name](t)

    return init, run


def _mix_kernel_tiles(*refs, sample, schedule):
    init, run = _mix_program(refs, sample, _sub_tiles(schedule))
    init()
    for tok in schedule.split():
        run(tok)


def _ffn_program(refs, sample):
    if sample:
        (x_ref, g_pre_ref, w_up_ref, cw_ref, cb_ref, w_down_ref, g_post_ref, st_ref,
         y_ref, nc_ref, *up_s) = refs
    else:
        (x_ref, g_pre_ref, w_up_ref, cw_ref, cb_ref, w_down_ref, g_post_ref,
         y_ref, nc_ref, *up_s) = refs
    rows = x_ref.shape[0]
    step = pl.program_id(0)
    st = {}

    def init():
        if not sample:
            @pl.when(step == 0)
            def _():
                for slab in up_s:
                    slab[:SUBLANES, :] = jnp.zeros((SUBLANES, LANES), F32)

    def head():
        st['x'] = x_ref[...]
        st['h'] = _rms(st['x'], g_pre_ref[...]).astype(BF16)
        st['up_pieces'] = [(half * D_FF + n0, min(FFN_UP_CHUNK, D_FF - n0))
                           for n0 in range(0, D_FF, FFN_UP_CHUNK) for half in range(2)]
        st['chunks'] = [(c0, min(FFN_CHUNK, D_FF - c0)) for c0 in range(0, D_FF, FFN_CHUNK)]
        st['f'] = None

    hist = CONV_WIDTH - 1
    seqs = rows // SUBLANES

    def up_cols(col0, width):
        up = jnp.dot(st['h'], w_up_ref[:, col0:col0 + width], preferred_element_type=F32)
        for t in range(width // LANES):
            s = col0 // LANES + t
            blk = up[:, t * LANES:(t + 1) * LANES]
            if sample:
                up_s[s][:, SUBLANES:, :] = blk.reshape(seqs, SUBLANES, LANES)
                up_s[s][:, SUBLANES - hist:SUBLANES, :] = st_ref[:, :, s * LANES:(s + 1) * LANES]
            else:
                up_s[s][SUBLANES:, :] = blk

    def conv_slab(s):
        cols = slice(s * LANES, (s + 1) * LANES)
        if sample:
            taps = [up_s[s][:, SUBLANES - hist + j:2 * SUBLANES - hist + j, :] for j in range(CONV_WIDTH)]
            nc_ref[:, :, cols] = up_s[s][:, 2 * SUBLANES - hist:, :]
        else:
            taps = [up_s[s][SUBLANES - hist + j:SUBLANES - hist + j + rows, :] for j in range(CONV_WIDTH)]
            nc_ref[:, cols] = up_s[s][SUBLANES + rows - hist:, :]
            up_s[s][:SUBLANES, :] = up_s[s][rows:, :]
        c = cb_ref[:, cols]
        for j in range(CONV_WIDTH):
            c = c + taps[j] * cw_ref[j:j + 1, cols]
        return c.reshape(rows, LANES)

    def up(count):
        for _ in range(min(count, len(st['up_pieces']))):
            up_cols(*st['up_pieces'].pop(0))

    def conv_down(count):
        for _ in range(min(count, len(st['chunks']))):
            c0, width = st['chunks'].pop(0)
            acts = []
            for t in range(width // LANES):
                gate = conv_slab(c0 // LANES + t)
                val = conv_slab((D_FF + c0) // LANES + t)
                acts.append((_gelu_tanh(gate) * val).astype(BF16))
            act = jnp.concatenate(acts, axis=1)
            part = jnp.dot(act, w_down_ref[c0:c0 + width, :], preferred_element_type=F32)
            st['f'] = part if st['f'] is None else st['f'] + part

    def tail():
        up(len(st['up_pieces']))
        conv_down(len(st['chunks']))
        y_ref[...] = st.pop('x') + _rms(st.pop('f'), g_post_ref[...])

    def run(tok):
        if tok == 'head':
            head()
        elif tok == 'tail':
            tail()
        elif tok.startswith('up'):
            up(int(tok[2:]))
        else:
            assert tok.startswith('vd'), tok
            conv_down(int(tok[2:]))

    return init, run


def _ffn_kernel(*refs, sample):
    init, run = _ffn_program(refs, sample)
    init()
    for tok in FFN_SCHEDULE.split():
        run(tok)


def _memkv_kernel(mem_ref, g_ref, w_ref, k_ref, v_ref, kb_ref, vb_ref):
    h = _rms(mem_ref[...], g_ref[...]).astype(BF16)
    kv = _dot(h, w_ref[...].astype(BF16))
    kb_ref[...] = kv[:, :MEM_Q].astype(BF16)
    vb_ref[...] = kv[:, MEM_Q:].astype(BF16)
    for hd in range(MEM_HEADS):
        k_ref[pl.ds(hd, N_MEM, stride=MEM_HEADS), :] = kv[:, hd * MEM_HEAD_DIM:(hd + 1) * MEM_HEAD_DIM]
        v_ref[pl.ds(hd, N_MEM, stride=MEM_HEADS), :] = kv[:, MEM_Q + hd * MEM_HEAD_DIM:
                                                          MEM_Q + (hd + 1) * MEM_HEAD_DIM]


def _resident():
    return pl.BlockSpec(memory_space=pltpu.VMEM)


def _row_spec(rows, width):
    return pl.BlockSpec((rows, width), lambda i: (i, 0))


def _const_spec(shape):
    nd = len(shape)
    return pl.BlockSpec(shape, lambda i: (0,) * nd)


def _compiler_params():
    return pltpu.CompilerParams(dimension_semantics=("arbitrary",), vmem_limit_bytes=VMEM_LIMIT_BYTES)


def _rope_tables(pos):
    half = HEAD_DIM // 2
    inv_freq = np.float64(ROPE_THETA) ** (-np.arange(half, dtype=np.float64) / half)
    ang = pos.astype(np.float64)[:, None] * inv_freq[None, :]
    cos = np.concatenate([np.cos(ang)] * (LANES // half), axis=1).astype(np.float32)
    sin = np.concatenate([np.sin(ang)] * (LANES // half), axis=1).astype(np.float32)
    sign = np.concatenate([-np.ones((1, half), np.float32), np.ones((1, half), np.float32)]
                          * (LANES // HEAD_DIM), axis=1)
    return cos, sin, cos * sign, sin * sign


def _mix_call(x2d, start, seq_len, lp, mem_k, mem_v, caches):
    n_rows = x2d.shape[0]
    sample = caches is not None
    rows = SAMPLE_SEQS * seq_len if sample else PROMPT_MIX_ROWS
    schedule = MIX_SCHEDULE_ONE if sample else _rolling_mix_schedule(PROMPT_MIX_ROWS // PROMPT_MIX_SUB_ROWS)
    steps = n_rows // rows
    if sample:
        pos = start + (np.arange(rows, dtype=np.int32) % seq_len)
        cos, _, _, sin_signed = _rope_tables(pos)
        rope_args = [cos, sin_signed]
        rope_specs = [_const_spec((rows, LANES))] * 2
        chunk = rows
        wmix = jnp.tile(lp['sg_w'][:, :seq_len, :seq_len], (1, rows // seq_len, rows // seq_len))
        sgb = jnp.repeat(jnp.tile(lp['sg_b'][:, :seq_len], (1, rows // seq_len)).T, SG_GROUP_DIM, axis=1)
    else:
        in_step = _rope_tables(np.arange(rows, dtype=np.int32))
        cos_b, sin_b, _, _ = _rope_tables(start + rows * np.arange(steps, dtype=np.int32))
        rope_args = list(in_step) + [cos_b, sin_b]
        rope_specs = [_const_spec((rows, LANES))] * 4 + [_const_spec((steps, LANES))] * 2
        chunk = CHUNK
        wmix = lp['sg_w'][:, :chunk, :chunk]
        sgb = jnp.repeat(lp['sg_b'][:, :chunk].T, SG_GROUP_DIM, axis=1)

    in_specs = [
        pl.BlockSpec(memory_space=pltpu.SMEM),
        _row_spec(rows, D_MODEL),
        *rope_specs,
        _const_spec((1, D_MODEL)),
        _resident(),
        _const_spec((1, SG_WIDTH)), _const_spec((1, SG_WIDTH)),
        _const_spec((SG_GROUPS, chunk, chunk)),
        _const_spec((chunk, SG_WIDTH)),
    ]
    args = [lp['sinks'], x2d, *rope_args, lp['pre_mix_g'].reshape(1, D_MODEL), lp['w_in'],
            lp['sg_ln_g'].reshape(1, SG_WIDTH), lp['sg_ln_b'].reshape(1, SG_WIDTH), wmix, sgb]
    if sample:
        seqs = SAMPLE_SEQS
        in_specs += [pl.BlockSpec((seqs, N_MEM * MEM_HEADS, MEM_HEAD_DIM), lambda i: (i, 0, 0))] * 2
    else:
        in_specs += [_const_spec((N_MEM, MEM_Q))] * 2
    args += [mem_k, mem_v]
    in_specs += [_resident(), _const_spec((1, D_MODEL))]
    args += [lp['w_o'], lp['post_mix_g'].reshape(1, D_MODEL)]

    out_shape = [jax.ShapeDtypeStruct((n_rows, D_MODEL), F32)]
    out_specs = [_row_spec(rows, D_MODEL)]
    scratch = []
    if sample:
        win_k, win_v = caches
        n_seq, past = win_k.shape[0], win_k.shape[1]
        cache_spec = pl.BlockSpec((SAMPLE_SEQS, past, ATTN_KV), lambda i: (i, 0, 0))
        in_specs += [cache_spec, cache_spec]
        args += [win_k, win_v]
        out_shape += [jax.ShapeDtypeStruct((n_seq, past, ATTN_KV), F32)] * 2
        out_specs += [cache_spec, cache_spec]
        out_shape += [jax.ShapeDtypeStruct((n_rows, SG_WIDTH), F32)]
        out_specs += [_row_spec(rows, SG_WIDTH)]
    else:
        out_shape += [jax.ShapeDtypeStruct((WINDOW, ATTN_KV), F32)] * 2
        out_specs += [_const_spec((WINDOW, ATTN_KV))] * 2
        scratch = [pltpu.VMEM((WINDOW, ATTN_KV), F32), pltpu.VMEM((WINDOW, ATTN_KV), F32)]
    scratch += [pltpu.VMEM((N_BRANCHES, rows, D_MODEL), F32)]

    return pl.pallas_call(
        functools.partial(_mix_kernel_tiles, sample=sample, schedule=schedule),
        grid=(steps,),
        in_specs=in_specs,
        out_specs=out_specs,
        out_shape=out_shape,
        scratch_shapes=scratch,
        compiler_params=_compiler_params(),
        name="mix_sample" if sample else "mix_prompt",
    )(*args)


def _ffn_call(x2d, seq_len, lp, conv_state):
    n_rows = x2d.shape[0]
    sample = conv_state is not None
    rows = SAMPLE_FFN_SEQS * seq_len if sample else PROMPT_ROWS
    steps = n_rows // rows
    in_specs = [
        _row_spec(rows, D_MODEL),
        _const_spec((1, D_MODEL)),
        _resident(),
        _const_spec((CONV_WIDTH, 2 * D_FF)),
        _const_spec((1, 2 * D_FF)),
        _resident(),
        _const_spec((1, D_MODEL)),
    ]
    args = [x2d, lp['pre_ffn_g'].reshape(1, D_MODEL), lp['w_up'], lp['conv_w'],
            lp['conv_b'].reshape(1, 2 * D_FF), lp['w_down'], lp['post_ffn_g'].reshape(1, D_MODEL)]
    out_shape = [jax.ShapeDtypeStruct((n_rows, D_MODEL), F32)]
    out_specs = [_row_spec(rows, D_MODEL)]
    n_slabs = 2 * D_FF // LANES
    if sample:
        n_seq = conv_state.shape[0]
        st_spec = pl.BlockSpec((SAMPLE_FFN_SEQS, CONV_WIDTH - 1, 2 * D_FF), lambda i: (i, 0, 0))
        in_specs += [st_spec]
        args += [conv_state]
        out_shape += [jax.ShapeDtypeStruct((n_seq, CONV_WIDTH - 1, 2 * D_FF), F32)]
        out_specs += [st_spec]
        scratch = [pltpu.VMEM((SAMPLE_FFN_SEQS, 2 * SUBLANES, LANES), F32)] * n_slabs
    else:
        out_shape += [jax.ShapeDtypeStruct((CONV_WIDTH - 1, 2 * D_FF), F32)]
        out_specs += [_const_spec((CONV_WIDTH - 1, 2 * D_FF))]
        scratch = [pltpu.VMEM((SUBLANES + rows, LANES), F32)] * n_slabs
    return pl.pallas_call(
        functools.partial(_ffn_kernel, sample=sample),
        grid=(steps,),
        in_specs=in_specs,
        out_specs=out_specs,
        out_shape=out_shape,
        scratch_shapes=scratch,
        compiler_params=_compiler_params(),
        name="ffn_sample" if sample else "ffn_prompt",
    )(*args)


def _memkv_call(mem2d, g, w_f32):
    return pl.pallas_call(
        _memkv_kernel,
        out_shape=[jax.ShapeDtypeStruct((mem2d.shape[0] * MEM_HEADS, MEM_HEAD_DIM), F32)] * 2
        + [jax.ShapeDtypeStruct((mem2d.shape[0], MEM_Q), BF16)] * 2,
        compiler_params=pltpu.CompilerParams(vmem_limit_bytes=VMEM_LIMIT_BYTES),
        name="mem_kv",
    )(mem2d, g.reshape(1, D_MODEL), w_f32)


def kernel(x_prompt, x_sample, cache_win_k, cache_win_v, cache_mem_k, cache_mem_v, state_conv, mem_prompt,
           pre_mix_g, w_in, attn_sinks, sg_ln_g, sg_ln_b, sg_w, sg_b, mem_norm_g, w_mem_kv, w_o,
           post_mix_g, pre_ffn_g, w_up, conv_w, conv_b, w_down, post_ffn_g):
    depth = w_in.shape[0]
    batch, seq, _ = x_prompt.shape
    dec_batch, dec_seq, _ = x_sample.shape
    past_len = PAST_LEN
    assert batch == 1 and depth == 1

    yp = x_prompt.reshape(batch * seq, D_MODEL)
    ys = x_sample.reshape(dec_batch * dec_seq, D_MODEL)
    outs = {name: [] for name in ('wk_p', 'wv_p', 'mk_p', 'mv_p', 'cv_p', 'wk_s', 'wv_s', 'sgv_s', 'cv_s')}
    for l in range(depth):
        lp = {
            'sinks': attn_sinks[l], 'pre_mix_g': pre_mix_g[l], 'w_in': w_in[l].astype(BF16),
            'sg_ln_g': sg_ln_g[l], 'sg_ln_b': sg_ln_b[l], 'sg_w': sg_w[l], 'sg_b': sg_b[l],
            'w_o': w_o[l].reshape(N_BRANCHES * ATTN_Q, D_MODEL).astype(BF16), 'post_mix_g': post_mix_g[l],
            'pre_ffn_g': pre_ffn_g[l], 'w_up': w_up[l].astype(BF16), 'conv_w': conv_w[l], 'conv_b': conv_b[l],
            'w_down': w_down[l].astype(BF16), 'post_ffn_g': post_ffn_g[l],
        }
        mem_k, mem_v, mem_k_b, mem_v_b = _memkv_call(mem_prompt.reshape(batch * N_MEM, D_MODEL), mem_norm_g[l],
                                                     w_mem_kv[l])
        outs['mk_p'].append(mem_k.reshape(batch, N_MEM, MEM_HEADS, MEM_HEAD_DIM))
        outs['mv_p'].append(mem_v.reshape(batch, N_MEM, MEM_HEADS, MEM_HEAD_DIM))

        x1, wk, wv = _mix_call(yp, 0, seq, lp, mem_k_b, mem_v_b, None)
        yp, nc = _ffn_call(x1, seq, lp, None)
        outs['wk_p'].append(wk.reshape(batch, N_KV_HEADS, HEAD_DIM, WINDOW).transpose(0, 3, 1, 2))
        outs['wv_p'].append(wv.reshape(batch, N_KV_HEADS, HEAD_DIM, WINDOW).transpose(0, 3, 1, 2))
        outs['cv_p'].append(nc.reshape(batch, CONV_WIDTH - 1, 2 * D_FF))

        past = cache_win_k.shape[2]
        x1s, wks, wvs, sgv = _mix_call(
            ys, past_len, dec_seq, lp,
            cache_mem_k[l].reshape(dec_batch, N_MEM * MEM_HEADS, MEM_HEAD_DIM),
            cache_mem_v[l].reshape(dec_batch, N_MEM * MEM_HEADS, MEM_HEAD_DIM),
            (cache_win_k[l].reshape(dec_batch, past, ATTN_KV), cache_win_v[l].reshape(dec_batch, past, ATTN_KV)))
        ys, ncs = _ffn_call(x1s, dec_seq, lp, state_conv[l])
        outs['wk_s'].append(wks.reshape(dec_batch, past, N_KV_HEADS, HEAD_DIM))
        outs['wv_s'].append(wvs.reshape(dec_batch, past, N_KV_HEADS, HEAD_DIM))
        outs['sgv_s'].append(sgv.reshape(dec_batch, dec_seq, SG_WIDTH))
        outs['cv_s'].append(ncs)

    return (yp.reshape(batch, seq, D_MODEL), ys.reshape(dec_batch, dec_seq, D_MODEL),
            jnp.stack(outs['wk_p']), jnp.stack(outs['wv_p']), jnp.stack(outs['mk_p']), jnp.stack(outs['mv_p']),
            jnp.stack(outs['cv_p']),
            jnp.stack(outs['wk_s']), jnp.stack(outs['wv_s']), jnp.stack(outs['sgv_s']), jnp.stack(outs['cv_s']))
```

```python
import functools
import math

import jax
import jax.numpy as jnp
import numpy as np
from jax import lax
from jax.experimental import pallas as pl
from jax.experimental.pallas import tpu as pltpu

F32 = jnp.float32
BF16 = jnp.bfloat16

D_MODEL = 1024
N_Q_HEADS = 8
N_KV_HEADS = 2
GROUP = N_Q_HEADS // N_KV_HEADS
HEAD_DIM = 64
ATTN_Q = N_Q_HEADS * HEAD_DIM
ATTN_KV = N_KV_HEADS * HEAD_DIM
WINDOW = 128
ROPE_THETA = 10000.0
CHUNK = 128
SG_GROUPS = 4
SG_GROUP_DIM = 128
SG_WIDTH = SG_GROUPS * SG_GROUP_DIM
N_MEM = 256
MEM_HEADS = 4
MEM_HEAD_DIM = 128
MEM_Q = MEM_HEADS * MEM_HEAD_DIM
N_BRANCHES = 3
D_FF = 2816
CONV_WIDTH = 3
EPS = 1e-6
NEG_INF = -1e30
LOG2E = math.log2(math.e)
PAST_LEN = 16384

OFF_Q = 0
OFF_K = ATTN_Q
OFF_V = OFF_K + ATTN_KV
OFF_SGU = OFF_V + ATTN_KV
OFF_SGV = OFF_SGU + SG_WIDTH
OFF_MQ = OFF_SGV + SG_WIDTH
OFF_GATE = OFF_MQ + MEM_Q
IN_WIDTH = OFF_GATE + N_BRANCHES * D_MODEL

LANES = 128
SUBLANES = 8
VMEM_LIMIT_BYTES = 56 * 1024 * 1024

PROMPT_ROWS = 512
PROMPT_MIX_ROWS = 512
PROMPT_MIX_SUB_ROWS = 256
MIX_SCHEDULE_ONE = ("headA inprojA ropeA scoresA gates2A geluA softmaxA valuesA gates2A outsA memvA mergeA")


def _rolling_mix_schedule(n_sub):
    names = [chr(ord('A') + i) for i in range(n_sub)]
    toks = [f"head{names[0]}", f"inproj{names[0]}", f"memsc{names[0]}", f"rope{names[0]}", f"scores{names[0]}"]
    for i, x in enumerate(names):
        w = names[i - 1] if i > 0 else None
        y = names[i + 1] if i + 1 < n_sub else None
        if w is None and y:
            toks += [f"head{y}", f"inproj{y}"]
        toks += [f"gelu{x}", f"sgmix{x}"] + ([f"gates3{w}"] if w else [])
        toks += [f"memsm{x}", f"memv{x}", f"softmax{x}"]
        if w:
            toks += [f"sig{w}"] + ([f"head{y}", f"inproj{y}"] if y else []) + [f"merge{w}"]
        toks += [f"pv{x}", f"gates3{x}" if y else f"gates6{x}"]
        if y:
            toks += [f"rope{y}", f"memsc{y}", f"scores{y}"]
        toks += [f"sgout{x}", f"attnout{x}", f"sig{x}"]
    return " ".join(toks + [f"merge{names[-1]}"])
SAMPLE_SEQS = 8
SAMPLE_FFN_SEQS = 32
FFN_CHUNK = 256
FFN_UP_CHUNK = 256
FFN_SCHEDULE = "head up22 vd11 tail"
GATE_CHUNK = 512


def _rms(x, g):
    return x * lax.rsqrt(jnp.mean(x * x, axis=-1, keepdims=True) + EPS) * g


def _dot(a, b):
    return jnp.dot(a, b, preferred_element_type=F32)


def _gelu_erf(x):
    return 0.5 * x * (1.0 + lax.erf(x * np.float32(math.sqrt(0.5))))


def _gelu_tanh(x):
    c = np.float32(math.sqrt(2.0 / math.pi))
    return x * (0.5 * (1.0 + jnp.tanh(c * (x + 0.044715 * (x * x * x)))))


def _lane_iota(shape):
    return lax.broadcasted_iota(jnp.int32, shape, len(shape) - 1)


def _rope(x, cos, sin_signed):
    w = x.shape[1]
    reps = w // LANES
    c = jnp.concatenate([cos] * reps, axis=1) if reps > 1 else cos
    s = jnp.concatenate([sin_signed] * reps, axis=1) if reps > 1 else sin_signed
    first_half = (_lane_iota(x.shape) % HEAD_DIM) < (HEAD_DIM // 2)
    swapped = jnp.where(first_half, pltpu.roll(x, w - HEAD_DIM // 2, 1), pltpu.roll(x, HEAD_DIM // 2, 1))
    return x * c + swapped * s


def _dup_head(kv, g):
    rolled = pltpu.roll(kv, HEAD_DIM, 1)
    low = _lane_iota(kv.shape) < HEAD_DIM
    return jnp.where(low, kv, rolled) if g == 0 else jnp.where(low, rolled, kv)


def _stack_group_queries(q, g):
    low = _lane_iota(q.shape[:2] + (LANES,)) < HEAD_DIM
    parts = []
    for i in range(GROUP):
        h = GROUP * g + i
        pair = q[:, :, (h // 2) * LANES:(h // 2 + 1) * LANES]
        parts.append(jnp.where(low if h % 2 == 0 else ~low, pair, 0.0))
    return jnp.concatenate(parts, axis=1)


def _unstack_group_outputs(o, rows):
    low = _lane_iota((o.shape[0], rows, LANES)) < HEAD_DIM
    pairs = []
    for p in range(GROUP // 2):
        even = o[:, (2 * p) * rows:(2 * p + 1) * rows]
        odd = o[:, (2 * p + 1) * rows:(2 * p + 2) * rows]
        pairs.append(jnp.where(low, even, odd))
    return jnp.concatenate(pairs, axis=2)


def _sink_softmax(s, scale, mask, sink):
    t = jnp.where(mask, s * np.float32(scale * LOG2E), NEG_INF)
    sink2 = sink * np.float32(LOG2E)
    m = jnp.maximum(jnp.max(t, axis=-1, keepdims=True), sink2)
    e = jnp.exp2(t - m)
    denom = jnp.sum(e, axis=-1, keepdims=True) + jnp.exp2(sink2 - m)
    return (e * (1.0 / denom)).astype(BF16)


def _softmax(s, scale):
    t = s * np.float32(scale * LOG2E)
    m = jnp.max(t, axis=-1, keepdims=True)
    e = jnp.exp2(t - m)
    return (e * (1.0 / jnp.sum(e, axis=-1, keepdims=True))).astype(BF16)


def _sigmoid(x):
    return 1.0 / (1.0 + jnp.exp2(x * np.float32(-LOG2E)))


def _sink_column(sinks_ref, g, rows):
    r = lax.broadcasted_iota(jnp.int32, (1, GROUP * rows, 1), 1)
    col = jnp.full((1, GROUP * rows, 1), sinks_ref[GROUP * g + GROUP - 1], F32)
    for i in range(GROUP - 2, -1, -1):
        col = jnp.where(r < (i + 1) * rows, sinks_ref[GROUP * g + i], col)
    return col


def _sub_tiles(schedule):
    return 1 + max(ord(tok[-1]) - ord('A') for tok in schedule.split())


def _mix_program(refs, sample, n_sub):
    if sample:
        (sinks_ref, x_ref, cos_ref, sin_ref, g_pre_ref, w_in_ref, ln_g_ref, ln_b_ref, wmix_ref, sgb_ref,
         mk_ref, mv_ref, w_o_ref, g_post_ref, ck_ref, cv_ref,
         x1_ref, wk_ref, wv_ref, sgv_ref, gate_ref) = refs
    else:
        (sinks_ref, x_ref, cos_ref, sin_ref, cos_s_ref, sin_s_ref, cos_b_ref, sin_b_ref,
         g_pre_ref, w_in_ref, ln_g_ref, ln_b_ref, wmix_ref, sgb_ref,
         mk_ref, mv_ref, w_o_ref, g_post_ref,
         x1_ref, wk_ref, wv_ref, kcar_ref, vcar_ref, gate_ref) = refs
    rows = x_ref.shape[0] // n_sub
    seqs = rows // SUBLANES
    step = pl.program_id(0)
    assert not (sample and n_sub > 1)

    def init():
        if not sample:
            @pl.when(step == 0)
            def _():
                kcar_ref[...] = jnp.zeros_like(kcar_ref)
                vcar_ref[...] = jnp.zeros_like(vcar_ref)

    st = [dict() for _ in range(n_sub)]
    chunk = wmix_ref.shape[1]
    mem_scale = MEM_HEAD_DIM ** -0.5

    def rows_of(t):
        return slice(t * rows, (t + 1) * rows)

    def head(t):
        s = st[t]
        s['x'] = x_ref[rows_of(t), :]
        s['h'] = _rms(s['x'], g_pre_ref[...]).astype(BF16)
        s['gate_pieces'] = [(n, c) for n in range(N_BRANCHES) for c in range(0, D_MODEL, GATE_CHUNK)]
        s['logit_pieces'] = []

    def gates(t, count):
        s = st[t]
        for _ in range(min(count, len(s['gate_pieces']))):
            n, c = s['gate_pieces'].pop(0)
            col = OFF_GATE + n * D_MODEL + c
            gate_ref[n, rows_of(t), c:c + GATE_CHUNK] = _dot(s['h'], w_in_ref[:, col:col + GATE_CHUNK])
            s['logit_pieces'].append((n, c))

    def gate_sigmoids(t):
        s = st[t]
        while s['logit_pieces']:
            n, c = s['logit_pieces'].pop(0)
            piece = (n, rows_of(t), slice(c, c + GATE_CHUNK))
            gate_ref[piece] = _sigmoid(gate_ref[piece])

    def in_proj(t):
        s = st[t]
        s['zqkv'] = _dot(s['h'], w_in_ref[:, OFF_Q:OFF_SGU])
        s['zsg'] = _dot(s['h'], w_in_ref[:, OFF_SGU:OFF_MQ])
        s['mq'] = _dot(s['h'], w_in_ref[:, OFF_MQ:OFF_GATE])

    def rope_stage(t):
        s = st[t]
        zqkv = s.pop('zqkv')
        if sample:
            cos, sin_signed = cos_ref[...], sin_ref[...]
        else:
            cb = cos_b_ref[pl.ds(step, 1), :]
            sb = sin_b_ref[pl.ds(step, 1), :]
            cos = cb * cos_ref[rows_of(t), :] - sb * sin_ref[rows_of(t), :]
            sin_signed = sb * cos_s_ref[rows_of(t), :] + cb * sin_s_ref[rows_of(t), :]
        qk = _rope(zqkv[:, :OFF_V], cos, sin_signed)
        q = qk[:, :ATTN_Q]
        k = qk[:, OFF_K:OFF_V]
        v = zqkv[:, OFF_V:OFF_SGU]
        if sample:
            past = ck_ref.shape[1]
            k3 = k.reshape(seqs, SUBLANES, ATTN_KV)
            v3 = v.reshape(seqs, SUBLANES, ATTN_KV)
            ck = ck_ref[...]
            cv = cv_ref[...]
            wk_ref[:, :past - SUBLANES, :] = ck[:, SUBLANES:, :]
            wk_ref[:, past - SUBLANES:, :] = k3
            wv_ref[:, :past - SUBLANES, :] = cv[:, SUBLANES:, :]
            wv_ref[:, past - SUBLANES:, :] = v3
            n_keys = past + 2 * SUBLANES
            pad = jnp.zeros((seqs, SUBLANES, ATTN_KV), F32)
            kk = jnp.concatenate([ck, k3, pad], axis=1).reshape(seqs * n_keys, ATTN_KV)
            vv = jnp.concatenate([cv, v3, pad], axis=1).reshape(seqs * n_keys, ATTN_KV)
            q3 = q.reshape(seqs, SUBLANES, ATTN_Q)
            s['q_rows'] = SUBLANES
            qi = lax.broadcasted_iota(jnp.int32, (1, GROUP * SUBLANES, n_keys), 1) % SUBLANES
            kj = lax.broadcasted_iota(jnp.int32, (1, GROUP * SUBLANES, n_keys), 2)
            s['mask'] = (kj > qi + (past - WINDOW)) & (kj <= qi + past)
            s['kds'] = [_dup_head(kk, g).astype(BF16).reshape(seqs, n_keys, LANES) for g in range(N_KV_HEADS)]
            s['vds'] = [_dup_head(vv, g).astype(BF16).reshape(seqs, n_keys, LANES) for g in range(N_KV_HEADS)]
        else:
            nb = rows // WINDOW
            k_prev = kcar_ref[...] if t == 0 else st[t - 1]['k_last']
            v_prev = vcar_ref[...] if t == 0 else st[t - 1]['v_last']
            s['k_last'] = k[rows - WINDOW:]
            s['v_last'] = v[rows - WINDOW:]
            if t == n_sub - 1:
                wk_ref[...] = s['k_last'].T
                wv_ref[...] = s['v_last'].T
                kcar_ref[...] = s['k_last']
                vcar_ref[...] = s['v_last']
            kk = jnp.concatenate([k_prev, k], axis=0)
            vv = jnp.concatenate([v_prev, v], axis=0)
            q3 = q.reshape(nb, WINDOW, ATTN_Q)
            s['q_rows'] = WINDOW
            shape = (nb, GROUP * WINDOW, 2 * WINDOW)
            qi = lax.broadcasted_iota(jnp.int32, shape, 1) % WINDOW
            kj = lax.broadcasted_iota(jnp.int32, shape, 2)
            s['mask'] = (kj > qi) & (kj <= qi + WINDOW)
            if t == 0:
                blk = lax.broadcasted_iota(jnp.int32, shape, 0)
                first_key = jnp.where((blk == 0) & (step == 0), WINDOW, 0)
                s['mask'] = s['mask'] & (kj >= first_key)
            s['kds'], s['vds'] = [], []
            for g in range(N_KV_HEADS):
                kd = _dup_head(kk, g).astype(BF16)
                vd = _dup_head(vv, g).astype(BF16)
                s['kds'].append(jnp.stack([kd[n * WINDOW:(n + 2) * WINDOW] for n in range(nb)], axis=0))
                s['vds'].append(jnp.stack([vd[n * WINDOW:(n + 2) * WINDOW] for n in range(nb)], axis=0))
        s['qss'] = [_stack_group_queries(q3, g).astype(BF16) for g in range(N_KV_HEADS)]
        s['sinks'] = [_sink_column(sinks_ref, g, s['q_rows']) for g in range(N_KV_HEADS)]

    def scores(t):
        s = st[t]
        qss, kds = s.pop('qss'), s.pop('kds')
        s['scores'] = [jnp.einsum('bqd,bkd->bqk', q, k, preferred_element_type=F32) for q, k in zip(qss, kds)]

    def gelu_ln(t, part=None):
        s = st[t]
        if part in (None, 0):
            s['u'] = _gelu_erf(s['zsg'][:, :SG_WIDTH])
        if part in (None, 1):
            gv = _gelu_erf(s['zsg'][:, SG_WIDTH:])
            gc = gv - jnp.mean(gv, axis=-1, keepdims=True)
            var = jnp.mean(gc * gc, axis=-1, keepdims=True)
            vn = gc * lax.rsqrt(var + EPS) * ln_g_ref[...] + ln_b_ref[...]
            if sample:
                sgv_ref[...] = vn
            s['vn_b'] = vn.astype(BF16)

    def softmax(t, g=None):
        s = st[t]
        for gi in range(N_KV_HEADS) if g is None else (g,):
            s.setdefault('probs', {})[gi] = _sink_softmax(s['scores'][gi], HEAD_DIM ** -0.5, s['mask'],
                                                          s['sinks'][gi])

    def values(t):
        attn_values(t)
        mem_scores(t)
        spatial_mix(t)

    def attn_values(t):
        s = st[t]
        probs, vds = s.pop('probs'), s.pop('vds')
        s['attn_outs'] = [jnp.einsum('bqk,bkd->bqd', probs[g], vds[g], preferred_element_type=F32)
                          for g in range(len(vds))]

    def mem_scores(t):
        s = st[t]
        mq = s.pop('mq')
        s['mem_scores'], s['mem_vals'] = [], []
        if sample:
            mq3 = mq.reshape(seqs, SUBLANES, MEM_Q)
            lane_head = _lane_iota(mq3.shape) // MEM_HEAD_DIM
            q_all = jnp.concatenate([jnp.where(lane_head == hd, mq3, 0.0) for hd in range(MEM_HEADS)],
                                    axis=1).astype(BF16)
            k_all = jnp.concatenate([mk_ref[:, pl.ds(hd, N_MEM, stride=MEM_HEADS), :] for hd in range(MEM_HEADS)],
                                    axis=2).astype(BF16)
            v_all = jnp.concatenate([mv_ref[:, pl.ds(hd, N_MEM, stride=MEM_HEADS), :] for hd in range(MEM_HEADS)],
                                    axis=2).astype(BF16)
            s['mem_scores'].append(jnp.einsum('bqd,bkd->bqk', q_all, k_all, preferred_element_type=F32))
            s['mem_vals'].append(v_all)
        else:
            for hd in range(MEM_HEADS):
                sl = slice(hd * MEM_HEAD_DIM, (hd + 1) * MEM_HEAD_DIM)
                s['mem_vals'].append(mv_ref[:, sl])
                s['mem_scores'].append(lax.dot_general(mq[:, sl].astype(BF16), mk_ref[:, sl],
                                                       (((1,), (1,)), ((), ())), preferred_element_type=F32))

    def spatial_mix(t):
        s = st[t]
        vn_b = s.pop('vn_b')
        t_idx = lax.broadcasted_iota(jnp.int32, (chunk, chunk), 0)
        s_idx = lax.broadcasted_iota(jnp.int32, (chunk, chunk), 1)
        keep = t_idx >= s_idx
        if sample:
            keep = keep & ((t_idx // SUBLANES) == (s_idx // SUBLANES))
        s['sg_cols'] = []
        for g in range(SG_GROUPS):
            wm = jnp.where(keep, wmix_ref[g], 0.0).astype(BF16)
            blocks = [_dot(wm, vn_b[c * chunk:(c + 1) * chunk, g * SG_GROUP_DIM:(g + 1) * SG_GROUP_DIM])
                      for c in range(rows // chunk)]
            s['sg_cols'].append(jnp.concatenate(blocks, axis=0) if len(blocks) > 1 else blocks[0])

    def branch_outs(t):
        mem_softmax(t)
        attn_out(t)
        spatial_out(t)

    def mem_softmax(t):
        s = st[t]
        s['mem_probs'] = [_softmax(sc, mem_scale) for sc in s.pop('mem_scores')]

    def attn_out(t):
        s = st[t]
        s['attn'] = jnp.concatenate([_unstack_group_outputs(o, s['q_rows']) for o in s.pop('attn_outs')],
                                    axis=2).reshape(rows, ATTN_Q).astype(BF16)

    def spatial_out(t):
        s = st[t]
        mixed_sg = jnp.concatenate(s.pop('sg_cols'), axis=1)
        bias = sgb_ref[...]
        if rows // chunk > 1:
            bias = jnp.concatenate([bias] * (rows // chunk), axis=0)
        s['sg'] = (s.pop('u') * (mixed_sg + bias)).astype(BF16)

    def mem_values(t):
        s = st[t]
        mem_probs, mem_vals = s.pop('mem_probs'), s.pop('mem_vals')
        if sample:
            o = jnp.einsum('bqk,bkd->bqd', mem_probs[0], mem_vals[0], preferred_element_type=F32)
            cols = [o[:, hd * SUBLANES:(hd + 1) * SUBLANES, hd * MEM_HEAD_DIM:(hd + 1) * MEM_HEAD_DIM]
                    .reshape(rows, MEM_HEAD_DIM) for hd in range(MEM_HEADS)]
        else:
            cols = [_dot(mem_probs[hd], mem_vals[hd]) for hd in range(MEM_HEADS)]
        s['memo'] = jnp.concatenate(cols, axis=1).astype(BF16)

    def merge(t):
        s = st[t]
        gates(t, len(s['gate_pieces']))
        gate_sigmoids(t)
        mixed = None
        for n, name in enumerate(('attn', 'sg', 'memo')):
            proj = _dot(s.pop(name), w_o_ref[n * ATTN_Q:(n + 1) * ATTN_Q, :])
            term = gate_ref[n, rows_of(t), :] * proj
            mixed = term if mixed is None else mixed + term
        x1_ref[rows_of(t), :] = s.pop('x') + _rms(mixed, g_post_ref[...])

    stages = {'head': head, 'inproj': in_proj, 'rope': rope_stage, 'scores': scores, 'gelu': gelu_ln,
              'softmax': softmax, 'values': values, 'outs': branch_outs, 'memv': mem_values, 'merge': merge,
              'sig': gate_sigmoids, 'pv': attn_values, 'memsc': mem_scores, 'sgmix': spatial_mix,
              'memsm': mem_softmax, 'attnout': attn_out, 'sgout': spatial_out}
    def run(tok):
        name, t = tok[:-1], ord(tok[-1]) - ord('A')
        if name.startswith('gates'):
            gates(t, int(name[len('gates'):]))
        elif name[-1].isdigit():
            stages[name[:-1]](t, int(name[-1]))
        else:
            stages[name](t)

    return init, run


def _mix_kernel_tiles(*refs, sample, schedule):
    init, run = _mix_program(refs, sample, _sub_tiles(schedule))
    init()
    for tok in schedule.split():
        run(tok)


def _ffn_program(refs, sample):
    if sample:
        (x_ref, g_pre_ref, w_up_ref, cw_ref, cb_ref, w_down_ref, g_post_ref, st_ref,
         y_ref, nc_ref, *up_s) = refs
    else:
        (x_ref, g_pre_ref, w_up_ref, cw_ref, cb_ref, w_down_ref, g_post_ref,
         y_ref, nc_ref, *up_s) = refs
    rows = x_ref.shape[0]
    step = pl.program_id(0)
    st = {}

    def init():
        if not sample:
            @pl.when(step == 0)
            def _():
                for slab in up_s:
                    slab[:SUBLANES, :] = jnp.zeros((SUBLANES, LANES), F32)

    def head():
        st['x'] = x_ref[...]
        st['h'] = _rms(st['x'], g_pre_ref[...]).astype(BF16)
        st['up_pieces'] = [(half * D_FF + n0, min(FFN_UP_CHUNK, D_FF - n0))
                           for n0 in range(0, D_FF, FFN_UP_CHUNK) for half in range(2)]
        st['chunks'] = [(c0, min(FFN_CHUNK, D_FF - c0)) for c0 in range(0, D_FF, FFN_CHUNK)]
        st['f'] = None

    hist = CONV_WIDTH - 1
    seqs = rows // SUBLANES

    def up_cols(col0, width):
        up = jnp.dot(st['h'], w_up_ref[:, col0:col0 + width], preferred_element_type=F32)
        for t in range(width // LANES):
            s = col0 // LANES + t
            blk = up[:, t * LANES:(t + 1) * LANES]
            if sample:
                up_s[s][:, SUBLANES:, :] = blk.reshape(seqs, SUBLANES, LANES)
                up_s[s][:, SUBLANES - hist:SUBLANES, :] = st_ref[:, :, s * LANES:(s + 1) * LANES]
            else:
                up_s[s][SUBLANES:, :] = blk

    def conv_slab(s):
        cols = slice(s * LANES, (s + 1) * LANES)
        if sample:
            taps = [up_s[s][:, SUBLANES - hist + j:2 * SUBLANES - hist + j, :] for j in range(CONV_WIDTH)]
            nc_ref[:, :, cols] = up_s[s][:, 2 * SUBLANES - hist:, :]
        else:
            taps = [up_s[s][SUBLANES - hist + j:SUBLANES - hist + j + rows, :] for j in range(CONV_WIDTH)]
            nc_ref[:, cols] = up_s[s][SUBLANES + rows - hist:, :]
            up_s[s][:SUBLANES, :] = up_s[s][rows:, :]
        c = cb_ref[:, cols]
        for j in range(CONV_WIDTH):
            c = c + taps[j] * cw_ref[j:j + 1, cols]
        return c.reshape(rows, LANES)

    def up(count):
        for _ in range(min(count, len(st['up_pieces']))):
            up_cols(*st['up_pieces'].pop(0))

    def conv_down(count):
        for _ in range(min(count, len(st['chunks']))):
            c0, width = st['chunks'].pop(0)
            acts = []
            for t in range(width // LANES):
                gate = conv_slab(c0 // LANES + t)
                val = conv_slab((D_FF + c0) // LANES + t)
                acts.append((_gelu_tanh(gate) * val).astype(BF16))
            act = jnp.concatenate(acts, axis=1)
            part = jnp.dot(act, w_down_ref[c0:c0 + width, :], preferred_element_type=F32)
            st['f'] = part if st['f'] is None else st['f'] + part

    def tail():
        up(len(st['up_pieces']))
        conv_down(len(st['chunks']))
        y_ref[...] = st.pop('x') + _rms(st.pop('f'), g_post_ref[...])

    def run(tok):
        if tok == 'head':
            head()
        elif tok == 'tail':
            tail()
        elif tok.startswith('up'):
            up(int(tok[2:]))
        else:
            assert tok.startswith('vd'), tok
            conv_down(int(tok[2:]))

    return init, run


def _ffn_kernel(*refs, sample):
    init, run = _ffn_program(refs, sample)
    init()
    for tok in FFN_SCHEDULE.split():
        run(tok)


def _memkv_kernel(mem_ref, g_ref, w_ref, k_ref, v_ref, kb_ref, vb_ref):
    h = _rms(mem_ref[...], g_ref[...]).astype(BF16)
    kv = _dot(h, w_ref[...].astype(BF16))
    kb_ref[...] = kv[:, :MEM_Q].astype(BF16)
    vb_ref[...] = kv[:, MEM_Q:].astype(BF16)
    for hd in range(MEM_HEADS):
        k_ref[pl.ds(hd, N_MEM, stride=MEM_HEADS), :] = kv[:, hd * MEM_HEAD_DIM:(hd + 1) * MEM_HEAD_DIM]
        v_ref[pl.ds(hd, N_MEM, stride=MEM_HEADS), :] = kv[:, MEM_Q + hd * MEM_HEAD_DIM:
                                                          MEM_Q + (hd + 1) * MEM_HEAD_DIM]


def _resident():
    return pl.BlockSpec(memory_space=pltpu.VMEM)


def _row_spec(rows, width):
    return pl.BlockSpec((rows, width), lambda i: (i, 0))


def _const_spec(shape):
    nd = len(shape)
    return pl.BlockSpec(shape, lambda i: (0,) * nd)


def _compiler_params():
    return pltpu.CompilerParams(dimension_semantics=("arbitrary",), vmem_limit_bytes=VMEM_LIMIT_BYTES)


def _rope_tables(pos):
    half = HEAD_DIM // 2
    inv_freq = np.float64(ROPE_THETA) ** (-np.arange(half, dtype=np.float64) / half)
    ang = pos.astype(np.float64)[:, None] * inv_freq[None, :]
    cos = np.concatenate([np.cos(ang)] * (LANES // half), axis=1).astype(np.float32)
    sin = np.concatenate([np.sin(ang)] * (LANES // half), axis=1).astype(np.float32)
    sign = np.concatenate([-np.ones((1, half), np.float32), np.ones((1, half), np.float32)]
                          * (LANES // HEAD_DIM), axis=1)
    return cos, sin, cos * sign, sin * sign


def _mix_call(x2d, start, seq_len, lp, mem_k, mem_v, caches):
    n_rows = x2d.shape[0]
    sample = caches is not None
    rows = SAMPLE_SEQS * seq_len if sample else PROMPT_MIX_ROWS
    schedule = MIX_SCHEDULE_ONE if sample else _rolling_mix_schedule(PROMPT_MIX_ROWS // PROMPT_MIX_SUB_ROWS)
    steps = n_rows // rows
    if sample:
        pos = start + (np.arange(rows, dtype=np.int32) % seq_len)
        cos, _, _, sin_signed = _rope_tables(pos)
        rope_args = [cos, sin_signed]
        rope_specs = [_const_spec((rows, LANES))] * 2
        chunk = rows
        wmix = jnp.tile(lp['sg_w'][:, :seq_len, :seq_len], (1, rows // seq_len, rows // seq_len))
        sgb = jnp.repeat(jnp.tile(lp['sg_b'][:, :seq_len], (1, rows // seq_len)).T, SG_GROUP_DIM, axis=1)
    else:
        in_step = _rope_tables(np.arange(rows, dtype=np.int32))
        cos_b, sin_b, _, _ = _rope_tables(start + rows * np.arange(steps, dtype=np.int32))
        rope_args = list(in_step) + [cos_b, sin_b]
        rope_specs = [_const_spec((rows, LANES))] * 4 + [_const_spec((steps, LANES))] * 2
        chunk = CHUNK
        wmix = lp['sg_w'][:, :chunk, :chunk]
        sgb = jnp.repeat(lp['sg_b'][:, :chunk].T, SG_GROUP_DIM, axis=1)

    in_specs = [
        pl.BlockSpec(memory_space=pltpu.SMEM),
        _row_spec(rows, D_MODEL),
        *rope_specs,
        _const_spec((1, D_MODEL)),
        _resident(),
        _const_spec((1, SG_WIDTH)), _const_spec((1, SG_WIDTH)),
        _const_spec((SG_GROUPS, chunk, chunk)),
        _const_spec((chunk, SG_WIDTH)),
    ]
    args = [lp['sinks'], x2d, *rope_args, lp['pre_mix_g'].reshape(1, D_MODEL), lp['w_in'],
            lp['sg_ln_g'].reshape(1, SG_WIDTH), lp['sg_ln_b'].reshape(1, SG_WIDTH), wmix, sgb]
    if sample:
        seqs = SAMPLE_SEQS
        in_specs += [pl.BlockSpec((seqs, N_MEM * MEM_HEADS, MEM_HEAD_DIM), lambda i: (i, 0, 0))] * 2
    else:
        in_specs += [_const_spec((N_MEM, MEM_Q))] * 2
    args += [mem_k, mem_v]
    in_specs += [_resident(), _const_spec((1, D_MODEL))]
    args += [lp['w_o'], lp['post_mix_g'].reshape(1, D_MODEL)]

    out_shape = [jax.ShapeDtypeStruct((n_rows, D_MODEL), F32)]
    out_specs = [_row_spec(rows, D_MODEL)]
    scratch = []
    if sample:
        win_k, win_v = caches
        n_seq, past = win_k.shape[0], win_k.shape[1]
        cache_spec = pl.BlockSpec((SAMPLE_SEQS, past, ATTN_KV), lambda i: (i, 0, 0))
        in_specs += [cache_spec, cache_spec]
        args += [win_k, win_v]
        out_shape += [jax.ShapeDtypeStruct((n_seq, past, ATTN_KV), F32)] * 2
        out_specs += [cache_spec, cache_spec]
        out_shape += [jax.ShapeDtypeStruct((n_rows, SG_WIDTH), F32)]
        out_specs += [_row_spec(rows, SG_WIDTH)]
    else:
        out_shape += [jax.ShapeDtypeStruct((WINDOW, ATTN_KV), F32)] * 2
        out_specs += [_const_spec((WINDOW, ATTN_KV))] * 2
        scratch = [pltpu.VMEM((WINDOW, ATTN_KV), F32), pltpu.VMEM((WINDOW, ATTN_KV), F32)]
    scratch += [pltpu.VMEM((N_BRANCHES, rows, D_MODEL), F32)]

    return pl.pallas_call(
        functools.partial(_mix_kernel_tiles, sample=sample, schedule=schedule),
        grid=(steps,),
        in_specs=in_specs,
        out_specs=out_specs,
        out_shape=out_shape,
        scratch_shapes=scratch,
        compiler_params=_compiler_params(),
        name="mix_sample" if sample else "mix_prompt",
    )(*args)


def _ffn_call(x2d, seq_len, lp, conv_state):
    n_rows = x2d.shape[0]
    sample = conv_state is not None
    rows = SAMPLE_FFN_SEQS * seq_len if sample else PROMPT_ROWS
    steps = n_rows // rows
    in_specs = [
        _row_spec(rows, D_MODEL),
        _const_spec((1, D_MODEL)),
        _resident(),
        _const_spec((CONV_WIDTH, 2 * D_FF)),
        _const_spec((1, 2 * D_FF)),
        _resident(),
        _const_spec((1, D_MODEL)),
    ]
    args = [x2d, lp['pre_ffn_g'].reshape(1, D_MODEL), lp['w_up'], lp['conv_w'],
            lp['conv_b'].reshape(1, 2 * D_FF), lp['w_down'], lp['post_ffn_g'].reshape(1, D_MODEL)]
    out_shape = [jax.ShapeDtypeStruct((n_rows, D_MODEL), F32)]
    out_specs = [_row_spec(rows, D_MODEL)]
    n_slabs = 2 * D_FF // LANES
    if sample:
        n_seq = conv_state.shape[0]
        st_spec = pl.BlockSpec((SAMPLE_FFN_SEQS, CONV_WIDTH - 1, 2 * D_FF), lambda i: (i, 0, 0))
        in_specs += [st_spec]
        args += [conv_state]
        out_shape += [jax.ShapeDtypeStruct((n_seq, CONV_WIDTH - 1, 2 * D_FF), F32)]
        out_specs += [st_spec]
        scratch = [pltpu.VMEM((SAMPLE_FFN_SEQS, 2 * SUBLANES, LANES), F32)] * n_slabs
    else:
        out_shape += [jax.ShapeDtypeStruct((CONV_WIDTH - 1, 2 * D_FF), F32)]
        out_specs += [_const_spec((CONV_WIDTH - 1, 2 * D_FF))]
        scratch = [pltpu.VMEM((SUBLANES + rows, LANES), F32)] * n_slabs
    return pl.pallas_call(
        functools.partial(_ffn_kernel, sample=sample),
        grid=(steps,),
        in_specs=in_specs,
        out_specs=out_specs,
        out_shape=out_shape,
        scratch_shapes=scratch,
        compiler_params=_compiler_params(),
        name="ffn_sample" if sample else "ffn_prompt",
    )(*args)


def _memkv_call(mem2d, g, w_f32):
    return pl.pallas_call(
        _memkv_kernel,
        out_shape=[jax.ShapeDtypeStruct((mem2d.shape[0] * MEM_HEADS, MEM_HEAD_DIM), F32)] * 2
        + [jax.ShapeDtypeStruct((mem2d.shape[0], MEM_Q), BF16)] * 2,
        compiler_params=pltpu.CompilerParams(vmem_limit_bytes=VMEM_LIMIT_BYTES),
        name="mem_kv",
    )(mem2d, g.reshape(1, D_MODEL), w_f32)


def kernel(x_prompt, x_sample, cache_win_k, cache_win_v, cache_mem_k, cache_mem_v, state_conv, mem_prompt,
           pre_mix_g, w_in, attn_sinks, sg_ln_g, sg_ln_b, sg_w, sg_b, mem_norm_g, w_mem_kv, w_o,
           post_mix_g, pre_ffn_g, w_up, conv_w, conv_b, w_down, post_ffn_g):
    depth = w_in.shape[0]
    batch, seq, _ = x_prompt.shape
    dec_batch, dec_seq, _ = x_sample.shape
    past_len = PAST_LEN
    assert batch == 1 and depth == 1

    yp = x_prompt.reshape(batch * seq, D_MODEL)
    ys = x_sample.reshape(dec_batch * dec_seq, D_MODEL)
    outs = {name: [] for name in ('wk_p', 'wv_p', 'mk_p', 'mv_p', 'cv_p', 'wk_s', 'wv_s', 'sgv_s', 'cv_s')}
    for l in range(depth):
        lp = {
            'sinks': attn_sinks[l], 'pre_mix_g': pre_mix_g[l], 'w_in': w_in[l].astype(BF16),
            'sg_ln_g': sg_ln_g[l], 'sg_ln_b': sg_ln_b[l], 'sg_w': sg_w[l], 'sg_b': sg_b[l],
            'w_o': w_o[l].reshape(N_BRANCHES * ATTN_Q, D_MODEL).astype(BF16), 'post_mix_g': post_mix_g[l],
            'pre_ffn_g': pre_ffn_g[l], 'w_up': w_up[l].astype(BF16), 'conv_w': conv_w[l], 'conv_b': conv_b[l],
            'w_down': w_down[l].astype(BF16), 'post_ffn_g': post_ffn_g[l],
        }
        mem_k, mem_v, mem_k_b, mem_v_b = _memkv_call(mem_prompt.reshape(batch * N_MEM, D_MODEL), mem_norm_g[l],
                                                     w_mem_kv[l])
        outs['mk_p'].append(mem_k.reshape(batch, N_MEM, MEM_HEADS, MEM_HEAD_DIM))
        outs['mv_p'].append(mem_v.reshape(batch, N_MEM, MEM_HEADS, MEM_HEAD_DIM))

        x1, wk, wv = _mix_call(yp, 0, seq, lp, mem_k_b, mem_v_b, None)
        yp, nc = _ffn_call(x1, seq, lp, None)
        outs['wk_p'].append(wk.reshape(batch, N_KV_HEADS, HEAD_DIM, WINDOW).transpose(0, 3, 1, 2))
        outs['wv_p'].append(wv.reshape(batch, N_KV_HEADS, HEAD_DIM, WINDOW).transpose(0, 3, 1, 2))
        outs['cv_p'].append(nc.reshape(batch, CONV_WIDTH - 1, 2 * D_FF))

        past = cache_win_k.shape[2]
        x1s, wks, wvs, sgv = _mix_call(
            ys, past_len, dec_seq, lp,
            cache_mem_k[l].reshape(dec_batch, N_MEM * MEM_HEADS, MEM_HEAD_DIM),
            cache_mem_v[l].reshape(dec_batch, N_MEM * MEM_HEADS, MEM_HEAD_DIM),
            (cache_win_k[l].reshape(dec_batch, past, ATTN_KV), cache_win_v[l].reshape(dec_batch, past, ATTN_KV)))
        ys, ncs = _ffn_call(x1s, dec_seq, lp, state_conv[l])
        outs['wk_s'].append(wks.reshape(dec_batch, past, N_KV_HEADS, HEAD_DIM))
        outs['wv_s'].append(wvs.reshape(dec_batch, past, N_KV_HEADS, HEAD_DIM))
        outs['sgv_s'].append(sgv.reshape(dec_batch, dec_seq, SG_WIDTH))
        outs['cv_s'].append(ncs)

    return (yp.reshape(batch, seq, D_MODEL), ys.reshape(dec_batch, dec_seq, D_MODEL),
            jnp.stack(outs['wk_p']), jnp.stack(outs['wv_p']), jnp.stack(outs['mk_p']), jnp.stack(outs['mv_p']),
            jnp.stack(outs['cv_p']),
            jnp.stack(outs['wk_s']), jnp.stack(outs['wv_s']), jnp.stack(outs['sgv_s']), jnp.stack(outs['cv_s']))
```

```python
import functools
import math

import jax
import jax.numpy as jnp
import numpy as np
from jax import lax
from jax.experimental import pallas as pl
from jax.experimental.pallas import tpu as pltpu

F32 = jnp.float32
BF16 = jnp.bfloat16

D_MODEL = 1024
N_Q_HEADS = 8
N_KV_HEADS = 2
GROUP = N_Q_HEADS // N_KV_HEADS
HEAD_DIM = 64
ATTN_Q = N_Q_HEADS * HEAD_DIM
ATTN_KV = N_KV_HEADS * HEAD_DIM
WINDOW = 128
ROPE_THETA = 10000.0
CHUNK = 128
SG_GROUPS = 4
SG_GROUP_DIM = 128
SG_WIDTH = SG_GROUPS * SG_GROUP_DIM
N_MEM = 256
MEM_HEADS = 4
MEM_HEAD_DIM = 128
MEM_Q = MEM_HEADS * MEM_HEAD_DIM
N_BRANCHES = 3
D_FF = 2816
CONV_WIDTH = 3
EPS = 1e-6
NEG_INF = -1e30
LOG2E = math.log2(math.e)
PAST_LEN = 16384

OFF_Q = 0
OFF_K = ATTN_Q
OFF_V = OFF_K + ATTN_KV
OFF_SGU = OFF_V + ATTN_KV
OFF_SGV = OFF_SGU + SG_WIDTH
OFF_MQ = OFF_SGV + SG_WIDTH
OFF_GATE = OFF_MQ + MEM_Q
IN_WIDTH = OFF_GATE + N_BRANCHES * D_MODEL

LANES = 128
SUBLANES = 8
VMEM_LIMIT_BYTES = 56 * 1024 * 1024

PROMPT_ROWS = 512
PROMPT_MIX_ROWS = 512
MIX_SCHEDULE_ONE = ("headA inprojA ropeA scoresA gates2A geluA softmaxA valuesA gates2A outsA memvA mergeA")


MIX_SCHEDULE_TWO = (
    "headA inprojA memscA ropeA scoresA headB inprojB geluA sgmixA gates1A memsmA memvA gates1A softmaxA pvA "
    "gates1A ropeB memscB scoresB sgoutA attnoutA sigA geluB gates1A sgmixB gates1A memsmB memvB gates1A "
    "softmaxB gates1B sigA pvB gates1B mergeA gates1B sgoutB gates1B attnoutB gates2B sigB mergeB")
SAMPLE_SEQS = 8
SAMPLE_FFN_SEQS = 32
FFN_CHUNK = 256
FFN_UP_CHUNK = 256
FFN_SCHEDULE = "head up22 vd11 tail"
GATE_CHUNK = 512


def _rms(x, g):
    return x * lax.rsqrt(jnp.mean(x * x, axis=-1, keepdims=True) + EPS) * g


def _dot(a, b):
    return jnp.dot(a, b, preferred_element_type=F32)


def _gelu_erf(x):
    return 0.5 * x * (1.0 + lax.erf(x * np.float32(math.sqrt(0.5))))


def _gelu_tanh(x):
    c = np.float32(math.sqrt(2.0 / math.pi))
    return x * (0.5 * (1.0 + jnp.tanh(c * (x + 0.044715 * (x * x * x)))))


def _lane_iota(shape):
    return lax.broadcasted_iota(jnp.int32, shape, len(shape) - 1)


def _rope(x, cos, sin_signed):
    w = x.shape[1]
    reps = w // LANES
    c = jnp.concatenate([cos] * reps, axis=1) if reps > 1 else cos
    s = jnp.concatenate([sin_signed] * reps, axis=1) if reps > 1 else sin_signed
    first_half = (_lane_iota(x.shape) % HEAD_DIM) < (HEAD_DIM // 2)
    swapped = jnp.where(first_half, pltpu.roll(x, w - HEAD_DIM // 2, 1), pltpu.roll(x, HEAD_DIM // 2, 1))
    return x * c + swapped * s


def _dup_head(kv, g):
    rolled = pltpu.roll(kv, HEAD_DIM, 1)
    low = _lane_iota(kv.shape) < HEAD_DIM
    return jnp.where(low, kv, rolled) if g == 0 else jnp.where(low, rolled, kv)


def _stack_group_queries(q, g):
    low = _lane_iota(q.shape[:2] + (LANES,)) < HEAD_DIM
    parts = []
    for i in range(GROUP):
        h = GROUP * g + i
        pair = q[:, :, (h // 2) * LANES:(h // 2 + 1) * LANES]
        parts.append(jnp.where(low if h % 2 == 0 else ~low, pair, 0.0))
    return jnp.concatenate(parts, axis=1)


def _unstack_group_outputs(o, rows):
    low = _lane_iota((o.shape[0], rows, LANES)) < HEAD_DIM
    pairs = []
    for p in range(GROUP // 2):
        even = o[:, (2 * p) * rows:(2 * p + 1) * rows]
        odd = o[:, (2 * p + 1) * rows:(2 * p + 2) * rows]
        pairs.append(jnp.where(low, even, odd))
    return jnp.concatenate(pairs, axis=2)


def _sink_softmax(s, scale, mask, sink):
    t = jnp.where(mask, s * np.float32(scale * LOG2E), NEG_INF)
    sink2 = sink * np.float32(LOG2E)
    m = jnp.maximum(jnp.max(t, axis=-1, keepdims=True), sink2)
    e = jnp.exp2(t - m)
    denom = jnp.sum(e, axis=-1, keepdims=True) + jnp.exp2(sink2 - m)
    return (e * (1.0 / denom)).astype(BF16)


def _softmax(s, scale):
    t = s * np.float32(scale * LOG2E)
    m = jnp.max(t, axis=-1, keepdims=True)
    e = jnp.exp2(t - m)
    return (e * (1.0 / jnp.sum(e, axis=-1, keepdims=True))).astype(BF16)


def _sigmoid(x):
    return 1.0 / (1.0 + jnp.exp2(x * np.float32(-LOG2E)))


def _sink_column(sinks_ref, g, rows):
    r = lax.broadcasted_iota(jnp.int32, (1, GROUP * rows, 1), 1)
    col = jnp.full((1, GROUP * rows, 1), sinks_ref[GROUP * g + GROUP - 1], F32)
    for i in range(GROUP - 2, -1, -1):
        col = jnp.where(r < (i + 1) * rows, sinks_ref[GROUP * g + i], col)
    return col


def _sub_tiles(schedule):
    return 1 + max(ord(tok[-1]) - ord('A') for tok in schedule.split())


def _mix_program(refs, sample, n_sub):
    if sample:
        (sinks_ref, x_ref, cos_ref, sin_ref, g_pre_ref, w_in_ref, ln_g_ref, ln_b_ref, wmix_ref, sgb_ref,
         mk_ref, mv_ref, w_o_ref, g_post_ref, ck_ref, cv_ref,
         x1_ref, wk_ref, wv_ref, sgv_ref, gate_ref) = refs
    else:
        (sinks_ref, x_ref, cos_ref, sin_ref, cos_s_ref, sin_s_ref, cos_b_ref, sin_b_ref,
         g_pre_ref, w_in_ref, ln_g_ref, ln_b_ref, wmix_ref, sgb_ref,
         mk_ref, mv_ref, w_o_ref, g_post_ref,
         x1_ref, wk_ref, wv_ref, kcar_ref, vcar_ref, gate_ref) = refs
    rows = x_ref.shape[0] // n_sub
    seqs = rows // SUBLANES
    step = pl.program_id(0)
    assert not (sample and n_sub > 1)

    def init():
        if not sample:
            @pl.when(step == 0)
            def _():
                kcar_ref[...] = jnp.zeros_like(kcar_ref)
                vcar_ref[...] = jnp.zeros_like(vcar_ref)

    st = [dict() for _ in range(n_sub)]
    chunk = wmix_ref.shape[1]
    mem_scale = MEM_HEAD_DIM ** -0.5

    def rows_of(t):
        return slice(t * rows, (t + 1) * rows)

    def head(t):
        s = st[t]
        s['x'] = x_ref[rows_of(t), :]
        s['h'] = _rms(s['x'], g_pre_ref[...]).astype(BF16)
        s['gate_pieces'] = [(n, c) for n in range(N_BRANCHES) for c in range(0, D_MODEL, GATE_CHUNK)]
        s['logit_pieces'] = []

    def gates(t, count):
        s = st[t]
        for _ in range(min(count, len(s['gate_pieces']))):
            n, c = s['gate_pieces'].pop(0)
            col = OFF_GATE + n * D_MODEL + c
            gate_ref[n, rows_of(t), c:c + GATE_CHUNK] = _dot(s['h'], w_in_ref[:, col:col + GATE_CHUNK])
            s['logit_pieces'].append((n, c))

    def gate_sigmoids(t):
        s = st[t]
        while s['logit_pieces']:
            n, c = s['logit_pieces'].pop(0)
            piece = (n, rows_of(t), slice(c, c + GATE_CHUNK))
            gate_ref[piece] = _sigmoid(gate_ref[piece])

    def in_proj(t):
        s = st[t]
        s['zqkv'] = _dot(s['h'], w_in_ref[:, OFF_Q:OFF_SGU])
        s['zsg'] = _dot(s['h'], w_in_ref[:, OFF_SGU:OFF_MQ])
        s['mq'] = _dot(s['h'], w_in_ref[:, OFF_MQ:OFF_GATE])

    def rope_stage(t):
        s = st[t]
        zqkv = s.pop('zqkv')
        if sample:
            cos, sin_signed = cos_ref[...], sin_ref[...]
        else:
            cb = cos_b_ref[pl.ds(step, 1), :]
            sb = sin_b_ref[pl.ds(step, 1), :]
            cos = cb * cos_ref[rows_of(t), :] - sb * sin_ref[rows_of(t), :]
            sin_signed = sb * cos_s_ref[rows_of(t), :] + cb * sin_s_ref[rows_of(t), :]
        qk = _rope(zqkv[:, :OFF_V], cos, sin_signed)
        q = qk[:, :ATTN_Q]
        k = qk[:, OFF_K:OFF_V]
        v = zqkv[:, OFF_V:OFF_SGU]
        if sample:
            past = ck_ref.shape[1]
            k3 = k.reshape(seqs, SUBLANES, ATTN_KV)
            v3 = v.reshape(seqs, SUBLANES, ATTN_KV)
            ck = ck_ref[...]
            cv = cv_ref[...]
            wk_ref[:, :past - SUBLANES, :] = ck[:, SUBLANES:, :]
            wk_ref[:, past - SUBLANES:, :] = k3
            wv_ref[:, :past - SUBLANES, :] = cv[:, SUBLANES:, :]
            wv_ref[:, past - SUBLANES:, :] = v3
            n_keys = past + 2 * SUBLANES
            pad = jnp.zeros((seqs, SUBLANES, ATTN_KV), F32)
            kk = jnp.concatenate([ck, k3, pad], axis=1).reshape(seqs * n_keys, ATTN_KV)
            vv = jnp.concatenate([cv, v3, pad], axis=1).reshape(seqs * n_keys, ATTN_KV)
            q3 = q.reshape(seqs, SUBLANES, ATTN_Q)
            s['q_rows'] = SUBLANES
            qi = lax.broadcasted_iota(jnp.int32, (1, GROUP * SUBLANES, n_keys), 1) % SUBLANES
            kj = lax.broadcasted_iota(jnp.int32, (1, GROUP * SUBLANES, n_keys), 2)
            s['mask'] = (kj > qi + (past - WINDOW)) & (kj <= qi + past)
            s['kds'] = [_dup_head(kk, g).astype(BF16).reshape(seqs, n_keys, LANES) for g in range(N_KV_HEADS)]
            s['vds'] = [_dup_head(vv, g).astype(BF16).reshape(seqs, n_keys, LANES) for g in range(N_KV_HEADS)]
        else:
            nb = rows // WINDOW
            k_prev = kcar_ref[...] if t == 0 else st[t - 1]['k_last']
            v_prev = vcar_ref[...] if t == 0 else st[t - 1]['v_last']
            s['k_last'] = k[rows - WINDOW:]
            s['v_last'] = v[rows - WINDOW:]
            if t == n_sub - 1:
                wk_ref[...] = s['k_last'].T
                wv_ref[...] = s['v_last'].T
                kcar_ref[...] = s['k_last']
                vcar_ref[...] = s['v_last']
            kk = jnp.concatenate([k_prev, k], axis=0)
            vv = jnp.concatenate([v_prev, v], axis=0)
            q3 = q.reshape(nb, WINDOW, ATTN_Q)
            s['q_rows'] = WINDOW
            shape = (nb, GROUP * WINDOW, 2 * WINDOW)
            qi = lax.broadcasted_iota(jnp.int32, shape, 1) % WINDOW
            kj = lax.broadcasted_iota(jnp.int32, shape, 2)
            s['mask'] = (kj > qi) & (kj <= qi + WINDOW)
            if t == 0:
                blk = lax.broadcasted_iota(jnp.int32, shape, 0)
                first_key = jnp.where((blk == 0) & (step == 0), WINDOW, 0)
                s['mask'] = s['mask'] & (kj >= first_key)
            s['kds'], s['vds'] = [], []
            for g in range(N_KV_HEADS):
                kd = _dup_head(kk, g).astype(BF16)
                vd = _dup_head(vv, g).astype(BF16)
                s['kds'].append(jnp.stack([kd[n * WINDOW:(n + 2) * WINDOW] for n in range(nb)], axis=0))
                s['vds'].append(jnp.stack([vd[n * WINDOW:(n + 2) * WINDOW] for n in range(nb)], axis=0))
        s['qss'] = [_stack_group_queries(q3, g).astype(BF16) for g in range(N_KV_HEADS)]
        s['sinks'] = [_sink_column(sinks_ref, g, s['q_rows']) for g in range(N_KV_HEADS)]

    def scores(t):
        s = st[t]
        qss, kds = s.pop('qss'), s.pop('kds')
        s['scores'] = [jnp.einsum('bqd,bkd->bqk', q, k, preferred_element_type=F32) for q, k in zip(qss, kds)]

    def gelu_ln(t, part=None):
        s = st[t]
        if part in (None, 0):
            s['u'] = _gelu_erf(s['zsg'][:, :SG_WIDTH])
        if part in (None, 1):
            gv = _gelu_erf(s['zsg'][:, SG_WIDTH:])
            gc = gv - jnp.mean(gv, axis=-1, keepdims=True)
            var = jnp.mean(gc * gc, axis=-1, keepdims=True)
            vn = gc * lax.rsqrt(var + EPS) * ln_g_ref[...] + ln_b_ref[...]
            if sample:
                sgv_ref[...] = vn
            s['vn_b'] = vn.astype(BF16)

    def softmax(t, g=None):
        s = st[t]
        for gi in range(N_KV_HEADS) if g is None else (g,):
            s.setdefault('probs', {})[gi] = _sink_softmax(s['scores'][gi], HEAD_DIM ** -0.5, s['mask'],
                                                          s['sinks'][gi])

    def values(t):
        attn_values(t)
        mem_scores(t)
        spatial_mix(t)

    def attn_values(t):
        s = st[t]
        probs, vds = s.pop('probs'), s.pop('vds')
        s['attn_outs'] = [jnp.einsum('bqk,bkd->bqd', probs[g], vds[g], preferred_element_type=F32)
                          for g in range(len(vds))]

    def mem_scores(t):
        s = st[t]
        mq = s.pop('mq')
        s['mem_scores'], s['mem_vals'] = [], []
        if sample:
            mq3 = mq.reshape(seqs, SUBLANES, MEM_Q)
            lane_head = _lane_iota(mq3.shape) // MEM_HEAD_DIM
            q_all = jnp.concatenate([jnp.where(lane_head == hd, mq3, 0.0) for hd in range(MEM_HEADS)],
                                    axis=1).astype(BF16)
            k_all = jnp.concatenate([mk_ref[:, pl.ds(hd, N_MEM, stride=MEM_HEADS), :] for hd in range(MEM_HEADS)],
                                    axis=2).astype(BF16)
            v_all = jnp.concatenate([mv_ref[:, pl.ds(hd, N_MEM, stride=MEM_HEADS), :] for hd in range(MEM_HEADS)],
                                    axis=2).astype(BF16)
            s['mem_scores'].append(jnp.einsum('bqd,bkd->bqk', q_all, k_all, preferred_element_type=F32))
            s['mem_vals'].append(v_all)
        else:
            for hd in range(MEM_HEADS):
                sl = slice(hd * MEM_HEAD_DIM, (hd + 1) * MEM_HEAD_DIM)
                s['mem_vals'].append(mv_ref[:, sl])
                s['mem_scores'].append(lax.dot_general(mq[:, sl].astype(BF16), mk_ref[:, sl],
                                                       (((1,), (1,)), ((), ())), preferred_element_type=F32))

    def spatial_mix(t):
        s = st[t]
        vn_b = s.pop('vn_b')
        t_idx = lax.broadcasted_iota(jnp.int32, (chunk, chunk), 0)
        s_idx = lax.broadcasted_iota(jnp.int32, (chunk, chunk), 1)
        keep = t_idx >= s_idx
        if sample:
            keep = keep & ((t_idx // SUBLANES) == (s_idx // SUBLANES))
        s['sg_cols'] = []
        for g in range(SG_GROUPS):
            wm = jnp.where(keep, wmix_ref[g], 0.0).astype(BF16)
            blocks = [_dot(wm, vn_b[c * chunk:(c + 1) * chunk, g * SG_GROUP_DIM:(g + 1) * SG_GROUP_DIM])
                      for c in range(rows // chunk)]
            s['sg_cols'].append(jnp.concatenate(blocks, axis=0) if len(blocks) > 1 else blocks[0])

    def branch_outs(t):
        mem_softmax(t)
        attn_out(t)
        spatial_out(t)

    def mem_softmax(t):
        s = st[t]
        s['mem_probs'] = [_softmax(sc, mem_scale) for sc in s.pop('mem_scores')]

    def attn_out(t):
        s = st[t]
        s['attn'] = jnp.concatenate([_unstack_group_outputs(o, s['q_rows']) for o in s.pop('attn_outs')],
                                    axis=2).reshape(rows, ATTN_Q).astype(BF16)

    def spatial_out(t):
        s = st[t]
        mixed_sg = jnp.concatenate(s.pop('sg_cols'), axis=1)
        bias = sgb_ref[...]
        if rows // chunk > 1:
            bias = jnp.concatenate([bias] * (rows // chunk), axis=0)
        s['sg'] = (s.pop('u') * (mixed_sg + bias)).astype(BF16)

    def mem_values(t):
        s = st[t]
        mem_probs, mem_vals = s.pop('mem_probs'), s.pop('mem_vals')
        if sample:
            o = jnp.einsum('bqk,bkd->bqd', mem_probs[0], mem_vals[0], preferred_element_type=F32)
            cols = [o[:, hd * SUBLANES:(hd + 1) * SUBLANES, hd * MEM_HEAD_DIM:(hd + 1) * MEM_HEAD_DIM]
                    .reshape(rows, MEM_HEAD_DIM) for hd in range(MEM_HEADS)]
        else:
            cols = [_dot(mem_probs[hd], mem_vals[hd]) for hd in range(MEM_HEADS)]
        s['memo'] = jnp.concatenate(cols, axis=1).astype(BF16)

    def merge(t):
        s = st[t]
        gates(t, len(s['gate_pieces']))
        gate_sigmoids(t)
        mixed = None
        for n, name in enumerate(('attn', 'sg', 'memo')):
            proj = _dot(s.pop(name), w_o_ref[n * ATTN_Q:(n + 1) * ATTN_Q, :])
            term = gate_ref[n, rows_of(t), :] * proj
            mixed = term if mixed is None else mixed + term
        x1_ref[rows_of(t), :] = s.pop('x') + _rms(mixed, g_post_ref[...])

    stages = {'head': head, 'inproj': in_proj, 'rope': rope_stage, 'scores': scores, 'gelu': gelu_ln,
              'softmax': softmax, 'values': values, 'outs': branch_outs, 'memv': mem_values, 'merge': merge,
              'sig': gate_sigmoids, 'pv': attn_values, 'memsc': mem_scores, 'sgmix': spatial_mix,
              'memsm': mem_softmax, 'attnout': attn_out, 'sgout': spatial_out}
    def run(tok):
        name, t = tok[:-1], ord(tok[-1]) - ord('A')
        if name.startswith('gates'):
            gates(t, int(name[len('gates'):]))
        elif name[-1].isdigit():
            stages[name[:-1]](t, int(name[-1]))
        else:
            stages[name](t)

    return init, run


def _mix_kernel_tiles(*refs, sample, schedule):
    init, run = _mix_program(refs, sample, _sub_tiles(schedule))
    init()
    for tok in schedule.split():
        run(tok)


def _ffn_program(refs, sample):
    if sample:
        (x_ref, g_pre_ref, w_up_ref, cw_ref, cb_ref, w_down_ref, g_post_ref, st_ref,
         y_ref, nc_ref, *up_s) = refs
    else:
        (x_ref, g_pre_ref, w_up_ref, cw_ref, cb_ref, w_down_ref, g_post_ref,
         y_ref, nc_ref, *up_s) = refs
    rows = x_ref.shape[0]
    step = pl.program_id(0)
    st = {}

    def init():
        if not sample:
            @pl.when(step == 0)
            def _():
                for slab in up_s:
                    slab[:SUBLANES, :] = jnp.zeros((SUBLANES, LANES), F32)

    def head():
        st['x'] = x_ref[...]
        st['h'] = _rms(st['x'], g_pre_ref[...]).astype(BF16)
        st['up_pieces'] = [(half * D_FF + n0, min(FFN_UP_CHUNK, D_FF - n0))
                           for n0 in range(0, D_FF, FFN_UP_CHUNK) for half in range(2)]
        st['chunks'] = [(c0, min(FFN_CHUNK, D_FF - c0)) for c0 in range(0, D_FF, FFN_CHUNK)]
        st['f'] = None

    hist = CONV_WIDTH - 1
    seqs = rows // SUBLANES

    def up_cols(col0, width):
        up = jnp.dot(st['h'], w_up_ref[:, col0:col0 + width], preferred_element_type=F32)
        for t in range(width // LANES):
            s = col0 // LANES + t
            blk = up[:, t * LANES:(t + 1) * LANES]
            if sample:
                up_s[s][:, SUBLANES:, :] = blk.reshape(seqs, SUBLANES, LANES)
                up_s[s][:, SUBLANES - hist:SUBLANES, :] = st_ref[:, :, s * LANES:(s + 1) * LANES]
            else:
                up_s[s][SUBLANES:, :] = blk

    def conv_slab(s):
        cols = slice(s * LANES, (s + 1) * LANES)
        if sample:
            taps = [up_s[s][:, SUBLANES - hist + j:2 * SUBLANES - hist + j, :] for j in range(CONV_WIDTH)]
            nc_ref[:, :, cols] = up_s[s][:, 2 * SUBLANES - hist:, :]
        else:
            taps = [up_s[s][SUBLANES - hist + j:SUBLANES - hist + j + rows, :] for j in range(CONV_WIDTH)]
            nc_ref[:, cols] = up_s[s][SUBLANES + rows - hist:, :]
            up_s[s][:SUBLANES, :] = up_s[s][rows:, :]
        c = cb_ref[:, cols]
        for j in range(CONV_WIDTH):
            c = c + taps[j] * cw_ref[j:j + 1, cols]
        return c.reshape(rows, LANES)

    def up(count):
        for _ in range(min(count, len(st['up_pieces']))):
            up_cols(*st['up_pieces'].pop(0))

    def conv_down(count):
        for _ in range(min(count, len(st['chunks']))):
            c0, width = st['chunks'].pop(0)
            acts = []
            for t in range(width // LANES):
                gate = conv_slab(c0 // LANES + t)
                val = conv_slab((D_FF + c0) // LANES + t)
                acts.append((_gelu_tanh(gate) * val).astype(BF16))
            act = jnp.concatenate(acts, axis=1)
            part = jnp.dot(act, w_down_ref[c0:c0 + width, :], preferred_element_type=F32)
            st['f'] = part if st['f'] is None else st['f'] + part

    def tail():
        up(len(st['up_pieces']))
        conv_down(len(st['chunks']))
        y_ref[...] = st.pop('x') + _rms(st.pop('f'), g_post_ref[...])

    def run(tok):
        if tok == 'head':
            head()
        elif tok == 'tail':
            tail()
        elif tok.startswith('up'):
            up(int(tok[2:]))
        else:
            assert tok.startswith('vd'), tok
            conv_down(int(tok[2:]))

    return init, run


def _ffn_kernel(*refs, sample):
    init, run = _ffn_program(refs, sample)
    init()
    for tok in FFN_SCHEDULE.split():
        run(tok)


def _memkv_kernel(mem_ref, g_ref, w_ref, k_ref, v_ref, kb_ref, vb_ref):
    h = _rms(mem_ref[...], g_ref[...]).astype(BF16)
    kv = _dot(h, w_ref[...].astype(BF16))
    kb_ref[...] = kv[:, :MEM_Q].astype(BF16)
    vb_ref[...] = kv[:, MEM_Q:].astype(BF16)
    for hd in range(MEM_HEADS):
        k_ref[pl.ds(hd, N_MEM, stride=MEM_HEADS), :] = kv[:, hd * MEM_HEAD_DIM:(hd + 1) * MEM_HEAD_DIM]
        v_ref[pl.ds(hd, N_MEM, stride=MEM_HEADS), :] = kv[:, MEM_Q + hd * MEM_HEAD_DIM:
                                                          MEM_Q + (hd + 1) * MEM_HEAD_DIM]


def _resident():
    return pl.BlockSpec(memory_space=pltpu.VMEM)


def _row_spec(rows, width):
    return pl.BlockSpec((rows, width), lambda i: (i, 0))


def _const_spec(shape):
    nd = len(shape)
    return pl.BlockSpec(shape, lambda i: (0,) * nd)


def _compiler_params():
    return pltpu.CompilerParams(dimension_semantics=("arbitrary",), vmem_limit_bytes=VMEM_LIMIT_BYTES)


def _rope_tables(pos):
    half = HEAD_DIM // 2
    inv_freq = np.float64(ROPE_THETA) ** (-np.arange(half, dtype=np.float64) / half)
    ang = pos.astype(np.float64)[:, None] * inv_freq[None, :]
    cos = np.concatenate([np.cos(ang)] * (LANES // half), axis=1).astype(np.float32)
    sin = np.concatenate([np.sin(ang)] * (LANES // half), axis=1).astype(np.float32)
    sign = np.concatenate([-np.ones((1, half), np.float32), np.ones((1, half), np.float32)]
                          * (LANES // HEAD_DIM), axis=1)
    return cos, sin, cos * sign, sin * sign


def _mix_call(x2d, start, seq_len, lp, mem_k, mem_v, caches):
    n_rows = x2d.shape[0]
    sample = caches is not None
    rows = SAMPLE_SEQS * seq_len if sample else PROMPT_MIX_ROWS
    schedule = MIX_SCHEDULE_ONE if sample else MIX_SCHEDULE_TWO
    steps = n_rows // rows
    if sample:
        pos = start + (np.arange(rows, dtype=np.int32) % seq_len)
        cos, _, _, sin_signed = _rope_tables(pos)
        rope_args = [cos, sin_signed]
        rope_specs = [_const_spec((rows, LANES))] * 2
        chunk = rows
        wmix = jnp.tile(lp['sg_w'][:, :seq_len, :seq_len], (1, rows // seq_len, rows // seq_len))
        sgb = jnp.repeat(jnp.tile(lp['sg_b'][:, :seq_len], (1, rows // seq_len)).T, SG_GROUP_DIM, axis=1)
    else:
        in_step = _rope_tables(np.arange(rows, dtype=np.int32))
        cos_b, sin_b, _, _ = _rope_tables(start + rows * np.arange(steps, dtype=np.int32))
        rope_args = list(in_step) + [cos_b, sin_b]
        rope_specs = [_const_spec((rows, LANES))] * 4 + [_const_spec((steps, LANES))] * 2
        chunk = CHUNK
        wmix = lp['sg_w'][:, :chunk, :chunk]
        sgb = jnp.repeat(lp['sg_b'][:, :chunk].T, SG_GROUP_DIM, axis=1)

    in_specs = [
        pl.BlockSpec(memory_space=pltpu.SMEM),
        _row_spec(rows, D_MODEL),
        *rope_specs,
        _const_spec((1, D_MODEL)),
        _resident(),
        _const_spec((1, SG_WIDTH)), _const_spec((1, SG_WIDTH)),
        _const_spec((SG_GROUPS, chunk, chunk)),
        _const_spec((chunk, SG_WIDTH)),
    ]
    args = [lp['sinks'], x2d, *rope_args, lp['pre_mix_g'].reshape(1, D_MODEL), lp['w_in'],
            lp['sg_ln_g'].reshape(1, SG_WIDTH), lp['sg_ln_b'].reshape(1, SG_WIDTH), wmix, sgb]
    if sample:
        seqs = SAMPLE_SEQS
        in_specs += [pl.BlockSpec((seqs, N_MEM * MEM_HEADS, MEM_HEAD_DIM), lambda i: (i, 0, 0))] * 2
    else:
        in_specs += [_const_spec((N_MEM, MEM_Q))] * 2
    args += [mem_k, mem_v]
    in_specs += [_resident(), _const_spec((1, D_MODEL))]
    args += [lp['w_o'], lp['post_mix_g'].reshape(1, D_MODEL)]

    out_shape = [jax.ShapeDtypeStruct((n_rows, D_MODEL), F32)]
    out_specs = [_row_spec(rows, D_MODEL)]
    scratch = []
    if sample:
        win_k, win_v = caches
        n_seq, past = win_k.shape[0], win_k.shape[1]
        cache_spec = pl.BlockSpec((SAMPLE_SEQS, past, ATTN_KV), lambda i: (i, 0, 0))
        in_specs += [cache_spec, cache_spec]
        args += [win_k, win_v]
        out_shape += [jax.ShapeDtypeStruct((n_seq, past, ATTN_KV), F32)] * 2
        out_specs += [cache_spec, cache_spec]
        out_shape += [jax.ShapeDtypeStruct((n_rows, SG_WIDTH), F32)]
        out_specs += [_row_spec(rows, SG_WIDTH)]
    else:
        out_shape += [jax.ShapeDtypeStruct((WINDOW, ATTN_KV), F32)] * 2
        out_specs += [_const_spec((WINDOW, ATTN_KV))] * 2
        scratch = [pltpu.VMEM((WINDOW, ATTN_KV), F32), pltpu.VMEM((WINDOW, ATTN_KV), F32)]
    scratch += [pltpu.VMEM((N_BRANCHES, rows, D_MODEL), F32)]

    return pl.pallas_call(
        functools.partial(_mix_kernel_tiles, sample=sample, schedule=schedule),
        grid=(steps,),
        in_specs=in_specs,
        out_specs=out_specs,
        out_shape=out_shape,
        scratch_shapes=scratch,
        compiler_params=_compiler_params(),
        name="mix_sample" if sample else "mix_prompt",
    )(*args)


def _ffn_call(x2d, seq_len, lp, conv_state):
    n_rows = x2d.shape[0]
    sample = conv_state is not None
    rows = SAMPLE_FFN_SEQS * seq_len if sample else PROMPT_ROWS
    steps = n_rows // rows
    in_specs = [
        _row_spec(rows, D_MODEL),
        _const_spec((1, D_MODEL)),
        _resident(),
        _const_spec((CONV_WIDTH, 2 * D_FF)),
        _const_spec((1, 2 * D_FF)),
        _resident(),
        _const_spec((1, D_MODEL)),
    ]
    args = [x2d, lp['pre_ffn_g'].reshape(1, D_MODEL), lp['w_up'], lp['conv_w'],
            lp['conv_b'].reshape(1, 2 * D_FF), lp['w_down'], lp['post_ffn_g'].reshape(1, D_MODEL)]
    out_shape = [jax.ShapeDtypeStruct((n_rows, D_MODEL), F32)]
    out_specs = [_row_spec(rows, D_MODEL)]
    n_slabs = 2 * D_FF // LANES
    if sample:
        n_seq = conv_state.shape[0]
        st_spec = pl.BlockSpec((SAMPLE_FFN_SEQS, CONV_WIDTH - 1, 2 * D_FF), lambda i: (i, 0, 0))
        in_specs += [st_spec]
        args += [conv_state]
        out_shape += [jax.ShapeDtypeStruct((n_seq, CONV_WIDTH - 1, 2 * D_FF), F32)]
        out_specs += [st_spec]
        scratch = [pltpu.VMEM((SAMPLE_FFN_SEQS, 2 * SUBLANES, LANES), F32)] * n_slabs
    else:
        out_shape += [jax.ShapeDtypeStruct((CONV_WIDTH - 1, 2 * D_FF), F32)]
        out_specs += [_const_spec((CONV_WIDTH - 1, 2 * D_FF))]
        scratch = [pltpu.VMEM((SUBLANES + rows, LANES), F32)] * n_slabs
    return pl.pallas_call(
        functools.partial(_ffn_kernel, sample=sample),
        grid=(steps,),
        in_specs=in_specs,
        out_specs=out_specs,
        out_shape=out_shape,
        scratch_shapes=scratch,
        compiler_params=_compiler_params(),
        name="ffn_sample" if sample else "ffn_prompt",
    )(*args)


def _memkv_call(mem2d, g, w_f32):
    return pl.pallas_call(
        _memkv_kernel,
        out_shape=[jax.ShapeDtypeStruct((mem2d.shape[0] * MEM_HEADS, MEM_HEAD_DIM), F32)] * 2
        + [jax.ShapeDtypeStruct((mem2d.shape[0], MEM_Q), BF16)] * 2,
        compiler_params=pltpu.CompilerParams(vmem_limit_bytes=VMEM_LIMIT_BYTES),
        name="mem_kv",
    )(mem2d, g.reshape(1, D_MODEL), w_f32)


def kernel(x_prompt, x_sample, cache_win_k, cache_win_v, cache_mem_k, cache_mem_v, state_conv, mem_prompt,
           pre_mix_g, w_in, attn_sinks, sg_ln_g, sg_ln_b, sg_w, sg_b, mem_norm_g, w_mem_kv, w_o,
           post_mix_g, pre_ffn_g, w_up, conv_w, conv_b, w_down, post_ffn_g):
    depth = w_in.shape[0]
    batch, seq, _ = x_prompt.shape
    dec_batch, dec_seq, _ = x_sample.shape
    past_len = PAST_LEN
    assert batch == 1 and depth == 1

    yp = x_prompt.reshape(batch * seq, D_MODEL)
    ys = x_sample.reshape(dec_batch * dec_seq, D_MODEL)
    outs = {name: [] for name in ('wk_p', 'wv_p', 'mk_p', 'mv_p', 'cv_p', 'wk_s', 'wv_s', 'sgv_s', 'cv_s')}
    for l in range(depth):
        lp = {
            'sinks': attn_sinks[l], 'pre_mix_g': pre_mix_g[l], 'w_in': w_in[l].astype(BF16),
            'sg_ln_g': sg_ln_g[l], 'sg_ln_b': sg_ln_b[l], 'sg_w': sg_w[l], 'sg_b': sg_b[l],
            'w_o': w_o[l].reshape(N_BRANCHES * ATTN_Q, D_MODEL).astype(BF16), 'post_mix_g': post_mix_g[l],
            'pre_ffn_g': pre_ffn_g[l], 'w_up': w_up[l].astype(BF16), 'conv_w': conv_w[l], 'conv_b': conv_b[l],
            'w_down': w_down[l].astype(BF16), 'post_ffn_g': post_ffn_g[l],
        }
        mem_k, mem_v, mem_k_b, mem_v_b = _memkv_call(mem_prompt.reshape(batch * N_MEM, D_MODEL), mem_norm_g[l],
                                                     w_mem_kv[l])
        outs['mk_p'].append(mem_k.reshape(batch, N_MEM, MEM_HEADS, MEM_HEAD_DIM))
        outs['mv_p'].append(mem_v.reshape(batch, N_MEM, MEM_HEADS, MEM_HEAD_DIM))

        x1, wk, wv = _mix_call(yp, 0, seq, lp, mem_k_b, mem_v_b, None)
        yp, nc = _ffn_call(x1, seq, lp, None)
        outs['wk_p'].append(wk.reshape(batch, N_KV_HEADS, HEAD_DIM, WINDOW).transpose(0, 3, 1, 2))
        outs['wv_p'].append(wv.reshape(batch, N_KV_HEADS, HEAD_DIM, WINDOW).transpose(0, 3, 1, 2))
        outs['cv_p'].append(nc.reshape(batch, CONV_WIDTH - 1, 2 * D_FF))

        past = cache_win_k.shape[2]
        x1s, wks, wvs, sgv = _mix_call(
            ys, past_len, dec_seq, lp,
            cache_mem_k[l].reshape(dec_batch, N_MEM * MEM_HEADS, MEM_HEAD_DIM),
            cache_mem_v[l].reshape(dec_batch, N_MEM * MEM_HEADS, MEM_HEAD_DIM),
            (cache_win_k[l].reshape(dec_batch, past, ATTN_KV), cache_win_v[l].reshape(dec_batch, past, ATTN_KV)))
        ys, ncs = _ffn_call(x1s, dec_seq, lp, state_conv[l])
        outs['wk_s'].append(wks.reshape(dec_batch, past, N_KV_HEADS, HEAD_DIM))
        outs['wv_s'].append(wvs.reshape(dec_batch, past, N_KV_HEADS, HEAD_DIM))
        outs['sgv_s'].append(sgv.reshape(dec_batch, dec_seq, SG_WIDTH))
        outs['cv_s'].append(ncs)

    return (yp.reshape(batch, seq, D_MODEL), ys.reshape(dec_batch, dec_seq, D_MODEL),
            jnp.stack(outs['wk_p']), jnp.stack(outs['wv_p']), jnp.stack(outs['mk_p']), jnp.stack(outs['mv_p']),
            jnp.stack(outs['cv_p']),
            jnp.stack(outs['wk_s']), jnp.stack(outs['wv_s']), jnp.stack(outs['sgv_s']), jnp.stack(outs['cv_s']))
```

```python
import functools
import math

import jax
import jax.numpy as jnp
import numpy as np
from jax import lax
from jax.experimental import pallas as pl
from jax.experimental.pallas import tpu as pltpu

F32 = jnp.float32
BF16 = jnp.bfloat16

D_MODEL = 1024
N_Q_HEADS = 8
N_KV_HEADS = 2
GROUP = N_Q_HEADS // N_KV_HEADS
HEAD_DIM = 64
ATTN_Q = N_Q_HEADS * HEAD_DIM
ATTN_KV = N_KV_HEADS * HEAD_DIM
WINDOW = 128
ROPE_THETA = 10000.0
CHUNK = 128
SG_GROUPS = 4
SG_GROUP_DIM = 128
SG_WIDTH = SG_GROUPS * SG_GROUP_DIM
N_MEM = 256
MEM_HEADS = 4
MEM_HEAD_DIM = 128
MEM_Q = MEM_HEADS * MEM_HEAD_DIM
N_BRANCHES = 3
D_FF = 2816
CONV_WIDTH = 3
EPS = 1e-6
NEG_INF = -1e30
LOG2E = math.log2(math.e)
PAST_LEN = 16384

OFF_Q = 0
OFF_K = ATTN_Q
OFF_V = OFF_K + ATTN_KV
OFF_SGU = OFF_V + ATTN_KV
OFF_SGV = OFF_SGU + SG_WIDTH
OFF_MQ = OFF_SGV + SG_WIDTH
OFF_GATE = OFF_MQ + MEM_Q
IN_WIDTH = OFF_GATE + N_BRANCHES * D_MODEL

LANES = 128
SUBLANES = 8
VMEM_LIMIT_BYTES = 56 * 1024 * 1024

PROMPT_ROWS = 512
PROMPT_MIX_ROWS = 512
MIX_SCHEDULE_ONE = ("headA inprojA ropeA scoresA gates4A geluA softmaxA valuesA gates4A outsA memvA mergeA")


MIX_SCHEDULE_TWO = (
    "headA inprojA memscA ropeA scoresA headB inprojB geluA gates1A sgmixA gates1A memsmA gates1A memvA gates1A "
    "softmaxA gates1A pvA gates1A ropeB memscB scoresB sgoutA gates1A attnoutA sigA geluB gates1A sgmixB gates1A "
    "memsmB gates1A memvB gates1A softmaxB gates1A gates1B sigA pvB gates2B mergeA gates2B sgoutB gates2B "
    "attnoutB gates5B sigB mergeB")
SAMPLE_SEQS = 8
SAMPLE_FFN_SEQS = 32
FFN_CHUNK = 256
FFN_UP_CHUNK = 256
FFN_SCHEDULE = "head up22 vd11 tail"
GATE_CHUNK = 256


def _rms(x, g):
    return x * lax.rsqrt(jnp.mean(x * x, axis=-1, keepdims=True) + EPS) * g


def _dot(a, b):
    return jnp.dot(a, b, preferred_element_type=F32)


def _gelu_erf(x):
    return 0.5 * x * (1.0 + lax.erf(x * np.float32(math.sqrt(0.5))))


def _gelu_tanh(x):
    c = np.float32(math.sqrt(2.0 / math.pi))
    return x * (0.5 * (1.0 + jnp.tanh(c * (x + 0.044715 * (x * x * x)))))


def _lane_iota(shape):
    return lax.broadcasted_iota(jnp.int32, shape, len(shape) - 1)


def _rope(x, cos, sin_signed):
    w = x.shape[1]
    reps = w // LANES
    c = jnp.concatenate([cos] * reps, axis=1) if reps > 1 else cos
    s = jnp.concatenate([sin_signed] * reps, axis=1) if reps > 1 else sin_signed
    first_half = (_lane_iota(x.shape) % HEAD_DIM) < (HEAD_DIM // 2)
    swapped = jnp.where(first_half, pltpu.roll(x, w - HEAD_DIM // 2, 1), pltpu.roll(x, HEAD_DIM // 2, 1))
    return x * c + swapped * s


def _dup_head(kv, g):
    rolled = pltpu.roll(kv, HEAD_DIM, 1)
    low = _lane_iota(kv.shape) < HEAD_DIM
    return jnp.where(low, kv, rolled) if g == 0 else jnp.where(low, rolled, kv)


def _stack_group_queries(q, g):
    low = _lane_iota(q.shape[:2] + (LANES,)) < HEAD_DIM
    parts = []
    for i in range(GROUP):
        h = GROUP * g + i
        pair = q[:, :, (h // 2) * LANES:(h // 2 + 1) * LANES]
        parts.append(jnp.where(low if h % 2 == 0 else ~low, pair, 0.0))
    return jnp.concatenate(parts, axis=1)


def _unstack_group_outputs(o, rows):
    low = _lane_iota((o.shape[0], rows, LANES)) < HEAD_DIM
    pairs = []
    for p in range(GROUP // 2):
        even = o[:, (2 * p) * rows:(2 * p + 1) * rows]
        odd = o[:, (2 * p + 1) * rows:(2 * p + 2) * rows]
        pairs.append(jnp.where(low, even, odd))
    return jnp.concatenate(pairs, axis=2)


def _sink_softmax(s, scale, mask, sink):
    t = jnp.where(mask, s * np.float32(scale * LOG2E), NEG_INF)
    sink2 = sink * np.float32(LOG2E)
    m = jnp.maximum(jnp.max(t, axis=-1, keepdims=True), sink2)
    e = jnp.exp2(t - m)
    denom = jnp.sum(e, axis=-1, keepdims=True) + jnp.exp2(sink2 - m)
    return (e * (1.0 / denom)).astype(BF16)


def _softmax(s, scale):
    t = s * np.float32(scale * LOG2E)
    m = jnp.max(t, axis=-1, keepdims=True)
    e = jnp.exp2(t - m)
    return (e * (1.0 / jnp.sum(e, axis=-1, keepdims=True))).astype(BF16)


def _sigmoid(x):
    return 1.0 / (1.0 + jnp.exp2(x * np.float32(-LOG2E)))


def _sink_column(sinks_ref, g, rows):
    r = lax.broadcasted_iota(jnp.int32, (1, GROUP * rows, 1), 1)
    col = jnp.full((1, GROUP * rows, 1), sinks_ref[GROUP * g + GROUP - 1], F32)
    for i in range(GROUP - 2, -1, -1):
        col = jnp.where(r < (i + 1) * rows, sinks_ref[GROUP * g + i], col)
    return col


def _sub_tiles(schedule):
    return 1 + max(ord(tok[-1]) - ord('A') for tok in schedule.split())


def _mix_program(refs, sample, n_sub):
    if sample:
        (sinks_ref, x_ref, cos_ref, sin_ref, g_pre_ref, w_in_ref, ln_g_ref, ln_b_ref, wmix_ref, sgb_ref,
         mk_ref, mv_ref, w_o_ref, g_post_ref, ck_ref, cv_ref,
         x1_ref, wk_ref, wv_ref, sgv_ref, gate_ref) = refs
    else:
        (sinks_ref, x_ref, cos_ref, sin_ref, cos_s_ref, sin_s_ref, cos_b_ref, sin_b_ref,
         g_pre_ref, w_in_ref, ln_g_ref, ln_b_ref, wmix_ref, sgb_ref,
         mk_ref, mv_ref, w_o_ref, g_post_ref,
         x1_ref, wk_ref, wv_ref, kcar_ref, vcar_ref, gate_ref) = refs
    rows = x_ref.shape[0] // n_sub
    seqs = rows // SUBLANES
    step = pl.program_id(0)
    assert not (sample and n_sub > 1)

    def init():
        if not sample:
            @pl.when(step == 0)
            def _():
                kcar_ref[...] = jnp.zeros_like(kcar_ref)
                vcar_ref[...] = jnp.zeros_like(vcar_ref)

    st = [dict() for _ in range(n_sub)]
    chunk = wmix_ref.shape[1]
    mem_scale = MEM_HEAD_DIM ** -0.5

    def rows_of(t):
        return slice(t * rows, (t + 1) * rows)

    def head(t):
        s = st[t]
        s['x'] = x_ref[rows_of(t), :]
        s['h'] = _rms(s['x'], g_pre_ref[...]).astype(BF16)
        s['gate_pieces'] = [(n, c) for n in range(N_BRANCHES) for c in range(0, D_MODEL, GATE_CHUNK)]
        s['logit_pieces'] = []

    def gates(t, count):
        s = st[t]
        for _ in range(min(count, len(s['gate_pieces']))):
            n, c = s['gate_pieces'].pop(0)
            col = OFF_GATE + n * D_MODEL + c
            gate_ref[n, rows_of(t), c:c + GATE_CHUNK] = _dot(s['h'], w_in_ref[:, col:col + GATE_CHUNK])
            s['logit_pieces'].append((n, c))

    def gate_sigmoids(t):
        s = st[t]
        while s['logit_pieces']:
            n, c = s['logit_pieces'].pop(0)
            piece = (n, rows_of(t), slice(c, c + GATE_CHUNK))
            gate_ref[piece] = _sigmoid(gate_ref[piece])

    def in_proj(t):
        s = st[t]
        s['zqkv'] = _dot(s['h'], w_in_ref[:, OFF_Q:OFF_SGU])
        s['zsg'] = _dot(s['h'], w_in_ref[:, OFF_SGU:OFF_MQ])
        s['mq'] = _dot(s['h'], w_in_ref[:, OFF_MQ:OFF_GATE])

    def rope_stage(t):
        s = st[t]
        zqkv = s.pop('zqkv')
        if sample:
            cos, sin_signed = cos_ref[...], sin_ref[...]
        else:
            cb = cos_b_ref[pl.ds(step, 1), :]
            sb = sin_b_ref[pl.ds(step, 1), :]
            cos = cb * cos_ref[rows_of(t), :] - sb * sin_ref[rows_of(t), :]
            sin_signed = sb * cos_s_ref[rows_of(t), :] + cb * sin_s_ref[rows_of(t), :]
        qk = _rope(zqkv[:, :OFF_V], cos, sin_signed)
        q = qk[:, :ATTN_Q]
        k = qk[:, OFF_K:OFF_V]
        v = zqkv[:, OFF_V:OFF_SGU]
        if sample:
            past = ck_ref.shape[1]
            k3 = k.reshape(seqs, SUBLANES, ATTN_KV)
            v3 = v.reshape(seqs, SUBLANES, ATTN_KV)
            ck = ck_ref[...]
            cv = cv_ref[...]
            wk_ref[:, :past - SUBLANES, :] = ck[:, SUBLANES:, :]
            wk_ref[:, past - SUBLANES:, :] = k3
            wv_ref[:, :past - SUBLANES, :] = cv[:, SUBLANES:, :]
            wv_ref[:, past - SUBLANES:, :] = v3
            n_keys = past + 2 * SUBLANES
            pad = jnp.zeros((seqs, SUBLANES, ATTN_KV), F32)
            kk = jnp.concatenate([ck, k3, pad], axis=1).reshape(seqs * n_keys, ATTN_KV)
            vv = jnp.concatenate([cv, v3, pad], axis=1).reshape(seqs * n_keys, ATTN_KV)
            q3 = q.reshape(seqs, SUBLANES, ATTN_Q)
            s['q_rows'] = SUBLANES
            qi = lax.broadcasted_iota(jnp.int32, (1, GROUP * SUBLANES, n_keys), 1) % SUBLANES
            kj = lax.broadcasted_iota(jnp.int32, (1, GROUP * SUBLANES, n_keys), 2)
            s['mask'] = (kj > qi + (past - WINDOW)) & (kj <= qi + past)
            s['kds'] = [_dup_head(kk, g).astype(BF16).reshape(seqs, n_keys, LANES) for g in range(N_KV_HEADS)]
            s['vds'] = [_dup_head(vv, g).astype(BF16).reshape(seqs, n_keys, LANES) for g in range(N_KV_HEADS)]
        else:
            nb = rows // WINDOW
            k_prev = kcar_ref[...] if t == 0 else st[t - 1]['k_last']
            v_prev = vcar_ref[...] if t == 0 else st[t - 1]['v_last']
            s['k_last'] = k[rows - WINDOW:]
            s['v_last'] = v[rows - WINDOW:]
            if t == n_sub - 1:
                wk_ref[...] = s['k_last'].T
                wv_ref[...] = s['v_last'].T
                kcar_ref[...] = s['k_last']
                vcar_ref[...] = s['v_last']
            kk = jnp.concatenate([k_prev, k], axis=0)
            vv = jnp.concatenate([v_prev, v], axis=0)
            q3 = q.reshape(nb, WINDOW, ATTN_Q)
            s['q_rows'] = WINDOW
            shape = (nb, GROUP * WINDOW, 2 * WINDOW)
            qi = lax.broadcasted_iota(jnp.int32, shape, 1) % WINDOW
            kj = lax.broadcasted_iota(jnp.int32, shape, 2)
            s['mask'] = (kj > qi) & (kj <= qi + WINDOW)
            if t == 0:
                blk = lax.broadcasted_iota(jnp.int32, shape, 0)
                first_key = jnp.where((blk == 0) & (step == 0), WINDOW, 0)
                s['mask'] = s['mask'] & (kj >= first_key)
            s['kds'], s['vds'] = [], []
            for g in range(N_KV_HEADS):
                kd = _dup_head(kk, g).astype(BF16)
                vd = _dup_head(vv, g).astype(BF16)
                s['kds'].append(jnp.stack([kd[n * WINDOW:(n + 2) * WINDOW] for n in range(nb)], axis=0))
                s['vds'].append(jnp.stack([vd[n * WINDOW:(n + 2) * WINDOW] for n in range(nb)], axis=0))
        s['qss'] = [_stack_group_queries(q3, g).astype(BF16) for g in range(N_KV_HEADS)]
        s['sinks'] = [_sink_column(sinks_ref, g, s['q_rows']) for g in range(N_KV_HEADS)]

    def scores(t):
        s = st[t]
        qss, kds = s.pop('qss'), s.pop('kds')
        s['scores'] = [jnp.einsum('bqd,bkd->bqk', q, k, preferred_element_type=F32) for q, k in zip(qss, kds)]

    def gelu_ln(t, part=None):
        s = st[t]
        if part in (None, 0):
            s['u'] = _gelu_erf(s['zsg'][:, :SG_WIDTH])
        if part in (None, 1):
            gv = _gelu_erf(s['zsg'][:, SG_WIDTH:])
            gc = gv - jnp.mean(gv, axis=-1, keepdims=True)
            var = jnp.mean(gc * gc, axis=-1, keepdims=True)
            vn = gc * lax.rsqrt(var + EPS) * ln_g_ref[...] + ln_b_ref[...]
            if sample:
                sgv_ref[...] = vn
            s['vn_b'] = vn.astype(BF16)

    def softmax(t, g=None):
        s = st[t]
        for gi in range(N_KV_HEADS) if g is None else (g,):
            s.setdefault('probs', {})[gi] = _sink_softmax(s['scores'][gi], HEAD_DIM ** -0.5, s['mask'],
                                                          s['sinks'][gi])

    def values(t):
        attn_values(t)
        mem_scores(t)
        spatial_mix(t)

    def attn_values(t):
        s = st[t]
        probs, vds = s.pop('probs'), s.pop('vds')
        s['attn_outs'] = [jnp.einsum('bqk,bkd->bqd', probs[g], vds[g], preferred_element_type=F32)
                          for g in range(len(vds))]

    def mem_scores(t):
        s = st[t]
        mq = s.pop('mq')
        s['mem_scores'], s['mem_vals'] = [], []
        if sample:
            mq3 = mq.reshape(seqs, SUBLANES, MEM_Q)
            lane_head = _lane_iota(mq3.shape) // MEM_HEAD_DIM
            q_all = jnp.concatenate([jnp.where(lane_head == hd, mq3, 0.0) for hd in range(MEM_HEADS)],
                                    axis=1).astype(BF16)
            k_all = jnp.concatenate([mk_ref[:, pl.ds(hd, N_MEM, stride=MEM_HEADS), :] for hd in range(MEM_HEADS)],
                                    axis=2).astype(BF16)
            v_all = jnp.concatenate([mv_ref[:, pl.ds(hd, N_MEM, stride=MEM_HEADS), :] for hd in range(MEM_HEADS)],
                                    axis=2).astype(BF16)
            s['mem_scores'].append(jnp.einsum('bqd,bkd->bqk', q_all, k_all, preferred_element_type=F32))
            s['mem_vals'].append(v_all)
        else:
            for hd in range(MEM_HEADS):
                sl = slice(hd * MEM_HEAD_DIM, (hd + 1) * MEM_HEAD_DIM)
                s['mem_vals'].append(mv_ref[:, sl])
                s['mem_scores'].append(lax.dot_general(mq[:, sl].astype(BF16), mk_ref[:, sl],
                                                       (((1,), (1,)), ((), ())), preferred_element_type=F32))

    def spatial_mix(t):
        s = st[t]
        vn_b = s.pop('vn_b')
        t_idx = lax.broadcasted_iota(jnp.int32, (chunk, chunk), 0)
        s_idx = lax.broadcasted_iota(jnp.int32, (chunk, chunk), 1)
        keep = t_idx >= s_idx
        if sample:
            keep = keep & ((t_idx // SUBLANES) == (s_idx // SUBLANES))
        s['sg_cols'] = []
        for g in range(SG_GROUPS):
            wm = jnp.where(keep, wmix_ref[g], 0.0).astype(BF16)
            blocks = [_dot(wm, vn_b[c * chunk:(c + 1) * chunk, g * SG_GROUP_DIM:(g + 1) * SG_GROUP_DIM])
                      for c in range(rows // chunk)]
            s['sg_cols'].append(jnp.concatenate(blocks, axis=0) if len(blocks) > 1 else blocks[0])

    def branch_outs(t):
        mem_softmax(t)
        attn_out(t)
        spatial_out(t)

    def mem_softmax(t):
        s = st[t]
        s['mem_probs'] = [_softmax(sc, mem_scale) for sc in s.pop('mem_scores')]

    def attn_out(t):
        s = st[t]
        s['attn'] = jnp.concatenate([_unstack_group_outputs(o, s['q_rows']) for o in s.pop('attn_outs')],
                                    axis=2).reshape(rows, ATTN_Q).astype(BF16)

    def spatial_out(t):
        s = st[t]
        mixed_sg = jnp.concatenate(s.pop('sg_cols'), axis=1)
        bias = sgb_ref[...]
        if rows // chunk > 1:
            bias = jnp.concatenate([bias] * (rows // chunk), axis=0)
        s['sg'] = (s.pop('u') * (mixed_sg + bias)).astype(BF16)

    def mem_values(t):
        s = st[t]
        mem_probs, mem_vals = s.pop('mem_probs'), s.pop('mem_vals')
        if sample:
            o = jnp.einsum('bqk,bkd->bqd', mem_probs[0], mem_vals[0], preferred_element_type=F32)
            cols = [o[:, hd * SUBLANES:(hd + 1) * SUBLANES, hd * MEM_HEAD_DIM:(hd + 1) * MEM_HEAD_DIM]
                    .reshape(rows, MEM_HEAD_DIM) for hd in range(MEM_HEADS)]
        else:
            cols = [_dot(mem_probs[hd], mem_vals[hd]) for hd in range(MEM_HEADS)]
        s['memo'] = jnp.concatenate(cols, axis=1).astype(BF16)

    def merge(t):
        s = st[t]
        gates(t, len(s['gate_pieces']))
        gate_sigmoids(t)
        mixed = None
        for n, name in enumerate(('attn', 'sg', 'memo')):
            proj = _dot(s.pop(name), w_o_ref[n * ATTN_Q:(n + 1) * ATTN_Q, :])
            term = gate_ref[n, rows_of(t), :] * proj
            mixed = term if mixed is None else mixed + term
        x1_ref[rows_of(t), :] = s.pop('x') + _rms(mixed, g_post_ref[...])

    stages = {'head': head, 'inproj': in_proj, 'rope': rope_stage, 'scores': scores, 'gelu': gelu_ln,
              'softmax': softmax, 'values': values, 'outs': branch_outs, 'memv': mem_values, 'merge': merge,
              'sig': gate_sigmoids, 'pv': attn_values, 'memsc': mem_scores, 'sgmix': spatial_mix,
              'memsm': mem_softmax, 'attnout': attn_out, 'sgout': spatial_out}
    def run(tok):
        name, t = tok[:-1], ord(tok[-1]) - ord('A')
        if name.startswith('gates'):
            gates(t, int(name[len('gates'):]))
        elif name[-1].isdigit():
            stages[name[:-1]](t, int(name[-1]))
        else:
            stages[name](t)

    return init, run


def _mix_kernel_tiles(*refs, sample, schedule):
    init, run = _mix_program(refs, sample, _sub_tiles(schedule))
    init()
    for tok in schedule.split():
        run(tok)


def _ffn_program(refs, sample):
    if sample:
        (x_ref, g_pre_ref, w_up_ref, cw_ref, cb_ref, w_down_ref, g_post_ref, st_ref,
         y_ref, nc_ref, *up_s) = refs
    else:
        (x_ref, g_pre_ref, w_up_ref, cw_ref, cb_ref, w_down_ref, g_post_ref,
         y_ref, nc_ref, *up_s) = refs
    rows = x_ref.shape[0]
    step = pl.program_id(0)
    st = {}

    def init():
        if not sample:
            @pl.when(step == 0)
            def _():
                for slab in up_s:
                    slab[:SUBLANES, :] = jnp.zeros((SUBLANES, LANES), F32)

    def head():
        st['x'] = x_ref[...]
        st['h'] = _rms(st['x'], g_pre_ref[...]).astype(BF16)
        st['up_pieces'] = [(half * D_FF + n0, min(FFN_UP_CHUNK, D_FF - n0))
                           for n0 in range(0, D_FF, FFN_UP_CHUNK) for half in range(2)]
        st['chunks'] = [(c0, min(FFN_CHUNK, D_FF - c0)) for c0 in range(0, D_FF, FFN_CHUNK)]
        st['f'] = None

    hist = CONV_WIDTH - 1
    seqs = rows // SUBLANES

    def up_cols(col0, width):
        up = jnp.dot(st['h'], w_up_ref[:, col0:col0 + width], preferred_element_type=F32)
        for t in range(width // LANES):
            s = col0 // LANES + t
            blk = up[:, t * LANES:(t + 1) * LANES]
            if sample:
                up_s[s][:, SUBLANES:, :] = blk.reshape(seqs, SUBLANES, LANES)
                up_s[s][:, SUBLANES - hist:SUBLANES, :] = st_ref[:, :, s * LANES:(s + 1) * LANES]
            else:
                up_s[s][SUBLANES:, :] = blk

    def conv_slab(s):
        cols = slice(s * LANES, (s + 1) * LANES)
        if sample:
            taps = [up_s[s][:, SUBLANES - hist + j:2 * SUBLANES - hist + j, :] for j in range(CONV_WIDTH)]
            nc_ref[:, :, cols] = up_s[s][:, 2 * SUBLANES - hist:, :]
        else:
            taps = [up_s[s][SUBLANES - hist + j:SUBLANES - hist + j + rows, :] for j in range(CONV_WIDTH)]
            nc_ref[:, cols] = up_s[s][SUBLANES + rows - hist:, :]
            up_s[s][:SUBLANES, :] = up_s[s][rows:, :]
        c = cb_ref[:, cols]
        for j in range(CONV_WIDTH):
            c = c + taps[j] * cw_ref[j:j + 1, cols]
        return c.reshape(rows, LANES)

    def up(count):
        for _ in range(min(count, len(st['up_pieces']))):
            up_cols(*st['up_pieces'].pop(0))

    def conv_down(count):
        for _ in range(min(count, len(st['chunks']))):
            c0, width = st['chunks'].pop(0)
            acts = []
            for t in range(width // LANES):
                gate = conv_slab(c0 // LANES + t)
                val = conv_slab((D_FF + c0) // LANES + t)
                acts.append((_gelu_tanh(gate) * val).astype(BF16))
            act = jnp.concatenate(acts, axis=1)
            part = jnp.dot(act, w_down_ref[c0:c0 + width, :], preferred_element_type=F32)
            st['f'] = part if st['f'] is None else st['f'] + part

    def tail():
        up(len(st['up_pieces']))
        conv_down(len(st['chunks']))
        y_ref[...] = st.pop('x') + _rms(st.pop('f'), g_post_ref[...])

    def run(tok):
        if tok == 'head':
            head()
        elif tok == 'tail':
            tail()
        elif tok.startswith('up'):
            up(int(tok[2:]))
        else:
            assert tok.startswith('vd'), tok
            conv_down(int(tok[2:]))

    return init, run


def _ffn_kernel(*refs, sample):
    init, run = _ffn_program(refs, sample)
    init()
    for tok in FFN_SCHEDULE.split():
        run(tok)


def _memkv_kernel(mem_ref, g_ref, w_ref, k_ref, v_ref, kb_ref, vb_ref):
    h = _rms(mem_ref[...], g_ref[...]).astype(BF16)
    kv = _dot(h, w_ref[...].astype(BF16))
    kb_ref[...] = kv[:, :MEM_Q].astype(BF16)
    vb_ref[...] = kv[:, MEM_Q:].astype(BF16)
    for hd in range(MEM_HEADS):
        k_ref[pl.ds(hd, N_MEM, stride=MEM_HEADS), :] = kv[:, hd * MEM_HEAD_DIM:(hd + 1) * MEM_HEAD_DIM]
        v_ref[pl.ds(hd, N_MEM, stride=MEM_HEADS), :] = kv[:, MEM_Q + hd * MEM_HEAD_DIM:
                                                          MEM_Q + (hd + 1) * MEM_HEAD_DIM]


def _resident():
    return pl.BlockSpec(memory_space=pltpu.VMEM)


def _row_spec(rows, width):
    return pl.BlockSpec((rows, width), lambda i: (i, 0))


def _const_spec(shape):
    nd = len(shape)
    return pl.BlockSpec(shape, lambda i: (0,) * nd)


def _compiler_params():
    return pltpu.CompilerParams(dimension_semantics=("arbitrary",), vmem_limit_bytes=VMEM_LIMIT_BYTES)


def _rope_tables(pos):
    half = HEAD_DIM // 2
    inv_freq = np.float64(ROPE_THETA) ** (-np.arange(half, dtype=np.float64) / half)
    ang = pos.astype(np.float64)[:, None] * inv_freq[None, :]
    cos = np.concatenate([np.cos(ang)] * (LANES // half), axis=1).astype(np.float32)
    sin = np.concatenate([np.sin(ang)] * (LANES // half), axis=1).astype(np.float32)
    sign = np.concatenate([-np.ones((1, half), np.float32), np.ones((1, half), np.float32)]
                          * (LANES // HEAD_DIM), axis=1)
    return cos, sin, cos * sign, sin * sign


def _mix_call(x2d, start, seq_len, lp, mem_k, mem_v, caches):
    n_rows = x2d.shape[0]
    sample = caches is not None
    rows = SAMPLE_SEQS * seq_len if sample else PROMPT_MIX_ROWS
    schedule = MIX_SCHEDULE_ONE if sample else MIX_SCHEDULE_TWO
    steps = n_rows // rows
    if sample:
        pos = start + (np.arange(rows, dtype=np.int32) % seq_len)
        cos, _, _, sin_signed = _rope_tables(pos)
        rope_args = [cos, sin_signed]
        rope_specs = [_const_spec((rows, LANES))] * 2
        chunk = rows
        wmix = jnp.tile(lp['sg_w'][:, :seq_len, :seq_len], (1, rows // seq_len, rows // seq_len))
        sgb = jnp.repeat(jnp.tile(lp['sg_b'][:, :seq_len], (1, rows // seq_len)).T, SG_GROUP_DIM, axis=1)
    else:
        in_step = _rope_tables(np.arange(rows, dtype=np.int32))
        cos_b, sin_b, _, _ = _rope_tables(start + rows * np.arange(steps, dtype=np.int32))
        rope_args = list(in_step) + [cos_b, sin_b]
        rope_specs = [_const_spec((rows, LANES))] * 4 + [_const_spec((steps, LANES))] * 2
        chunk = CHUNK
        wmix = lp['sg_w'][:, :chunk, :chunk]
        sgb = jnp.repeat(lp['sg_b'][:, :chunk].T, SG_GROUP_DIM, axis=1)

    in_specs = [
        pl.BlockSpec(memory_space=pltpu.SMEM),
        _row_spec(rows, D_MODEL),
        *rope_specs,
        _const_spec((1, D_MODEL)),
        _resident(),
        _const_spec((1, SG_WIDTH)), _const_spec((1, SG_WIDTH)),
        _const_spec((SG_GROUPS, chunk, chunk)),
        _const_spec((chunk, SG_WIDTH)),
    ]
    args = [lp['sinks'], x2d, *rope_args, lp['pre_mix_g'].reshape(1, D_MODEL), lp['w_in'],
            lp['sg_ln_g'].reshape(1, SG_WIDTH), lp['sg_ln_b'].reshape(1, SG_WIDTH), wmix, sgb]
    if sample:
        seqs = SAMPLE_SEQS
        in_specs += [pl.BlockSpec((seqs, N_MEM * MEM_HEADS, MEM_HEAD_DIM), lambda i: (i, 0, 0))] * 2
    else:
        in_specs += [_const_spec((N_MEM, MEM_Q))] * 2
    args += [mem_k, mem_v]
    in_specs += [_resident(), _const_spec((1, D_MODEL))]
    args += [lp['w_o'], lp['post_mix_g'].reshape(1, D_MODEL)]

    out_shape = [jax.ShapeDtypeStruct((n_rows, D_MODEL), F32)]
    out_specs = [_row_spec(rows, D_MODEL)]
    scratch = []
    if sample:
        win_k, win_v = caches
        n_seq, past = win_k.shape[0], win_k.shape[1]
        cache_spec = pl.BlockSpec((SAMPLE_SEQS, past, ATTN_KV), lambda i: (i, 0, 0))
        in_specs += [cache_spec, cache_spec]
        args += [win_k, win_v]
        out_shape += [jax.ShapeDtypeStruct((n_seq, past, ATTN_KV), F32)] * 2
        out_specs += [cache_spec, cache_spec]
        out_shape += [jax.ShapeDtypeStruct((n_rows, SG_WIDTH), F32)]
        out_specs += [_row_spec(rows, SG_WIDTH)]
    else:
        out_shape += [jax.ShapeDtypeStruct((WINDOW, ATTN_KV), F32)] * 2
        out_specs += [_const_spec((WINDOW, ATTN_KV))] * 2
        scratch = [pltpu.VMEM((WINDOW, ATTN_KV), F32), pltpu.VMEM((WINDOW, ATTN_KV), F32)]
    scratch += [pltpu.VMEM((N_BRANCHES, rows, D_MODEL), F32)]

    return pl.pallas_call(
        functools.partial(_mix_kernel_tiles, sample=sample, schedule=schedule),
        grid=(steps,),
        in_specs=in_specs,
        out_specs=out_specs,
        out_shape=out_shape,
        scratch_shapes=scratch,
        compiler_params=_compiler_params(),
        name="mix_sample" if sample else "mix_prompt",
    )(*args)


def _ffn_call(x2d, seq_len, lp, conv_state):
    n_rows = x2d.shape[0]
    sample = conv_state is not None
    rows = SAMPLE_FFN_SEQS * seq_len if sample else PROMPT_ROWS
    steps = n_rows // rows
    in_specs = [
        _row_spec(rows, D_MODEL),
        _const_spec((1, D_MODEL)),
        _resident(),
        _const_spec((CONV_WIDTH, 2 * D_FF)),
        _const_spec((1, 2 * D_FF)),
        _resident(),
        _const_spec((1, D_MODEL)),
    ]
    args = [x2d, lp['pre_ffn_g'].reshape(1, D_MODEL), lp['w_up'], lp['conv_w'],
            lp['conv_b'].reshape(1, 2 * D_FF), lp['w_down'], lp['post_ffn_g'].reshape(1, D_MODEL)]
    out_shape = [jax.ShapeDtypeStruct((n_rows, D_MODEL), F32)]
    out_specs = [_row_spec(rows, D_MODEL)]
    n_slabs = 2 * D_FF // LANES
    if sample:
        n_seq = conv_state.shape[0]
        st_spec = pl.BlockSpec((SAMPLE_FFN_SEQS, CONV_WIDTH - 1, 2 * D_FF), lambda i: (i, 0, 0))
        in_specs += [st_spec]
        args += [conv_state]
        out_shape += [jax.ShapeDtypeStruct((n_seq, CONV_WIDTH - 1, 2 * D_FF), F32)]
        out_specs += [st_spec]
        scratch = [pltpu.VMEM((SAMPLE_FFN_SEQS, 2 * SUBLANES, LANES), F32)] * n_slabs
    else:
        out_shape += [jax.ShapeDtypeStruct((CONV_WIDTH - 1, 2 * D_FF), F32)]
        out_specs += [_const_spec((CONV_WIDTH - 1, 2 * D_FF))]
        scratch = [pltpu.VMEM((SUBLANES + rows, LANES), F32)] * n_slabs
    return pl.pallas_call(
        functools.partial(_ffn_kernel, sample=sample),
        grid=(steps,),
        in_specs=in_specs,
        out_specs=out_specs,
        out_shape=out_shape,
        scratch_shapes=scratch,
        compiler_params=_compiler_params(),
        name="ffn_sample" if sample else "ffn_prompt",
    )(*args)


def _memkv_call(mem2d, g, w_f32):
    return pl.pallas_call(
        _memkv_kernel,
        out_shape=[jax.ShapeDtypeStruct((mem2d.shape[0] * MEM_HEADS, MEM_HEAD_DIM), F32)] * 2
        + [jax.ShapeDtypeStruct((mem2d.shape[0], MEM_Q), BF16)] * 2,
        compiler_params=pltpu.CompilerParams(vmem_limit_bytes=VMEM_LIMIT_BYTES),
        name="mem_kv",
    )(mem2d, g.reshape(1, D_MODEL), w_f32)


def kernel(x_prompt, x_sample, cache_win_k, cache_win_v, cache_mem_k, cache_mem_v, state_conv, mem_prompt,
           pre_mix_g, w_in, attn_sinks, sg_ln_g, sg_ln_b, sg_w, sg_b, mem_norm_g, w_mem_kv, w_o,
           post_mix_g, pre_ffn_g, w_up, conv_w, conv_b, w_down, post_ffn_g):
    depth = w_in.shape[0]
    batch, seq, _ = x_prompt.shape
    dec_batch, dec_seq, _ = x_sample.shape
    past_len = PAST_LEN
    assert batch == 1 and depth == 1

    yp = x_prompt.reshape(batch * seq, D_MODEL)
    ys = x_sample.reshape(dec_batch * dec_seq, D_MODEL)
    outs = {name: [] for name in ('wk_p', 'wv_p', 'mk_p', 'mv_p', 'cv_p', 'wk_s', 'wv_s', 'sgv_s', 'cv_s')}
    for l in range(depth):
        lp = {
            'sinks': attn_sinks[l], 'pre_mix_g': pre_mix_g[l], 'w_in': w_in[l].astype(BF16),
            'sg_ln_g': sg_ln_g[l], 'sg_ln_b': sg_ln_b[l], 'sg_w': sg_w[l], 'sg_b': sg_b[l],
            'w_o': w_o[l].reshape(N_BRANCHES * ATTN_Q, D_MODEL).astype(BF16), 'post_mix_g': post_mix_g[l],
            'pre_ffn_g': pre_ffn_g[l], 'w_up': w_up[l].astype(BF16), 'conv_w': conv_w[l], 'conv_b': conv_b[l],
            'w_down': w_down[l].astype(BF16), 'post_ffn_g': post_ffn_g[l],
        }
        mem_k, mem_v, mem_k_b, mem_v_b = _memkv_call(mem_prompt.reshape(batch * N_MEM, D_MODEL), mem_norm_g[l],
                                                     w_mem_kv[l])
        outs['mk_p'].append(mem_k.reshape(batch, N_MEM, MEM_HEADS, MEM_HEAD_DIM))
        outs['mv_p'].append(mem_v.reshape(batch, N_MEM, MEM_HEADS, MEM_HEAD_DIM))

        x1, wk, wv = _mix_call(yp, 0, seq, lp, mem_k_b, mem_v_b, None)
        yp, nc = _ffn_call(x1, seq, lp, None)
        outs['wk_p'].append(wk.reshape(batch, N_KV_HEADS, HEAD_DIM, WINDOW).transpose(0, 3, 1, 2))
        outs['wv_p'].append(wv.reshape(batch, N_KV_HEADS, HEAD_DIM, WINDOW).transpose(0, 3, 1, 2))
        outs['cv_p'].append(nc.reshape(batch, CONV_WIDTH - 1, 2 * D_FF))

        past = cache_win_k.shape[2]
        x1s, wks, wvs, sgv = _mix_call(
            ys, past_len, dec_seq, lp,
            cache_mem_k[l].reshape(dec_batch, N_MEM * MEM_HEADS, MEM_HEAD_DIM),
            cache_mem_v[l].reshape(dec_batch, N_MEM * MEM_HEADS, MEM_HEAD_DIM),
            (cache_win_k[l].reshape(dec_batch, past, ATTN_KV), cache_win_v[l].reshape(dec_batch, past, ATTN_KV)))
        ys, ncs = _ffn_call(x1s, dec_seq, lp, state_conv[l])
        outs['wk_s'].append(wks.reshape(dec_batch, past, N_KV_HEADS, HEAD_DIM))
        outs['wv_s'].append(wvs.reshape(dec_batch, past, N_KV_HEADS, HEAD_DIM))
        outs['sgv_s'].append(sgv.reshape(dec_batch, dec_seq, SG_WIDTH))
        outs['cv_s'].append(ncs)

    return (yp.reshape(batch, seq, D_MODEL), ys.reshape(dec_batch, dec_seq, D_MODEL),
            jnp.stack(outs['wk_p']), jnp.stack(outs['wv_p']), jnp.stack(outs['mk_p']), jnp.stack(outs['mv_p']),
            jnp.stack(outs['cv_p']),
            jnp.stack(outs['wk_s']), jnp.stack(outs['wv_s']), jnp.stack(outs['sgv_s']), jnp.stack(outs['cv_s']))
```

```python
import functools
import math

import jax
import jax.numpy as jnp
import numpy as np
from jax import lax
from jax.experimental import pallas as pl
from jax.experimental.pallas import tpu as pltpu

F32 = jnp.float32
BF16 = jnp.bfloat16

D_MODEL = 1024
N_Q_HEADS = 8
N_KV_HEADS = 2
GROUP = N_Q_HEADS // N_KV_HEADS
HEAD_DIM = 64
ATTN_Q = N_Q_HEADS * HEAD_DIM
ATTN_KV = N_KV_HEADS * HEAD_DIM
WINDOW = 128
ROPE_THETA = 10000.0
CHUNK = 128
SG_GROUPS = 4
SG_GROUP_DIM = 128
SG_WIDTH = SG_GROUPS * SG_GROUP_DIM
N_MEM = 256
MEM_HEADS = 4
MEM_HEAD_DIM = 128
MEM_Q = MEM_HEADS * MEM_HEAD_DIM
N_BRANCHES = 3
D_FF = 2816
CONV_WIDTH = 3
EPS = 1e-6
NEG_INF = -1e30
LOG2E = math.log2(math.e)
PAST_LEN = 16384

OFF_Q = 0
OFF_K = ATTN_Q
OFF_V = OFF_K + ATTN_KV
OFF_SGU = OFF_V + ATTN_KV
OFF_SGV = OFF_SGU + SG_WIDTH
OFF_MQ = OFF_SGV + SG_WIDTH
OFF_GATE = OFF_MQ + MEM_Q
IN_WIDTH = OFF_GATE + N_BRANCHES * D_MODEL

LANES = 128
SUBLANES = 8
VMEM_LIMIT_BYTES = 56 * 1024 * 1024

PROMPT_ROWS = 512
PROMPT_MIX_ROWS = 512
MIX_SCHEDULE_ONE = ("headA inprojA ropeA scoresA gates4A geluA softmaxA valuesA gates4A outsA memvA mergeA")


MIX_SCHEDULE_TWO = (
    "headA inprojA memscA ropeA scoresA headB inprojB geluA gates1A sgmixA gates1A memsmA gates1A memvA gates1A "
    "softmaxA gates1A pvA gates1A ropeB memscB scoresB sgoutA gates1A attnoutA sigA geluB gates1A sgmixB gates1A "
    "memsmB gates1A memvB gates1A softmaxB gates1A gates1B sigA pvB gates2B mergeA gates2B sgoutB gates2B "
    "attnoutB gates5B sigB mergeB")
SAMPLE_SEQS = 8
MEM_RING_SLOTS = 3
SAMPLE_FFN_SEQS = 32
FFN_CHUNK = 256
FFN_UP_CHUNK = 256
FFN_SCHEDULE = "head up22 vd11 tail"
GATE_CHUNK = 256


def _rms(x, g):
    return x * lax.rsqrt(jnp.mean(x * x, axis=-1, keepdims=True) + EPS) * g


def _dot(a, b):
    return jnp.dot(a, b, preferred_element_type=F32)


def _gelu_erf(x):
    return 0.5 * x * (1.0 + lax.erf(x * np.float32(math.sqrt(0.5))))


def _gelu_tanh(x):
    c = np.float32(math.sqrt(2.0 / math.pi))
    return x * (0.5 * (1.0 + jnp.tanh(c * (x + 0.044715 * (x * x * x)))))


def _lane_iota(shape):
    return lax.broadcasted_iota(jnp.int32, shape, len(shape) - 1)


def _rope(x, cos, sin_signed):
    w = x.shape[1]
    reps = w // LANES
    c = jnp.concatenate([cos] * reps, axis=1) if reps > 1 else cos
    s = jnp.concatenate([sin_signed] * reps, axis=1) if reps > 1 else sin_signed
    first_half = (_lane_iota(x.shape) % HEAD_DIM) < (HEAD_DIM // 2)
    swapped = jnp.where(first_half, pltpu.roll(x, w - HEAD_DIM // 2, 1), pltpu.roll(x, HEAD_DIM // 2, 1))
    return x * c + swapped * s


def _dup_head(kv, g):
    rolled = pltpu.roll(kv, HEAD_DIM, 1)
    low = _lane_iota(kv.shape) < HEAD_DIM
    return jnp.where(low, kv, rolled) if g == 0 else jnp.where(low, rolled, kv)


def _stack_group_queries(q, g):
    low = _lane_iota(q.shape[:2] + (LANES,)) < HEAD_DIM
    parts = []
    for i in range(GROUP):
        h = GROUP * g + i
        pair = q[:, :, (h // 2) * LANES:(h // 2 + 1) * LANES]
        parts.append(jnp.where(low if h % 2 == 0 else ~low, pair, 0.0))
    return jnp.concatenate(parts, axis=1)


def _unstack_group_outputs(o, rows):
    low = _lane_iota((o.shape[0], rows, LANES)) < HEAD_DIM
    pairs = []
    for p in range(GROUP // 2):
        even = o[:, (2 * p) * rows:(2 * p + 1) * rows]
        odd = o[:, (2 * p + 1) * rows:(2 * p + 2) * rows]
        pairs.append(jnp.where(low, even, odd))
    return jnp.concatenate(pairs, axis=2)


def _sink_softmax(s, scale, mask, sink):
    t = jnp.where(mask, s * np.float32(scale * LOG2E), NEG_INF)
    sink2 = sink * np.float32(LOG2E)
    m = jnp.maximum(jnp.max(t, axis=-1, keepdims=True), sink2)
    e = jnp.exp2(t - m)
    denom = jnp.sum(e, axis=-1, keepdims=True) + jnp.exp2(sink2 - m)
    return (e * (1.0 / denom)).astype(BF16)


def _softmax(s, scale):
    t = s * np.float32(scale * LOG2E)
    m = jnp.max(t, axis=-1, keepdims=True)
    e = jnp.exp2(t - m)
    return (e * (1.0 / jnp.sum(e, axis=-1, keepdims=True))).astype(BF16)


def _sigmoid(x):
    return 1.0 / (1.0 + jnp.exp2(x * np.float32(-LOG2E)))


def _sink_column(sinks_ref, g, rows):
    r = lax.broadcasted_iota(jnp.int32, (1, GROUP * rows, 1), 1)
    col = jnp.full((1, GROUP * rows, 1), sinks_ref[GROUP * g + GROUP - 1], F32)
    for i in range(GROUP - 2, -1, -1):
        col = jnp.where(r < (i + 1) * rows, sinks_ref[GROUP * g + i], col)
    return col


def _sub_tiles(schedule):
    return 1 + max(ord(tok[-1]) - ord('A') for tok in schedule.split())


def _mix_program(refs, sample, n_sub):
    if sample:
        (sinks_ref, x_ref, cos_ref, sin_ref, g_pre_ref, w_in_ref, ln_g_ref, ln_b_ref, wmix_ref, sgb_ref,
         mk_ref, mv_ref, w_o_ref, g_post_ref, ck_ref, cv_ref,
         x1_ref, wk_ref, wv_ref, sgv_ref, gate_ref, kbuf_ref, vbuf_ref, mem_sem) = refs
    else:
        (sinks_ref, x_ref, cos_ref, sin_ref, cos_s_ref, sin_s_ref, cos_b_ref, sin_b_ref,
         g_pre_ref, w_in_ref, ln_g_ref, ln_b_ref, wmix_ref, sgb_ref,
         mk_ref, mv_ref, w_o_ref, g_post_ref,
         x1_ref, wk_ref, wv_ref, kcar_ref, vcar_ref, gate_ref) = refs
    rows = x_ref.shape[0] // n_sub
    seqs = rows // SUBLANES
    step = pl.program_id(0)
    assert not (sample and n_sub > 1)

    def mem_copies(j, slot):
        src = pl.ds(j * seqs, seqs)
        return (pltpu.make_async_copy(mk_ref.at[src], kbuf_ref.at[slot], mem_sem.at[0, slot]),
                pltpu.make_async_copy(mv_ref.at[src], vbuf_ref.at[slot], mem_sem.at[1, slot]))

    def init():
        if not sample:
            @pl.when(step == 0)
            def _():
                kcar_ref[...] = jnp.zeros_like(kcar_ref)
                vcar_ref[...] = jnp.zeros_like(vcar_ref)
        else:
            ahead = MEM_RING_SLOTS - 1

            @pl.when(step == 0)
            def _():
                for j in range(ahead):
                    for cp in mem_copies(j, j):
                        cp.start()

            @pl.when(step + ahead < pl.num_programs(0))
            def _():
                for cp in mem_copies(step + ahead, lax.rem(step + ahead, MEM_RING_SLOTS)):
                    cp.start()

            for cp in mem_copies(step, lax.rem(step, MEM_RING_SLOTS)):
                cp.wait()

    st = [dict() for _ in range(n_sub)]
    chunk = wmix_ref.shape[1]
    mem_scale = MEM_HEAD_DIM ** -0.5

    def rows_of(t):
        return slice(t * rows, (t + 1) * rows)

    def head(t):
        s = st[t]
        s['x'] = x_ref[rows_of(t), :]
        s['h'] = _rms(s['x'], g_pre_ref[...]).astype(BF16)
        s['gate_pieces'] = [(n, c) for n in range(N_BRANCHES) for c in range(0, D_MODEL, GATE_CHUNK)]
        s['logit_pieces'] = []

    def gates(t, count):
        s = st[t]
        for _ in range(min(count, len(s['gate_pieces']))):
            n, c = s['gate_pieces'].pop(0)
            col = OFF_GATE + n * D_MODEL + c
            gate_ref[n, rows_of(t), c:c + GATE_CHUNK] = _dot(s['h'], w_in_ref[:, col:col + GATE_CHUNK])
            s['logit_pieces'].append((n, c))

    def gate_sigmoids(t):
        s = st[t]
        while s['logit_pieces']:
            n, c = s['logit_pieces'].pop(0)
            piece = (n, rows_of(t), slice(c, c + GATE_CHUNK))
            gate_ref[piece] = _sigmoid(gate_ref[piece])

    def in_proj(t):
        s = st[t]
        s['zqkv'] = _dot(s['h'], w_in_ref[:, OFF_Q:OFF_SGU])
        s['zsg'] = _dot(s['h'], w_in_ref[:, OFF_SGU:OFF_MQ])
        s['mq'] = _dot(s['h'], w_in_ref[:, OFF_MQ:OFF_GATE])

    def rope_stage(t):
        s = st[t]
        zqkv = s.pop('zqkv')
        if sample:
            cos, sin_signed = cos_ref[...], sin_ref[...]
        else:
            cb = cos_b_ref[pl.ds(step, 1), :]
            sb = sin_b_ref[pl.ds(step, 1), :]
            cos = cb * cos_ref[rows_of(t), :] - sb * sin_ref[rows_of(t), :]
            sin_signed = sb * cos_s_ref[rows_of(t), :] + cb * sin_s_ref[rows_of(t), :]
        qk = _rope(zqkv[:, :OFF_V], cos, sin_signed)
        q = qk[:, :ATTN_Q]
        k = qk[:, OFF_K:OFF_V]
        v = zqkv[:, OFF_V:OFF_SGU]
        if sample:
            past = ck_ref.shape[1]
            k3 = k.reshape(seqs, SUBLANES, ATTN_KV)
            v3 = v.reshape(seqs, SUBLANES, ATTN_KV)
            ck = ck_ref[...]
            cv = cv_ref[...]
            wk_ref[:, :past - SUBLANES, :] = ck[:, SUBLANES:, :]
            wk_ref[:, past - SUBLANES:, :] = k3
            wv_ref[:, :past - SUBLANES, :] = cv[:, SUBLANES:, :]
            wv_ref[:, past - SUBLANES:, :] = v3
            n_keys = past + 2 * SUBLANES
            pad = jnp.zeros((seqs, SUBLANES, ATTN_KV), F32)
            kk = jnp.concatenate([ck, k3, pad], axis=1).reshape(seqs * n_keys, ATTN_KV)
            vv = jnp.concatenate([cv, v3, pad], axis=1).reshape(seqs * n_keys, ATTN_KV)
            q3 = q.reshape(seqs, SUBLANES, ATTN_Q)
            s['q_rows'] = SUBLANES
            qi = lax.broadcasted_iota(jnp.int32, (1, GROUP * SUBLANES, n_keys), 1) % SUBLANES
            kj = lax.broadcasted_iota(jnp.int32, (1, GROUP * SUBLANES, n_keys), 2)
            s['mask'] = (kj > qi + (past - WINDOW)) & (kj <= qi + past)
            s['kds'] = [_dup_head(kk, g).astype(BF16).reshape(seqs, n_keys, LANES) for g in range(N_KV_HEADS)]
            s['vds'] = [_dup_head(vv, g).astype(BF16).reshape(seqs, n_keys, LANES) for g in range(N_KV_HEADS)]
        else:
            nb = rows // WINDOW
            k_prev = kcar_ref[...] if t == 0 else st[t - 1]['k_last']
            v_prev = vcar_ref[...] if t == 0 else st[t - 1]['v_last']
            s['k_last'] = k[rows - WINDOW:]
            s['v_last'] = v[rows - WINDOW:]
            if t == n_sub - 1:
                wk_ref[...] = s['k_last'].T
                wv_ref[...] = s['v_last'].T
                kcar_ref[...] = s['k_last']
                vcar_ref[...] = s['v_last']
            kk = jnp.concatenate([k_prev, k], axis=0)
            vv = jnp.concatenate([v_prev, v], axis=0)
            q3 = q.reshape(nb, WINDOW, ATTN_Q)
            s['q_rows'] = WINDOW
            shape = (nb, GROUP * WINDOW, 2 * WINDOW)
            qi = lax.broadcasted_iota(jnp.int32, shape, 1) % WINDOW
            kj = lax.broadcasted_iota(jnp.int32, shape, 2)
            s['mask'] = (kj > qi) & (kj <= qi + WINDOW)
            if t == 0:
                blk = lax.broadcasted_iota(jnp.int32, shape, 0)
                first_key = jnp.where((blk == 0) & (step == 0), WINDOW, 0)
                s['mask'] = s['mask'] & (kj >= first_key)
            s['kds'], s['vds'] = [], []
            for g in range(N_KV_HEADS):
                kd = _dup_head(kk, g).astype(BF16)
                vd = _dup_head(vv, g).astype(BF16)
                s['kds'].append(jnp.stack([kd[n * WINDOW:(n + 2) * WINDOW] for n in range(nb)], axis=0))
                s['vds'].append(jnp.stack([vd[n * WINDOW:(n + 2) * WINDOW] for n in range(nb)], axis=0))
        s['qss'] = [_stack_group_queries(q3, g).astype(BF16) for g in range(N_KV_HEADS)]
        s['sinks'] = [_sink_column(sinks_ref, g, s['q_rows']) for g in range(N_KV_HEADS)]

    def scores(t):
        s = st[t]
        qss, kds = s.pop('qss'), s.pop('kds')
        s['scores'] = [jnp.einsum('bqd,bkd->bqk', q, k, preferred_element_type=F32) for q, k in zip(qss, kds)]

    def gelu_ln(t, part=None):
        s = st[t]
        if part in (None, 0):
            s['u'] = _gelu_erf(s['zsg'][:, :SG_WIDTH])
        if part in (None, 1):
            gv = _gelu_erf(s['zsg'][:, SG_WIDTH:])
            gc = gv - jnp.mean(gv, axis=-1, keepdims=True)
            var = jnp.mean(gc * gc, axis=-1, keepdims=True)
            vn = gc * lax.rsqrt(var + EPS) * ln_g_ref[...] + ln_b_ref[...]
            if sample:
                sgv_ref[...] = vn
            s['vn_b'] = vn.astype(BF16)

    def softmax(t, g=None):
        s = st[t]
        for gi in range(N_KV_HEADS) if g is None else (g,):
            s.setdefault('probs', {})[gi] = _sink_softmax(s['scores'][gi], HEAD_DIM ** -0.5, s['mask'],
                                                          s['sinks'][gi])

    def values(t):
        attn_values(t)
        mem_scores(t)
        spatial_mix(t)

    def attn_values(t):
        s = st[t]
        probs, vds = s.pop('probs'), s.pop('vds')
        s['attn_outs'] = [jnp.einsum('bqk,bkd->bqd', probs[g], vds[g], preferred_element_type=F32)
                          for g in range(len(vds))]

    def mem_scores(t):
        s = st[t]
        mq = s.pop('mq')
        s['mem_scores'], s['mem_vals'] = [], []
        if sample:
            mq3 = mq.reshape(seqs, SUBLANES, MEM_Q)
            lane_head = _lane_iota(mq3.shape) // MEM_HEAD_DIM
            q_all = jnp.concatenate([jnp.where(lane_head == hd, mq3, 0.0) for hd in range(MEM_HEADS)],
                                    axis=1).astype(BF16)
            slot = lax.rem(step, MEM_RING_SLOTS)
            k_all = jnp.concatenate([kbuf_ref[slot, :, pl.ds(hd, N_MEM, stride=MEM_HEADS), :]
                                     for hd in range(MEM_HEADS)], axis=2).astype(BF16)
            v_all = jnp.concatenate([vbuf_ref[slot, :, pl.ds(hd, N_MEM, stride=MEM_HEADS), :]
                                     for hd in range(MEM_HEADS)], axis=2).astype(BF16)
            s['mem_scores'].append(jnp.einsum('bqd,bkd->bqk', q_all, k_all, preferred_element_type=F32))
            s['mem_vals'].append(v_all)
        else:
            for hd in range(MEM_HEADS):
                sl = slice(hd * MEM_HEAD_DIM, (hd + 1) * MEM_HEAD_DIM)
                s['mem_vals'].append(mv_ref[:, sl])
                s['mem_scores'].append(lax.dot_general(mq[:, sl].astype(BF16), mk_ref[:, sl],
                                                       (((1,), (1,)), ((), ())), preferred_element_type=F32))

    def spatial_mix(t):
        s = st[t]
        vn_b = s.pop('vn_b')
        t_idx = lax.broadcasted_iota(jnp.int32, (chunk, chunk), 0)
        s_idx = lax.broadcasted_iota(jnp.int32, (chunk, chunk), 1)
        keep = t_idx >= s_idx
        if sample:
            keep = keep & ((t_idx // SUBLANES) == (s_idx // SUBLANES))
        s['sg_cols'] = []
        for g in range(SG_GROUPS):
            wm = jnp.where(keep, wmix_ref[g], 0.0).astype(BF16)
            blocks = [_dot(wm, vn_b[c * chunk:(c + 1) * chunk, g * SG_GROUP_DIM:(g + 1) * SG_GROUP_DIM])
                      for c in range(rows // chunk)]
            s['sg_cols'].append(jnp.concatenate(blocks, axis=0) if len(blocks) > 1 else blocks[0])

    def branch_outs(t):
        mem_softmax(t)
        attn_out(t)
        spatial_out(t)

    def mem_softmax(t):
        s = st[t]
        s['mem_probs'] = [_softmax(sc, mem_scale) for sc in s.pop('mem_scores')]

    def attn_out(t):
        s = st[t]
        s['attn'] = jnp.concatenate([_unstack_group_outputs(o, s['q_rows']) for o in s.pop('attn_outs')],
                                    axis=2).reshape(rows, ATTN_Q).astype(BF16)

    def spatial_out(t):
        s = st[t]
        mixed_sg = jnp.concatenate(s.pop('sg_cols'), axis=1)
        bias = sgb_ref[...]
        if rows // chunk > 1:
            bias = jnp.concatenate([bias] * (rows // chunk), axis=0)
        s['sg'] = (s.pop('u') * (mixed_sg + bias)).astype(BF16)

    def mem_values(t):
        s = st[t]
        mem_probs, mem_vals = s.pop('mem_probs'), s.pop('mem_vals')
        if sample:
            o = jnp.einsum('bqk,bkd->bqd', mem_probs[0], mem_vals[0], preferred_element_type=F32)
            cols = [o[:, hd * SUBLANES:(hd + 1) * SUBLANES, hd * MEM_HEAD_DIM:(hd + 1) * MEM_HEAD_DIM]
                    .reshape(rows, MEM_HEAD_DIM) for hd in range(MEM_HEADS)]
        else:
            cols = [_dot(mem_probs[hd], mem_vals[hd]) for hd in range(MEM_HEADS)]
        s['memo'] = jnp.concatenate(cols, axis=1).astype(BF16)

    def merge(t):
        s = st[t]
        gates(t, len(s['gate_pieces']))
        gate_sigmoids(t)
        mixed = None
        for n, name in enumerate(('attn', 'sg', 'memo')):
            proj = _dot(s.pop(name), w_o_ref[n * ATTN_Q:(n + 1) * ATTN_Q, :])
            term = gate_ref[n, rows_of(t), :] * proj
            mixed = term if mixed is None else mixed + term
        x1_ref[rows_of(t), :] = s.pop('x') + _rms(mixed, g_post_ref[...])

    stages = {'head': head, 'inproj': in_proj, 'rope': rope_stage, 'scores': scores, 'gelu': gelu_ln,
              'softmax': softmax, 'values': values, 'outs': branch_outs, 'memv': mem_values, 'merge': merge,
              'sig': gate_sigmoids, 'pv': attn_values, 'memsc': mem_scores, 'sgmix': spatial_mix,
              'memsm': mem_softmax, 'attnout': attn_out, 'sgout': spatial_out}
    def run(tok):
        name, t = tok[:-1], ord(tok[-1]) - ord('A')
        if name.startswith('gates'):
            gates(t, int(name[len('gates'):]))
        elif name[-1].isdigit():
            stages[name[:-1]](t, int(name[-1]))
        else:
            stages[name](t)

    return init, run


def _mix_kernel_tiles(*refs, sample, schedule):
    init, run = _mix_program(refs, sample, _sub_tiles(schedule))
    init()
    for tok in schedule.split():
        run(tok)


def _ffn_program(refs, sample):
    if sample:
        (x_ref, g_pre_ref, w_up_ref, cw_ref, cb_ref, w_down_ref, g_post_ref, st_ref,
         y_ref, nc_ref, *up_s) = refs
    else:
        (x_ref, g_pre_ref, w_up_ref, cw_ref, cb_ref, w_down_ref, g_post_ref,
         y_ref, nc_ref, *up_s) = refs
    rows = x_ref.shape[0]
    step = pl.program_id(0)
    st = {}

    def init():
        if not sample:
            @pl.when(step == 0)
            def _():
                for slab in up_s:
                    slab[:SUBLANES, :] = jnp.zeros((SUBLANES, LANES), F32)

    def head():
        st['x'] = x_ref[...]
        st['h'] = _rms(st['x'], g_pre_ref[...]).astype(BF16)
        st['up_pieces'] = [(half * D_FF + n0, min(FFN_UP_CHUNK, D_FF - n0))
                           for n0 in range(0, D_FF, FFN_UP_CHUNK) for half in range(2)]
        st['chunks'] = [(c0, min(FFN_CHUNK, D_FF - c0)) for c0 in range(0, D_FF, FFN_CHUNK)]
        st['f'] = None

    hist = CONV_WIDTH - 1
    seqs = rows // SUBLANES

    def up_cols(col0, width):
        up = jnp.dot(st['h'], w_up_ref[:, col0:col0 + width], preferred_element_type=F32)
        for t in range(width // LANES):
            s = col0 // LANES + t
            blk = up[:, t * LANES:(t + 1) * LANES]
            if sample:
                up_s[s][:, SUBLANES:, :] = blk.reshape(seqs, SUBLANES, LANES)
                up_s[s][:, SUBLANES - hist:SUBLANES, :] = st_ref[:, :, s * LANES:(s + 1) * LANES]
            else:
                up_s[s][SUBLANES:, :] = blk

    def conv_slab(s):
        cols = slice(s * LANES, (s + 1) * LANES)
        if sample:
            taps = [up_s[s][:, SUBLANES - hist + j:2 * SUBLANES - hist + j, :] for j in range(CONV_WIDTH)]
            nc_ref[:, :, cols] = up_s[s][:, 2 * SUBLANES - hist:, :]
        else:
            taps = [up_s[s][SUBLANES - hist + j:SUBLANES - hist + j + rows, :] for j in range(CONV_WIDTH)]
            nc_ref[:, cols] = up_s[s][SUBLANES + rows - hist:, :]
            up_s[s][:SUBLANES, :] = up_s[s][rows:, :]
        c = cb_ref[:, cols]
        for j in range(CONV_WIDTH):
            c = c + taps[j] * cw_ref[j:j + 1, cols]
        return c.reshape(rows, LANES)

    def up(count):
        for _ in range(min(count, len(st['up_pieces']))):
            up_cols(*st['up_pieces'].pop(0))

    def conv_down(count):
        for _ in range(min(count, len(st['chunks']))):
            c0, width = st['chunks'].pop(0)
            acts = []
            for t in range(width // LANES):
                gate = conv_slab(c0 // LANES + t)
                val = conv_slab((D_FF + c0) // LANES + t)
                acts.append((_gelu_tanh(gate) * val).astype(BF16))
            act = jnp.concatenate(acts, axis=1)
            part = jnp.dot(act, w_down_ref[c0:c0 + width, :], preferred_element_type=F32)
            st['f'] = part if st['f'] is None else st['f'] + part

    def tail():
        up(len(st['up_pieces']))
        conv_down(len(st['chunks']))
        y_ref[...] = st.pop('x') + _rms(st.pop('f'), g_post_ref[...])

    def run(tok):
        if tok == 'head':
            head()
        elif tok == 'tail':
            tail()
        elif tok.startswith('up'):
            up(int(tok[2:]))
        else:
            assert tok.startswith('vd'), tok
            conv_down(int(tok[2:]))

    return init, run


def _ffn_kernel(*refs, sample):
    init, run = _ffn_program(refs, sample)
    init()
    for tok in FFN_SCHEDULE.split():
        run(tok)


def _memkv_kernel(mem_ref, g_ref, w_ref, k_ref, v_ref, kb_ref, vb_ref):
    h = _rms(mem_ref[...], g_ref[...]).astype(BF16)
    kv = _dot(h, w_ref[...].astype(BF16))
    kb_ref[...] = kv[:, :MEM_Q].astype(BF16)
    vb_ref[...] = kv[:, MEM_Q:].astype(BF16)
    for hd in range(MEM_HEADS):
        k_ref[pl.ds(hd, N_MEM, stride=MEM_HEADS), :] = kv[:, hd * MEM_HEAD_DIM:(hd + 1) * MEM_HEAD_DIM]
        v_ref[pl.ds(hd, N_MEM, stride=MEM_HEADS), :] = kv[:, MEM_Q + hd * MEM_HEAD_DIM:
                                                          MEM_Q + (hd + 1) * MEM_HEAD_DIM]


def _resident():
    return pl.BlockSpec(memory_space=pltpu.VMEM)


def _row_spec(rows, width):
    return pl.BlockSpec((rows, width), lambda i: (i, 0))


def _const_spec(shape):
    nd = len(shape)
    return pl.BlockSpec(shape, lambda i: (0,) * nd)


def _compiler_params():
    return pltpu.CompilerParams(dimension_semantics=("arbitrary",), vmem_limit_bytes=VMEM_LIMIT_BYTES)


def _rope_tables(pos):
    half = HEAD_DIM // 2
    inv_freq = np.float64(ROPE_THETA) ** (-np.arange(half, dtype=np.float64) / half)
    ang = pos.astype(np.float64)[:, None] * inv_freq[None, :]
    cos = np.concatenate([np.cos(ang)] * (LANES // half), axis=1).astype(np.float32)
    sin = np.concatenate([np.sin(ang)] * (LANES // half), axis=1).astype(np.float32)
    sign = np.concatenate([-np.ones((1, half), np.float32), np.ones((1, half), np.float32)]
                          * (LANES // HEAD_DIM), axis=1)
    return cos, sin, cos * sign, sin * sign


def _mix_call(x2d, start, seq_len, lp, mem_k, mem_v, caches):
    n_rows = x2d.shape[0]
    sample = caches is not None
    rows = SAMPLE_SEQS * seq_len if sample else PROMPT_MIX_ROWS
    schedule = MIX_SCHEDULE_ONE if sample else MIX_SCHEDULE_TWO
    steps = n_rows // rows
    if sample:
        pos = start + (np.arange(rows, dtype=np.int32) % seq_len)
        cos, _, _, sin_signed = _rope_tables(pos)
        rope_args = [cos, sin_signed]
        rope_specs = [_const_spec((rows, LANES))] * 2
        chunk = rows
        wmix = jnp.tile(lp['sg_w'][:, :seq_len, :seq_len], (1, rows // seq_len, rows // seq_len))
        sgb = jnp.repeat(jnp.tile(lp['sg_b'][:, :seq_len], (1, rows // seq_len)).T, SG_GROUP_DIM, axis=1)
    else:
        in_step = _rope_tables(np.arange(rows, dtype=np.int32))
        cos_b, sin_b, _, _ = _rope_tables(start + rows * np.arange(steps, dtype=np.int32))
        rope_args = list(in_step) + [cos_b, sin_b]
        rope_specs = [_const_spec((rows, LANES))] * 4 + [_const_spec((steps, LANES))] * 2
        chunk = CHUNK
        wmix = lp['sg_w'][:, :chunk, :chunk]
        sgb = jnp.repeat(lp['sg_b'][:, :chunk].T, SG_GROUP_DIM, axis=1)

    in_specs = [
        pl.BlockSpec(memory_space=pltpu.SMEM),
        _row_spec(rows, D_MODEL),
        *rope_specs,
        _const_spec((1, D_MODEL)),
        _resident(),
        _const_spec((1, SG_WIDTH)), _const_spec((1, SG_WIDTH)),
        _const_spec((SG_GROUPS, chunk, chunk)),
        _const_spec((chunk, SG_WIDTH)),
    ]
    args = [lp['sinks'], x2d, *rope_args, lp['pre_mix_g'].reshape(1, D_MODEL), lp['w_in'],
            lp['sg_ln_g'].reshape(1, SG_WIDTH), lp['sg_ln_b'].reshape(1, SG_WIDTH), wmix, sgb]
    if sample:
        seqs = SAMPLE_SEQS
        in_specs += [pl.BlockSpec(memory_space=pl.ANY)] * 2
    else:
        in_specs += [_const_spec((N_MEM, MEM_Q))] * 2
    args += [mem_k, mem_v]
    in_specs += [_resident(), _const_spec((1, D_MODEL))]
    args += [lp['w_o'], lp['post_mix_g'].reshape(1, D_MODEL)]

    out_shape = [jax.ShapeDtypeStruct((n_rows, D_MODEL), F32)]
    out_specs = [_row_spec(rows, D_MODEL)]
    scratch = []
    if sample:
        win_k, win_v = caches
        n_seq, past = win_k.shape[0], win_k.shape[1]
        cache_spec = pl.BlockSpec((SAMPLE_SEQS, past, ATTN_KV), lambda i: (i, 0, 0))
        in_specs += [cache_spec, cache_spec]
        args += [win_k, win_v]
        out_shape += [jax.ShapeDtypeStruct((n_seq, past, ATTN_KV), F32)] * 2
        out_specs += [cache_spec, cache_spec]
        out_shape += [jax.ShapeDtypeStruct((n_rows, SG_WIDTH), F32)]
        out_specs += [_row_spec(rows, SG_WIDTH)]
    else:
        out_shape += [jax.ShapeDtypeStruct((WINDOW, ATTN_KV), F32)] * 2
        out_specs += [_const_spec((WINDOW, ATTN_KV))] * 2
        scratch = [pltpu.VMEM((WINDOW, ATTN_KV), F32), pltpu.VMEM((WINDOW, ATTN_KV), F32)]
    scratch += [pltpu.VMEM((N_BRANCHES, rows, D_MODEL), F32)]
    if sample:
        ring = (MEM_RING_SLOTS, SAMPLE_SEQS, N_MEM * MEM_HEADS, MEM_HEAD_DIM)
        scratch += [pltpu.VMEM(ring, F32), pltpu.VMEM(ring, F32), pltpu.SemaphoreType.DMA((2, MEM_RING_SLOTS))]

    return pl.pallas_call(
        functools.partial(_mix_kernel_tiles, sample=sample, schedule=schedule),
        grid=(steps,),
        in_specs=in_specs,
        out_specs=out_specs,
        out_shape=out_shape,
        scratch_shapes=scratch,
        compiler_params=_compiler_params(),
        name="mix_sample" if sample else "mix_prompt",
    )(*args)


def _ffn_call(x2d, seq_len, lp, conv_state):
    n_rows = x2d.shape[0]
    sample = conv_state is not None
    rows = SAMPLE_FFN_SEQS * seq_len if sample else PROMPT_ROWS
    steps = n_rows // rows
    in_specs = [
        _row_spec(rows, D_MODEL),
        _const_spec((1, D_MODEL)),
        _resident(),
        _const_spec((CONV_WIDTH, 2 * D_FF)),
        _const_spec((1, 2 * D_FF)),
        _resident(),
        _const_spec((1, D_MODEL)),
    ]
    args = [x2d, lp['pre_ffn_g'].reshape(1, D_MODEL), lp['w_up'], lp['conv_w'],
            lp['conv_b'].reshape(1, 2 * D_FF), lp['w_down'], lp['post_ffn_g'].reshape(1, D_MODEL)]
    out_shape = [jax.ShapeDtypeStruct((n_rows, D_MODEL), F32)]
    out_specs = [_row_spec(rows, D_MODEL)]
    n_slabs = 2 * D_FF // LANES
    if sample:
        n_seq = conv_state.shape[0]
        st_spec = pl.BlockSpec((SAMPLE_FFN_SEQS, CONV_WIDTH - 1, 2 * D_FF), lambda i: (i, 0, 0))
        in_specs += [st_spec]
        args += [conv_state]
        out_shape += [jax.ShapeDtypeStruct((n_seq, CONV_WIDTH - 1, 2 * D_FF), F32)]
        out_specs += [st_spec]
        scratch = [pltpu.VMEM((SAMPLE_FFN_SEQS, 2 * SUBLANES, LANES), F32)] * n_slabs
    else:
        out_shape += [jax.ShapeDtypeStruct((CONV_WIDTH - 1, 2 * D_FF), F32)]
        out_specs += [_const_spec((CONV_WIDTH - 1, 2 * D_FF))]
        scratch = [pltpu.VMEM((SUBLANES + rows, LANES), F32)] * n_slabs
    return pl.pallas_call(
        functools.partial(_ffn_kernel, sample=sample),
        grid=(steps,),
        in_specs=in_specs,
        out_specs=out_specs,
        out_shape=out_shape,
        scratch_shapes=scratch,
        compiler_params=_compiler_params(),
        name="ffn_sample" if sample else "ffn_prompt",
    )(*args)


def _memkv_call(mem2d, g, w_f32):
    return pl.pallas_call(
        _memkv_kernel,
        out_shape=[jax.ShapeDtypeStruct((mem2d.shape[0] * MEM_HEADS, MEM_HEAD_DIM), F32)] * 2
        + [jax.ShapeDtypeStruct((mem2d.shape[0], MEM_Q), BF16)] * 2,
        compiler_params=pltpu.CompilerParams(vmem_limit_bytes=VMEM_LIMIT_BYTES),
        name="mem_kv",
    )(mem2d, g.reshape(1, D_MODEL), w_f32)


def kernel(x_prompt, x_sample, cache_win_k, cache_win_v, cache_mem_k, cache_mem_v, state_conv, mem_prompt,
           pre_mix_g, w_in, attn_sinks, sg_ln_g, sg_ln_b, sg_w, sg_b, mem_norm_g, w_mem_kv, w_o,
           post_mix_g, pre_ffn_g, w_up, conv_w, conv_b, w_down, post_ffn_g):
    depth = w_in.shape[0]
    batch, seq, _ = x_prompt.shape
    dec_batch, dec_seq, _ = x_sample.shape
    past_len = PAST_LEN
    assert batch == 1 and depth == 1

    yp = x_prompt.reshape(batch * seq, D_MODEL)
    ys = x_sample.reshape(dec_batch * dec_seq, D_MODEL)
    outs = {name: [] for name in ('wk_p', 'wv_p', 'mk_p', 'mv_p', 'cv_p', 'wk_s', 'wv_s', 'sgv_s', 'cv_s')}
    for l in range(depth):
        lp = {
            'sinks': attn_sinks[l], 'pre_mix_g': pre_mix_g[l], 'w_in': w_in[l].astype(BF16),
            'sg_ln_g': sg_ln_g[l], 'sg_ln_b': sg_ln_b[l], 'sg_w': sg_w[l], 'sg_b': sg_b[l],
            'w_o': w_o[l].reshape(N_BRANCHES * ATTN_Q, D_MODEL).astype(BF16), 'post_mix_g': post_mix_g[l],
            'pre_ffn_g': pre_ffn_g[l], 'w_up': w_up[l].astype(BF16), 'conv_w': conv_w[l], 'conv_b': conv_b[l],
            'w_down': w_down[l].astype(BF16), 'post_ffn_g': post_ffn_g[l],
        }
        mem_k, mem_v, mem_k_b, mem_v_b = _memkv_call(mem_prompt.reshape(batch * N_MEM, D_MODEL), mem_norm_g[l],
                                                     w_mem_kv[l])
        outs['mk_p'].append(mem_k.reshape(batch, N_MEM, MEM_HEADS, MEM_HEAD_DIM))
        outs['mv_p'].append(mem_v.reshape(batch, N_MEM, MEM_HEADS, MEM_HEAD_DIM))

        x1, wk, wv = _mix_call(yp, 0, seq, lp, mem_k_b, mem_v_b, None)
        yp, nc = _ffn_call(x1, seq, lp, None)
        outs['wk_p'].append(wk.reshape(batch, N_KV_HEADS, HEAD_DIM, WINDOW).transpose(0, 3, 1, 2))
        outs['wv_p'].append(wv.reshape(batch, N_KV_HEADS, HEAD_DIM, WINDOW).transpose(0, 3, 1, 2))
        outs['cv_p'].append(nc.reshape(batch, CONV_WIDTH - 1, 2 * D_FF))

        past = cache_win_k.shape[2]
        x1s, wks, wvs, sgv = _mix_call(
            ys, past_len, dec_seq, lp,
            cache_mem_k[l].reshape(dec_batch, N_MEM * MEM_HEADS, MEM_HEAD_DIM),
            cache_mem_v[l].reshape(dec_batch, N_MEM * MEM_HEADS, MEM_HEAD_DIM),
            (cache_win_k[l].reshape(dec_batch, past, ATTN_KV), cache_win_v[l].reshape(dec_batch, past, ATTN_KV)))
        ys, ncs = _ffn_call(x1s, dec_seq, lp, state_conv[l])
        outs['wk_s'].append(wks.reshape(dec_batch, past, N_KV_HEADS, HEAD_DIM))
        outs['wv_s'].append(wvs.reshape(dec_batch, past, N_KV_HEADS, HEAD_DIM))
        outs['sgv_s'].append(sgv.reshape(dec_batch, dec_seq, SG_WIDTH))
        outs['cv_s'].append(ncs)

    return (yp.reshape(batch, seq, D_MODEL), ys.reshape(dec_batch, dec_seq, D_MODEL),
            jnp.stack(outs['wk_p']), jnp.stack(outs['wv_p']), jnp.stack(outs['mk_p']), jnp.stack(outs['mv_p']),
            jnp.stack(outs['cv_p']),
            jnp.stack(outs['wk_s']), jnp.stack(outs['wv_s']), jnp.stack(outs['sgv_s']), jnp.stack(outs['cv_s']))
```

```python
import functools
import math

import jax
import jax.numpy as jnp
import numpy as np
from jax import lax
from jax.experimental import pallas as pl
from jax.experimental.pallas import tpu as pltpu

F32 = jnp.float32
BF16 = jnp.bfloat16

D_MODEL = 1024
N_Q_HEADS = 8
N_KV_HEADS = 2
GROUP = N_Q_HEADS // N_KV_HEADS
HEAD_DIM = 64
ATTN_Q = N_Q_HEADS * HEAD_DIM
ATTN_KV = N_KV_HEADS * HEAD_DIM
WINDOW = 128
ROPE_THETA = 10000.0
CHUNK = 128
SG_GROUPS = 4
SG_GROUP_DIM = 128
SG_WIDTH = SG_GROUPS * SG_GROUP_DIM
N_MEM = 256
MEM_HEADS = 4
MEM_HEAD_DIM = 128
MEM_Q = MEM_HEADS * MEM_HEAD_DIM
N_BRANCHES = 3
D_FF = 2816
CONV_WIDTH = 3
EPS = 1e-6
NEG_INF = -1e30
LOG2E = math.log2(math.e)
PAST_LEN = 16384

OFF_Q = 0
OFF_K = ATTN_Q
OFF_V = OFF_K + ATTN_KV
OFF_SGU = OFF_V + ATTN_KV
OFF_SGV = OFF_SGU + SG_WIDTH
OFF_MQ = OFF_SGV + SG_WIDTH
OFF_GATE = OFF_MQ + MEM_Q
IN_WIDTH = OFF_GATE + N_BRANCHES * D_MODEL

LANES = 128
SUBLANES = 8
VMEM_LIMIT_BYTES = 56 * 1024 * 1024

PROMPT_ROWS = 512
PROMPT_MIX_ROWS = 512
MIX_SCHEDULE_ONE = ("headA inprojA ropeA scoresA gates4A geluA softmaxA valuesA gates4A outsA memvA mergeA")


MIX_SCHEDULE_TWO = (
    "headA inprojA memscA ropeA scoresA headB inprojB geluA gates1A sgmixA gates1A memsmA gates1A memvA gates1A "
    "softmaxA gates1A pvA gates1A ropeB memscB scoresB sgoutA gates1A attnoutA sigA geluB gates1A sgmixB gates1A "
    "memsmB gates1A memvB gates1A softmaxB gates1A gates1B sigA pvB gates2B mergeA gates2B sgoutB gates2B "
    "attnoutB gates5B sigB mergeB")
SAMPLE_SEQS = 8
MEM_RING_SLOTS = 4
SAMPLE_FFN_SEQS = 32
FFN_CHUNK = 256
FFN_UP_CHUNK = 256
FFN_SCHEDULE = "head up22 vd11 tail"
GATE_CHUNK = 256


def _rms(x, g):
    return x * lax.rsqrt(jnp.mean(x * x, axis=-1, keepdims=True) + EPS) * g


def _dot(a, b):
    return jnp.dot(a, b, preferred_element_type=F32)


def _gelu_erf(x):
    return 0.5 * x * (1.0 + lax.erf(x * np.float32(math.sqrt(0.5))))


def _gelu_tanh(x):
    c = np.float32(math.sqrt(2.0 / math.pi))
    return x * (0.5 * (1.0 + jnp.tanh(c * (x + 0.044715 * (x * x * x)))))


def _lane_iota(shape):
    return lax.broadcasted_iota(jnp.int32, shape, len(shape) - 1)


def _rope(x, cos, sin_signed):
    w = x.shape[1]
    reps = w // LANES
    c = jnp.concatenate([cos] * reps, axis=1) if reps > 1 else cos
    s = jnp.concatenate([sin_signed] * reps, axis=1) if reps > 1 else sin_signed
    first_half = (_lane_iota(x.shape) % HEAD_DIM) < (HEAD_DIM // 2)
    swapped = jnp.where(first_half, pltpu.roll(x, w - HEAD_DIM // 2, 1), pltpu.roll(x, HEAD_DIM // 2, 1))
    return x * c + swapped * s


def _dup_head(kv, g):
    rolled = pltpu.roll(kv, HEAD_DIM, 1)
    low = _lane_iota(kv.shape) < HEAD_DIM
    return jnp.where(low, kv, rolled) if g == 0 else jnp.where(low, rolled, kv)


def _stack_group_queries(q, g):
    low = _lane_iota(q.shape[:2] + (LANES,)) < HEAD_DIM
    parts = []
    for i in range(GROUP):
        h = GROUP * g + i
        pair = q[:, :, (h // 2) * LANES:(h // 2 + 1) * LANES]
        parts.append(jnp.where(low if h % 2 == 0 else ~low, pair, 0.0))
    return jnp.concatenate(parts, axis=1)


def _unstack_group_outputs(o, rows):
    low = _lane_iota((o.shape[0], rows, LANES)) < HEAD_DIM
    pairs = []
    for p in range(GROUP // 2):
        even = o[:, (2 * p) * rows:(2 * p + 1) * rows]
        odd = o[:, (2 * p + 1) * rows:(2 * p + 2) * rows]
        pairs.append(jnp.where(low, even, odd))
    return jnp.concatenate(pairs, axis=2)


def _sink_softmax(s, scale, mask, sink):
    t = jnp.where(mask, s * np.float32(scale * LOG2E), NEG_INF)
    sink2 = sink * np.float32(LOG2E)
    m = jnp.maximum(jnp.max(t, axis=-1, keepdims=True), sink2)
    e = jnp.exp2(t - m)
    denom = jnp.sum(e, axis=-1, keepdims=True) + jnp.exp2(sink2 - m)
    return (e * (1.0 / denom)).astype(BF16)


def _softmax(s, scale):
    t = s * np.float32(scale * LOG2E)
    m = jnp.max(t, axis=-1, keepdims=True)
    e = jnp.exp2(t - m)
    return (e * (1.0 / jnp.sum(e, axis=-1, keepdims=True))).astype(BF16)


def _sigmoid(x):
    return 1.0 / (1.0 + jnp.exp2(x * np.float32(-LOG2E)))


def _sink_column(sinks_ref, g, rows):
    r = lax.broadcasted_iota(jnp.int32, (1, GROUP * rows, 1), 1)
    col = jnp.full((1, GROUP * rows, 1), sinks_ref[GROUP * g + GROUP - 1], F32)
    for i in range(GROUP - 2, -1, -1):
        col = jnp.where(r < (i + 1) * rows, sinks_ref[GROUP * g + i], col)
    return col


def _sub_tiles(schedule):
    return 1 + max(ord(tok[-1]) - ord('A') for tok in schedule.split())


def _mix_program(refs, sample, n_sub):
    if sample:
        (sinks_ref, x_ref, cos_ref, sin_ref, g_pre_ref, w_in_ref, ln_g_ref, ln_b_ref, wmix_ref, sgb_ref,
         mk_ref, mv_ref, w_o_ref, g_post_ref, ck_ref, cv_ref,
         x1_ref, wk_ref, wv_ref, sgv_ref, gate_ref, kbuf_ref, vbuf_ref, mem_sem) = refs
    else:
        (sinks_ref, x_ref, cos_ref, sin_ref, cos_s_ref, sin_s_ref, cos_b_ref, sin_b_ref,
         g_pre_ref, w_in_ref, ln_g_ref, ln_b_ref, wmix_ref, sgb_ref,
         mk_ref, mv_ref, w_o_ref, g_post_ref,
         x1_ref, wk_ref, wv_ref, kcar_ref, vcar_ref, gate_ref) = refs
    rows = x_ref.shape[0] // n_sub
    seqs = rows // SUBLANES
    step = pl.program_id(0)
    assert not (sample and n_sub > 1)

    def mem_copies(j, slot):
        src = pl.ds(j * seqs, seqs)
        return (pltpu.make_async_copy(mk_ref.at[src], kbuf_ref.at[slot], mem_sem.at[0, slot]),
                pltpu.make_async_copy(mv_ref.at[src], vbuf_ref.at[slot], mem_sem.at[1, slot]))

    def init():
        if not sample:
            @pl.when(step == 0)
            def _():
                kcar_ref[...] = jnp.zeros_like(kcar_ref)
                vcar_ref[...] = jnp.zeros_like(vcar_ref)
        else:
            ahead = MEM_RING_SLOTS - 1

            @pl.when(step == 0)
            def _():
                for j in range(ahead):
                    for cp in mem_copies(j, j):
                        cp.start()

            @pl.when(step + ahead < pl.num_programs(0))
            def _():
                for cp in mem_copies(step + ahead, lax.rem(step + ahead, MEM_RING_SLOTS)):
                    cp.start()

            for cp in mem_copies(step, lax.rem(step, MEM_RING_SLOTS)):
                cp.wait()

    st = [dict() for _ in range(n_sub)]
    chunk = wmix_ref.shape[1]
    mem_scale = MEM_HEAD_DIM ** -0.5

    def rows_of(t):
        return slice(t * rows, (t + 1) * rows)

    def head(t):
        s = st[t]
        s['x'] = x_ref[rows_of(t), :]
        s['h'] = _rms(s['x'], g_pre_ref[...]).astype(BF16)
        s['gate_pieces'] = [(n, c) for n in range(N_BRANCHES) for c in range(0, D_MODEL, GATE_CHUNK)]
        s['logit_pieces'] = []

    def gates(t, count):
        s = st[t]
        for _ in range(min(count, len(s['gate_pieces']))):
            n, c = s['gate_pieces'].pop(0)
            col = OFF_GATE + n * D_MODEL + c
            gate_ref[n, rows_of(t), c:c + GATE_CHUNK] = _dot(s['h'], w_in_ref[:, col:col + GATE_CHUNK])
            s['logit_pieces'].append((n, c))

    def gate_sigmoids(t):
        s = st[t]
        while s['logit_pieces']:
            n, c = s['logit_pieces'].pop(0)
            piece = (n, rows_of(t), slice(c, c + GATE_CHUNK))
            gate_ref[piece] = _sigmoid(gate_ref[piece])

    def in_proj(t):
        s = st[t]
        s['zqkv'] = _dot(s['h'], w_in_ref[:, OFF_Q:OFF_SGU])
        s['zsg'] = _dot(s['h'], w_in_ref[:, OFF_SGU:OFF_MQ])
        s['mq'] = _dot(s['h'], w_in_ref[:, OFF_MQ:OFF_GATE])

    def rope_stage(t):
        s = st[t]
        zqkv = s.pop('zqkv')
        if sample:
            cos, sin_signed = cos_ref[...], sin_ref[...]
        else:
            cb = cos_b_ref[pl.ds(step, 1), :]
            sb = sin_b_ref[pl.ds(step, 1), :]
            cos = cb * cos_ref[rows_of(t), :] - sb * sin_ref[rows_of(t), :]
            sin_signed = sb * cos_s_ref[rows_of(t), :] + cb * sin_s_ref[rows_of(t), :]
        qk = _rope(zqkv[:, :OFF_V], cos, sin_signed)
        q = qk[:, :ATTN_Q]
        k = qk[:, OFF_K:OFF_V]
        v = zqkv[:, OFF_V:OFF_SGU]
        if sample:
            past = ck_ref.shape[1]
            k3 = k.reshape(seqs, SUBLANES, ATTN_KV)
            v3 = v.reshape(seqs, SUBLANES, ATTN_KV)
            ck = ck_ref[...]
            cv = cv_ref[...]
            wk_ref[:, :past - SUBLANES, :] = ck[:, SUBLANES:, :]
            wk_ref[:, past - SUBLANES:, :] = k3
            wv_ref[:, :past - SUBLANES, :] = cv[:, SUBLANES:, :]
            wv_ref[:, past - SUBLANES:, :] = v3
            n_keys = past + 2 * SUBLANES
            pad = jnp.zeros((seqs, SUBLANES, ATTN_KV), F32)
            kk = jnp.concatenate([ck, k3, pad], axis=1).reshape(seqs * n_keys, ATTN_KV)
            vv = jnp.concatenate([cv, v3, pad], axis=1).reshape(seqs * n_keys, ATTN_KV)
            q3 = q.reshape(seqs, SUBLANES, ATTN_Q)
            s['q_rows'] = SUBLANES
            qi = lax.broadcasted_iota(jnp.int32, (1, GROUP * SUBLANES, n_keys), 1) % SUBLANES
            kj = lax.broadcasted_iota(jnp.int32, (1, GROUP * SUBLANES, n_keys), 2)
            s['mask'] = (kj > qi + (past - WINDOW)) & (kj <= qi + past)
            s['kds'] = [_dup_head(kk, g).astype(BF16).reshape(seqs, n_keys, LANES) for g in range(N_KV_HEADS)]
            s['vds'] = [_dup_head(vv, g).astype(BF16).reshape(seqs, n_keys, LANES) for g in range(N_KV_HEADS)]
        else:
            nb = rows // WINDOW
            k_prev = kcar_ref[...] if t == 0 else st[t - 1]['k_last']
            v_prev = vcar_ref[...] if t == 0 else st[t - 1]['v_last']
            s['k_last'] = k[rows - WINDOW:]
            s['v_last'] = v[rows - WINDOW:]
            if t == n_sub - 1:
                wk_ref[...] = s['k_last'].T
                wv_ref[...] = s['v_last'].T
                kcar_ref[...] = s['k_last']
                vcar_ref[...] = s['v_last']
            kk = jnp.concatenate([k_prev, k], axis=0)
            vv = jnp.concatenate([v_prev, v], axis=0)
            q3 = q.reshape(nb, WINDOW, ATTN_Q)
            s['q_rows'] = WINDOW
            shape = (nb, GROUP * WINDOW, 2 * WINDOW)
            qi = lax.broadcasted_iota(jnp.int32, shape, 1) % WINDOW
            kj = lax.broadcasted_iota(jnp.int32, shape, 2)
            s['mask'] = (kj > qi) & (kj <= qi + WINDOW)
            if t == 0:
                blk = lax.broadcasted_iota(jnp.int32, shape, 0)
                first_key = jnp.where((blk == 0) & (step == 0), WINDOW, 0)
                s['mask'] = s['mask'] & (kj >= first_key)
            s['kds'], s['vds'] = [], []
            for g in range(N_KV_HEADS):
                kd = _dup_head(kk, g).astype(BF16)
                vd = _dup_head(vv, g).astype(BF16)
                s['kds'].append(jnp.stack([kd[n * WINDOW:(n + 2) * WINDOW] for n in range(nb)], axis=0))
                s['vds'].append(jnp.stack([vd[n * WINDOW:(n + 2) * WINDOW] for n in range(nb)], axis=0))
        s['qss'] = [_stack_group_queries(q3, g).astype(BF16) for g in range(N_KV_HEADS)]
        s['sinks'] = [_sink_column(sinks_ref, g, s['q_rows']) for g in range(N_KV_HEADS)]

    def scores(t):
        s = st[t]
        qss, kds = s.pop('qss'), s.pop('kds')
        s['scores'] = [jnp.einsum('bqd,bkd->bqk', q, k, preferred_element_type=F32) for q, k in zip(qss, kds)]

    def gelu_ln(t, part=None):
        s = st[t]
        if part in (None, 0):
            s['u'] = _gelu_erf(s['zsg'][:, :SG_WIDTH])
        if part in (None, 1):
            gv = _gelu_erf(s['zsg'][:, SG_WIDTH:])
            gc = gv - jnp.mean(gv, axis=-1, keepdims=True)
            var = jnp.mean(gc * gc, axis=-1, keepdims=True)
            vn = gc * lax.rsqrt(var + EPS) * ln_g_ref[...] + ln_b_ref[...]
            if sample:
                sgv_ref[...] = vn
            s['vn_b'] = vn.astype(BF16)

    def softmax(t, g=None):
        s = st[t]
        for gi in range(N_KV_HEADS) if g is None else (g,):
            s.setdefault('probs', {})[gi] = _sink_softmax(s['scores'][gi], HEAD_DIM ** -0.5, s['mask'],
                                                          s['sinks'][gi])

    def values(t):
        attn_values(t)
        mem_scores(t)
        spatial_mix(t)

    def attn_values(t):
        s = st[t]
        probs, vds = s.pop('probs'), s.pop('vds')
        s['attn_outs'] = [jnp.einsum('bqk,bkd->bqd', probs[g], vds[g], preferred_element_type=F32)
                          for g in range(len(vds))]

    def mem_scores(t):
        s = st[t]
        mq = s.pop('mq')
        s['mem_scores'], s['mem_vals'] = [], []
        if sample:
            mq3 = mq.reshape(seqs, SUBLANES, MEM_Q)
            lane_head = _lane_iota(mq3.shape) // MEM_HEAD_DIM
            q_all = jnp.concatenate([jnp.where(lane_head == hd, mq3, 0.0) for hd in range(MEM_HEADS)],
                                    axis=1).astype(BF16)
            slot = lax.rem(step, MEM_RING_SLOTS)
            k_all = jnp.concatenate([kbuf_ref[slot, :, pl.ds(hd, N_MEM, stride=MEM_HEADS), :]
                                     for hd in range(MEM_HEADS)], axis=2).astype(BF16)
            v_all = jnp.concatenate([vbuf_ref[slot, :, pl.ds(hd, N_MEM, stride=MEM_HEADS), :]
                                     for hd in range(MEM_HEADS)], axis=2).astype(BF16)
            s['mem_scores'].append(jnp.einsum('bqd,bkd->bqk', q_all, k_all, preferred_element_type=F32))
            s['mem_vals'].append(v_all)
        else:
            for hd in range(MEM_HEADS):
                sl = slice(hd * MEM_HEAD_DIM, (hd + 1) * MEM_HEAD_DIM)
                s['mem_vals'].append(mv_ref[:, sl])
                s['mem_scores'].append(lax.dot_general(mq[:, sl].astype(BF16), mk_ref[:, sl],
                                                       (((1,), (1,)), ((), ())), preferred_element_type=F32))

    def spatial_mix(t):
        s = st[t]
        vn_b = s.pop('vn_b')
        t_idx = lax.broadcasted_iota(jnp.int32, (chunk, chunk), 0)
        s_idx = lax.broadcasted_iota(jnp.int32, (chunk, chunk), 1)
        keep = t_idx >= s_idx
        if sample:
            keep = keep & ((t_idx // SUBLANES) == (s_idx // SUBLANES))
        s['sg_cols'] = []
        for g in range(SG_GROUPS):
            wm = jnp.where(keep, wmix_ref[g], 0.0).astype(BF16)
            blocks = [_dot(wm, vn_b[c * chunk:(c + 1) * chunk, g * SG_GROUP_DIM:(g + 1) * SG_GROUP_DIM])
                      for c in range(rows // chunk)]
            s['sg_cols'].append(jnp.concatenate(blocks, axis=0) if len(blocks) > 1 else blocks[0])

    def branch_outs(t):
        mem_softmax(t)
        attn_out(t)
        spatial_out(t)

    def mem_softmax(t):
        s = st[t]
        s['mem_probs'] = [_softmax(sc, mem_scale) for sc in s.pop('mem_scores')]

    def attn_out(t):
        s = st[t]
        s['attn'] = jnp.concatenate([_unstack_group_outputs(o, s['q_rows']) for o in s.pop('attn_outs')],
                                    axis=2).reshape(rows, ATTN_Q).astype(BF16)

    def spatial_out(t):
        s = st[t]
        mixed_sg = jnp.concatenate(s.pop('sg_cols'), axis=1)
        bias = sgb_ref[...]
        if rows // chunk > 1:
            bias = jnp.concatenate([bias] * (rows // chunk), axis=0)
        s['sg'] = (s.pop('u') * (mixed_sg + bias)).astype(BF16)

    def mem_values(t):
        s = st[t]
        mem_probs, mem_vals = s.pop('mem_probs'), s.pop('mem_vals')
        if sample:
            o = jnp.einsum('bqk,bkd->bqd', mem_probs[0], mem_vals[0], preferred_element_type=F32)
            cols = [o[:, hd * SUBLANES:(hd + 1) * SUBLANES, hd * MEM_HEAD_DIM:(hd + 1) * MEM_HEAD_DIM]
                    .reshape(rows, MEM_HEAD_DIM) for hd in range(MEM_HEADS)]
        else:
            cols = [_dot(mem_probs[hd], mem_vals[hd]) for hd in range(MEM_HEADS)]
        s['memo'] = jnp.concatenate(cols, axis=1).astype(BF16)

    def merge(t):
        s = st[t]
        gates(t, len(s['gate_pieces']))
        gate_sigmoids(t)
        mixed = None
        for n, name in enumerate(('attn', 'sg', 'memo')):
            proj = _dot(s.pop(name), w_o_ref[n * ATTN_Q:(n + 1) * ATTN_Q, :])
            term = gate_ref[n, rows_of(t), :] * proj
            mixed = term if mixed is None else mixed + term
        x1_ref[rows_of(t), :] = s.pop('x') + _rms(mixed, g_post_ref[...])

    stages = {'head': head, 'inproj': in_proj, 'rope': rope_stage, 'scores': scores, 'gelu': gelu_ln,
              'softmax': softmax, 'values': values, 'outs': branch_outs, 'memv': mem_values, 'merge': merge,
              'sig': gate_sigmoids, 'pv': attn_values, 'memsc': mem_scores, 'sgmix': spatial_mix,
              'memsm': mem_softmax, 'attnout': attn_out, 'sgout': spatial_out}
    def run(tok):
        name, t = tok[:-1], ord(tok[-1]) - ord('A')
        if name.startswith('gates'):
            gates(t, int(name[len('gates'):]))
        elif name[-1].isdigit():
            stages[name[:-1]](t, int(name[-1]))
        else:
            stages[name](t)

    return init, run


def _mix_kernel_tiles(*refs, sample, schedule):
    init, run = _mix_program(refs, sample, _sub_tiles(schedule))
    init()
    for tok in schedule.split():
        run(tok)


def _ffn_program(refs, sample):
    if sample:
        (x_ref, g_pre_ref, w_up_ref, cw_ref, cb_ref, w_down_ref, g_post_ref, st_ref,
         y_ref, nc_ref, *up_s) = refs
    else:
        (x_ref, g_pre_ref, w_up_ref, cw_ref, cb_ref, w_down_ref, g_post_ref,
         y_ref, nc_ref, *up_s) = refs
    rows = x_ref.shape[0]
    step = pl.program_id(0)
    st = {}

    def init():
        if not sample:
            @pl.when(step == 0)
            def _():
                for slab in up_s:
                    slab[:SUBLANES, :] = jnp.zeros((SUBLANES, LANES), F32)

    def head():
        st['x'] = x_ref[...]
        st['h'] = _rms(st['x'], g_pre_ref[...]).astype(BF16)
        st['up_pieces'] = [(half * D_FF + n0, min(FFN_UP_CHUNK, D_FF - n0))
                           for n0 in range(0, D_FF, FFN_UP_CHUNK) for half in range(2)]
        st['chunks'] = [(c0, min(FFN_CHUNK, D_FF - c0)) for c0 in range(0, D_FF, FFN_CHUNK)]
        st['f'] = None

    hist = CONV_WIDTH - 1
    seqs = rows // SUBLANES

    def up_cols(col0, width):
        up = jnp.dot(st['h'], w_up_ref[:, col0:col0 + width], preferred_element_type=F32)
        for t in range(width // LANES):
            s = col0 // LANES + t
            blk = up[:, t * LANES:(t + 1) * LANES]
            if sample:
                up_s[s][:, SUBLANES:, :] = blk.reshape(seqs, SUBLANES, LANES)
                up_s[s][:, SUBLANES - hist:SUBLANES, :] = st_ref[:, :, s * LANES:(s + 1) * LANES]
            else:
                up_s[s][SUBLANES:, :] = blk

    def conv_slab(s):
        cols = slice(s * LANES, (s + 1) * LANES)
        if sample:
            taps = [up_s[s][:, SUBLANES - hist + j:2 * SUBLANES - hist + j, :] for j in range(CONV_WIDTH)]
            nc_ref[:, :, cols] = up_s[s][:, 2 * SUBLANES - hist:, :]
        else:
            taps = [up_s[s][SUBLANES - hist + j:SUBLANES - hist + j + rows, :] for j in range(CONV_WIDTH)]
            nc_ref[:, cols] = up_s[s][SUBLANES + rows - hist:, :]
            up_s[s][:SUBLANES, :] = up_s[s][rows:, :]
        c = cb_ref[:, cols]
        for j in range(CONV_WIDTH):
            c = c + taps[j] * cw_ref[j:j + 1, cols]
        return c.reshape(rows, LANES)

    def up(count):
        for _ in range(min(count, len(st['up_pieces']))):
            up_cols(*st['up_pieces'].pop(0))

    def conv_down(count):
        for _ in range(min(count, len(st['chunks']))):
            c0, width = st['chunks'].pop(0)
            acts = []
            for t in range(width // LANES):
                gate = conv_slab(c0 // LANES + t)
                val = conv_slab((D_FF + c0) // LANES + t)
                acts.append((_gelu_tanh(gate) * val).astype(BF16))
            act = jnp.concatenate(acts, axis=1)
            part = jnp.dot(act, w_down_ref[c0:c0 + width, :], preferred_element_type=F32)
            st['f'] = part if st['f'] is None else st['f'] + part

    def tail():
        up(len(st['up_pieces']))
        conv_down(len(st['chunks']))
        y_ref[...] = st.pop('x') + _rms(st.pop('f'), g_post_ref[...])

    def run(tok):
        if tok == 'head':
            head()
        elif tok == 'tail':
            tail()
        elif tok.startswith('up'):
            up(int(tok[2:]))
        else:
            assert tok.startswith('vd'), tok
            conv_down(int(tok[2:]))

    return init, run


def _ffn_kernel(*refs, sample):
    init, run = _ffn_program(refs, sample)
    init()
    for tok in FFN_SCHEDULE.split():
        run(tok)


def _memkv_kernel(mem_ref, g_ref, w_ref, k_ref, v_ref, kb_ref, vb_ref):
    h = _rms(mem_ref[...], g_ref[...]).astype(BF16)
    kv = _dot(h, w_ref[...].astype(BF16))
    kb_ref[...] = kv[:, :MEM_Q].astype(BF16)
    vb_ref[...] = kv[:, MEM_Q:].astype(BF16)
    for hd in range(MEM_HEADS):
        k_ref[pl.ds(hd, N_MEM, stride=MEM_HEADS), :] = kv[:, hd * MEM_HEAD_DIM:(hd + 1) * MEM_HEAD_DIM]
        v_ref[pl.ds(hd, N_MEM, stride=MEM_HEADS), :] = kv[:, MEM_Q + hd * MEM_HEAD_DIM:
                                                          MEM_Q + (hd + 1) * MEM_HEAD_DIM]


def _resident():
    return pl.BlockSpec(memory_space=pltpu.VMEM)


def _row_spec(rows, width):
    return pl.BlockSpec((rows, width), lambda i: (i, 0))


def _const_spec(shape):
    nd = len(shape)
    return pl.BlockSpec(shape, lambda i: (0,) * nd)


def _compiler_params():
    return pltpu.CompilerParams(dimension_semantics=("arbitrary",), vmem_limit_bytes=VMEM_LIMIT_BYTES)


def _rope_tables(pos):
    half = HEAD_DIM // 2
    inv_freq = np.float64(ROPE_THETA) ** (-np.arange(half, dtype=np.float64) / half)
    ang = pos.astype(np.float64)[:, None] * inv_freq[None, :]
    cos = np.concatenate([np.cos(ang)] * (LANES // half), axis=1).astype(np.float32)
    sin = np.concatenate([np.sin(ang)] * (LANES // half), axis=1).astype(np.float32)
    sign = np.concatenate([-np.ones((1, half), np.float32), np.ones((1, half), np.float32)]
                          * (LANES // HEAD_DIM), axis=1)
    return cos, sin, cos * sign, sin * sign


def _mix_call(x2d, start, seq_len, lp, mem_k, mem_v, caches):
    n_rows = x2d.shape[0]
    sample = caches is not None
    rows = SAMPLE_SEQS * seq_len if sample else PROMPT_MIX_ROWS
    schedule = MIX_SCHEDULE_ONE if sample else MIX_SCHEDULE_TWO
    steps = n_rows // rows
    if sample:
        pos = start + (np.arange(rows, dtype=np.int32) % seq_len)
        cos, _, _, sin_signed = _rope_tables(pos)
        rope_args = [cos, sin_signed]
        rope_specs = [_const_spec((rows, LANES))] * 2
        chunk = rows
        wmix = jnp.tile(lp['sg_w'][:, :seq_len, :seq_len], (1, rows // seq_len, rows // seq_len))
        sgb = jnp.repeat(jnp.tile(lp['sg_b'][:, :seq_len], (1, rows // seq_len)).T, SG_GROUP_DIM, axis=1)
    else:
        in_step = _rope_tables(np.arange(rows, dtype=np.int32))
        cos_b, sin_b, _, _ = _rope_tables(start + rows * np.arange(steps, dtype=np.int32))
        rope_args = list(in_step) + [cos_b, sin_b]
        rope_specs = [_const_spec((rows, LANES))] * 4 + [_const_spec((steps, LANES))] * 2
        chunk = CHUNK
        wmix = lp['sg_w'][:, :chunk, :chunk]
        sgb = jnp.repeat(lp['sg_b'][:, :chunk].T, SG_GROUP_DIM, axis=1)

    in_specs = [
        pl.BlockSpec(memory_space=pltpu.SMEM),
        _row_spec(rows, D_MODEL),
        *rope_specs,
        _const_spec((1, D_MODEL)),
        _resident(),
        _const_spec((1, SG_WIDTH)), _const_spec((1, SG_WIDTH)),
        _const_spec((SG_GROUPS, chunk, chunk)),
        _const_spec((chunk, SG_WIDTH)),
    ]
    args = [lp['sinks'], x2d, *rope_args, lp['pre_mix_g'].reshape(1, D_MODEL), lp['w_in'],
            lp['sg_ln_g'].reshape(1, SG_WIDTH), lp['sg_ln_b'].reshape(1, SG_WIDTH), wmix, sgb]
    if sample:
        seqs = SAMPLE_SEQS
        in_specs += [pl.BlockSpec(memory_space=pl.ANY)] * 2
    else:
        in_specs += [_const_spec((N_MEM, MEM_Q))] * 2
    args += [mem_k, mem_v]
    in_specs += [_resident(), _const_spec((1, D_MODEL))]
    args += [lp['w_o'], lp['post_mix_g'].reshape(1, D_MODEL)]

    out_shape = [jax.ShapeDtypeStruct((n_rows, D_MODEL), F32)]
    out_specs = [_row_spec(rows, D_MODEL)]
    scratch = []
    if sample:
        win_k, win_v = caches
        n_seq, past = win_k.shape[0], win_k.shape[1]
        cache_spec = pl.BlockSpec((SAMPLE_SEQS, past, ATTN_KV), lambda i: (i, 0, 0))
        in_specs += [cache_spec, cache_spec]
        args += [win_k, win_v]
        out_shape += [jax.ShapeDtypeStruct((n_seq, past, ATTN_KV), F32)] * 2
        out_specs += [cache_spec, cache_spec]
        out_shape += [jax.ShapeDtypeStruct((n_rows, SG_WIDTH), F32)]
        out_specs += [_row_spec(rows, SG_WIDTH)]
    else:
        out_shape += [jax.ShapeDtypeStruct((WINDOW, ATTN_KV), F32)] * 2
        out_specs += [_const_spec((WINDOW, ATTN_KV))] * 2
        scratch = [pltpu.VMEM((WINDOW, ATTN_KV), F32), pltpu.VMEM((WINDOW, ATTN_KV), F32)]
    scratch += [pltpu.VMEM((N_BRANCHES, rows, D_MODEL), F32)]
    if sample:
        ring = (MEM_RING_SLOTS, SAMPLE_SEQS, N_MEM * MEM_HEADS, MEM_HEAD_DIM)
        scratch += [pltpu.VMEM(ring, F32), pltpu.VMEM(ring, F32), pltpu.SemaphoreType.DMA((2, MEM_RING_SLOTS))]

    return pl.pallas_call(
        functools.partial(_mix_kernel_tiles, sample=sample, schedule=schedule),
        grid=(steps,),
        in_specs=in_specs,
        out_specs=out_specs,
        out_shape=out_shape,
        scratch_shapes=scratch,
        compiler_params=_compiler_params(),
        name="mix_sample" if sample else "mix_prompt",
    )(*args)


def _ffn_call(x2d, seq_len, lp, conv_state):
    n_rows = x2d.shape[0]
    sample = conv_state is not None
    rows = SAMPLE_FFN_SEQS * seq_len if sample else PROMPT_ROWS
    steps = n_rows // rows
    in_specs = [
        _row_spec(rows, D_MODEL),
        _const_spec((1, D_MODEL)),
        _resident(),
        _const_spec((CONV_WIDTH, 2 * D_FF)),
        _const_spec((1, 2 * D_FF)),
        _resident(),
        _const_spec((1, D_MODEL)),
    ]
    args = [x2d, lp['pre_ffn_g'].reshape(1, D_MODEL), lp['w_up'], lp['conv_w'],
            lp['conv_b'].reshape(1, 2 * D_FF), lp['w_down'], lp['post_ffn_g'].reshape(1, D_MODEL)]
    out_shape = [jax.ShapeDtypeStruct((n_rows, D_MODEL), F32)]
    out_specs = [_row_spec(rows, D_MODEL)]
    n_slabs = 2 * D_FF // LANES
    if sample:
        n_seq = conv_state.shape[0]
        st_spec = pl.BlockSpec((SAMPLE_FFN_SEQS, CONV_WIDTH - 1, 2 * D_FF), lambda i: (i, 0, 0))
        in_specs += [st_spec]
        args += [conv_state]
        out_shape += [jax.ShapeDtypeStruct((n_seq, CONV_WIDTH - 1, 2 * D_FF), F32)]
        out_specs += [st_spec]
        scratch = [pltpu.VMEM((SAMPLE_FFN_SEQS, 2 * SUBLANES, LANES), F32)] * n_slabs
    else:
        out_shape += [jax.ShapeDtypeStruct((CONV_WIDTH - 1, 2 * D_FF), F32)]
        out_specs += [_const_spec((CONV_WIDTH - 1, 2 * D_FF))]
        scratch = [pltpu.VMEM((SUBLANES + rows, LANES), F32)] * n_slabs
    return pl.pallas_call(
        functools.partial(_ffn_kernel, sample=sample),
        grid=(steps,),
        in_specs=in_specs,
        out_specs=out_specs,
        out_shape=out_shape,
        scratch_shapes=scratch,
        compiler_params=_compiler_params(),
        name="ffn_sample" if sample else "ffn_prompt",
    )(*args)


def _memkv_call(mem2d, g, w_f32):
    return pl.pallas_call(
        _memkv_kernel,
        out_shape=[jax.ShapeDtypeStruct((mem2d.shape[0] * MEM_HEADS, MEM_HEAD_DIM), F32)] * 2
        + [jax.ShapeDtypeStruct((mem2d.shape[0], MEM_Q), BF16)] * 2,
        compiler_params=pltpu.CompilerParams(vmem_limit_bytes=VMEM_LIMIT_BYTES),
        name="mem_kv",
    )(mem2d, g.reshape(1, D_MODEL), w_f32)


def kernel(x_prompt, x_sample, cache_win_k, cache_win_v, cache_mem_k, cache_mem_v, state_conv, mem_prompt,
           pre_mix_g, w_in, attn_sinks, sg_ln_g, sg_ln_b, sg_w, sg_b, mem_norm_g, w_mem_kv, w_o,
           post_mix_g, pre_ffn_g, w_up, conv_w, conv_b, w_down, post_ffn_g):
    depth = w_in.shape[0]
    batch, seq, _ = x_prompt.shape
    dec_batch, dec_seq, _ = x_sample.shape
    past_len = PAST_LEN
    assert batch == 1 and depth == 1

    yp = x_prompt.reshape(batch * seq, D_MODEL)
    ys = x_sample.reshape(dec_batch * dec_seq, D_MODEL)
    outs = {name: [] for name in ('wk_p', 'wv_p', 'mk_p', 'mv_p', 'cv_p', 'wk_s', 'wv_s', 'sgv_s', 'cv_s')}
    for l in range(depth):
        lp = {
            'sinks': attn_sinks[l], 'pre_mix_g': pre_mix_g[l], 'w_in': w_in[l].astype(BF16),
            'sg_ln_g': sg_ln_g[l], 'sg_ln_b': sg_ln_b[l], 'sg_w': sg_w[l], 'sg_b': sg_b[l],
            'w_o': w_o[l].reshape(N_BRANCHES * ATTN_Q, D_MODEL).astype(BF16), 'post_mix_g': post_mix_g[l],
            'pre_ffn_g': pre_ffn_g[l], 'w_up': w_up[l].astype(BF16), 'conv_w': conv_w[l], 'conv_b': conv_b[l],
            'w_down': w_down[l].astype(BF16), 'post_ffn_g': post_ffn_g[l],
        }
        mem_k, mem_v, mem_k_b, mem_v_b = _memkv_call(mem_prompt.reshape(batch * N_MEM, D_MODEL), mem_norm_g[l],
                                                     w_mem_kv[l])
        outs['mk_p'].append(mem_k.reshape(batch, N_MEM, MEM_HEADS, MEM_HEAD_DIM))
        outs['mv_p'].append(mem_v.reshape(batch, N_MEM, MEM_HEADS, MEM_HEAD_DIM))

        x1, wk, wv = _mix_call(yp, 0, seq, lp, mem_k_b, mem_v_b, None)
        yp, nc = _ffn_call(x1, seq, lp, None)
        outs['wk_p'].append(wk.reshape(batch, N_KV_HEADS, HEAD_DIM, WINDOW).transpose(0, 3, 1, 2))
        outs['wv_p'].append(wv.reshape(batch, N_KV_HEADS, HEAD_DIM, WINDOW).transpose(0, 3, 1, 2))
        outs['cv_p'].append(nc.reshape(batch, CONV_WIDTH - 1, 2 * D_FF))

        past = cache_win_k.shape[2]
        x1s, wks, wvs, sgv = _mix_call(
            ys, past_len, dec_seq, lp,
            cache_mem_k[l].reshape(dec_batch, N_MEM * MEM_HEADS, MEM_HEAD_DIM),
            cache_mem_v[l].reshape(dec_batch, N_MEM * MEM_HEADS, MEM_HEAD_DIM),
            (cache_win_k[l].reshape(dec_batch, past, ATTN_KV), cache_win_v[l].reshape(dec_batch, past, ATTN_KV)))
        ys, ncs = _ffn_call(x1s, dec_seq, lp, state_conv[l])
        outs['wk_s'].append(wks.reshape(dec_batch, past, N_KV_HEADS, HEAD_DIM))
        outs['wv_s'].append(wvs.reshape(dec_batch, past, N_KV_HEADS, HEAD_DIM))
        outs['sgv_s'].append(sgv.reshape(dec_batch, dec_seq, SG_WIDTH))
        outs['cv_s'].append(ncs)

    return (yp.reshape(batch, seq, D_MODEL), ys.reshape(dec_batch, dec_seq, D_MODEL),
            jnp.stack(outs['wk_p']), jnp.stack(outs['wv_p']), jnp.stack(outs['mk_p']), jnp.stack(outs['mv_p']),
            jnp.stack(outs['cv_p']),
            jnp.stack(outs['wk_s']), jnp.stack(outs['wv_s']), jnp.stack(outs['sgv_s']), jnp.stack(outs['cv_s']))
```
